```python
import jax, jax.numpy as jnp
from jax import lax
import numpy as np

D_MODEL = 1024
BATCH = 2
SEQ = 8192
DEPTH = 1

HEAD_DIM = 64
A_Q_HEADS = 8
A_KV_HEADS = 2
B_Q_HEADS = 8
B_KV_HEADS = 2
A_Q = A_Q_HEADS * HEAD_DIM
A_KV = A_KV_HEADS * HEAD_DIM
B_Q = B_Q_HEADS * HEAD_DIM
B_KV = B_KV_HEADS * HEAD_DIM
IN_COLS = A_Q + 2 * A_KV + B_Q + 2 * B_KV + 2 * D_MODEL
Q_BLOCK = 128
WINDOW = 128
GRID_W = 64
ROPE_THETA = 10000.0
N_EXPERTS = 32
TOP_K = 4
D_FF = 1024
SWIGLU_LIMIT = 7.0
SWIGLU_ALPHA = 1.702
MOE_BLOCK = 128
EPS = 1e-6
NEG_INF = -1e30

kernel_name = "hybrid_gated_gqa_window_sink_moe_encoder"


def rmsnorm(x, g):
    xf = x.astype(jnp.float32)
    y = xf * lax.rsqrt(jnp.mean(xf * xf, axis=-1, keepdims=True) + EPS)
    return (y * g.astype(jnp.float32)).astype(x.dtype)


def rope_rotate(x, cos, sin):
    half = x.shape[-1] // 2
    xf = x.astype(jnp.float32)
    x1, x2 = xf[..., :half], xf[..., half:]
    c = cos[:, None, :]
    s = sin[:, None, :]
    return jnp.concatenate([x1 * c - x2 * s, x2 * c + x1 * s], axis=-1).astype(x.dtype)


def rope_1d_tables(seq):
    pos = jnp.arange(seq, dtype=jnp.float32)
    inv = ROPE_THETA ** (-jnp.arange(0, HEAD_DIM, 2, dtype=jnp.float32) / HEAD_DIM)
    ang = pos[:, None] * inv[None, :]
    return jnp.cos(ang), jnp.sin(ang)


def rope_axial_tables(seq):
    rows = seq // GRID_W
    row = jnp.broadcast_to(jnp.arange(rows, dtype=jnp.float32)[:, None], (rows, GRID_W)).reshape(-1)
    col = jnp.broadcast_to(jnp.arange(GRID_W, dtype=jnp.float32)[None, :], (rows, GRID_W)).reshape(-1)
    axis_dim = HEAD_DIM // 2
    inv = ROPE_THETA ** (-jnp.arange(0, axis_dim, 2, dtype=jnp.float32) / axis_dim)
    ang = jnp.concatenate([row[:, None] * inv[None, :], col[:, None] * inv[None, :]], axis=-1)
    return jnp.cos(ang), jnp.sin(ang)


def global_axial_gqa(q, k, v, q_gain, k_gain, cos, sin):
    bsz, seq = q.shape[0], q.shape[1]
    q = rope_rotate(rmsnorm(q, q_gain), cos, sin)
    k = rope_rotate(rmsnorm(k, k_gain), cos, sin)
    groups = A_Q_HEADS // A_KV_HEADS
    nb = seq // Q_BLOCK
    scale = HEAD_DIM ** -0.5
    qb = q.reshape(bsz, nb, Q_BLOCK, A_KV_HEADS, groups, HEAD_DIM).transpose(1, 0, 2, 3, 4, 5)

    def one_block(qblk):
        s = jnp.einsum('bqkgd,bskd->bkgqs', qblk, k, preferred_element_type=jnp.float32) * scale
        p = jax.nn.softmax(s, axis=-1).astype(v.dtype)
        return jnp.einsum('bkgqs,bskd->bqkgd', p, v)

    o = lax.map(one_block, qb)
    return o.transpose(1, 0, 2, 3, 4, 5).reshape(bsz, seq, A_Q)


def window_sink_gqa(q, k, v, sink, cos, sin):
    bsz, seq = q.shape[0], q.shape[1]
    q = rope_rotate(q, cos, sin)
    k = rope_rotate(k, cos, sin)
    groups = B_Q_HEADS // B_KV_HEADS
    nb = seq // Q_BLOCK
    span = Q_BLOCK + 2 * WINDOW
    scale = HEAD_DIM ** -0.5
    qb = q.reshape(bsz, nb, Q_BLOCK, B_KV_HEADS, groups, HEAD_DIM)
    pad = ((0, 0), (WINDOW, WINDOW), (0, 0), (0, 0))
    kpad = jnp.pad(k, pad)
    vpad = jnp.pad(v, pad)
    kidx = jnp.arange(nb)[:, None] * Q_BLOCK + jnp.arange(span)[None, :]
    kwin = kpad[:, kidx]
    vwin = vpad[:, kidx]
    n = jnp.arange(nb)[:, None, None]
    qpos = n * Q_BLOCK + jnp.arange(Q_BLOCK)[None, :, None]
    kpos = n * Q_BLOCK - WINDOW + jnp.arange(span)[None, None, :]
    mask = (jnp.abs(kpos - qpos) <= WINDOW) & (kpos >= 0) & (kpos < seq)
    s = jnp.einsum('bnqkgd,bnjkd->bnkgqj', qb, kwin, preferred_element_type=jnp.float32) * scale
    s = jnp.where(mask[None, :, None, None], s, NEG_INF)
    sink_logit = jnp.broadcast_to(
        sink.astype(jnp.float32).reshape(1, 1, B_KV_HEADS, groups, 1, 1), s.shape[:-1] + (1,))
    p = jax.nn.softmax(jnp.concatenate([s, sink_logit], axis=-1), axis=-1)[..., :-1]
    o = jnp.einsum('bnkgqj,bnjkd->bnqkgd', p.astype(v.dtype), vwin)
    return o.reshape(bsz, seq, B_Q)


def clamped_swiglu(gu):
    gate, up = gu[..., :D_FF], gu[..., D_FF:]
    gate = jnp.minimum(gate, SWIGLU_LIMIT)
    up = jnp.clip(up, -SWIGLU_LIMIT, SWIGLU_LIMIT)
    return (up + 1.0) * (gate * jax.nn.sigmoid(SWIGLU_ALPHA * gate))


def moe(h, w_router, b_router, w_gate_up, b_gate_up, w_down, b_down):
    bsz, seq, d = h.shape
    T = bsz * seq
    hf = h.reshape(T, d)
    logits = jnp.einsum('td,de->te', hf, w_router, preferred_element_type=jnp.float32) \
        + b_router.astype(jnp.float32)
    top_vals, top_idx = lax.top_k(logits, TOP_K)
    gate = jax.nn.softmax(top_vals, axis=-1)
    A = T * TOP_K
    e_flat = top_idx.reshape(A)
    tok_flat = jnp.arange(A, dtype=jnp.int32) // TOP_K
    g_flat = gate.reshape(A)
    order = jnp.argsort(e_flat, stable=True)
    e_sorted = e_flat[order]
    counts = jnp.bincount(e_flat, length=N_EXPERTS)
    padded = (counts + MOE_BLOCK - 1) // MOE_BLOCK * MOE_BLOCK
    ends_p = jnp.cumsum(padded)
    starts_p = ends_p - padded
    starts = jnp.cumsum(counts) - counts
    rank = jnp.arange(A, dtype=jnp.int32) - starts[e_sorted]
    dest = starts_p[e_sorted] + rank
    n_blocks = -(-A // MOE_BLOCK) + N_EXPERTS
    R = n_blocks * MOE_BLOCK
    row_tok = jnp.full((R,), T, jnp.int32).at[dest].set(tok_flat[order])
    row_gate = jnp.zeros((R,), jnp.float32).at[dest].set(g_flat[order])
    block_exp = jnp.minimum(
        jnp.searchsorted(ends_p, jnp.arange(n_blocks) * MOE_BLOCK, side='right'), N_EXPERTS - 1)
    h_pad = jnp.concatenate([hf, jnp.zeros((1, d), hf.dtype)], axis=0)
    xs = h_pad[row_tok].reshape(n_blocks, MOE_BLOCK, d)

    def run_block(args):
        xb, e = args
        gu = xb @ w_gate_up[e] + b_gate_up[e]
        return clamped_swiglu(gu) @ w_down[e] + b_down[e]

    ys = lax.map(run_block, (xs, block_exp)).reshape(R, d)
    ys = ys.astype(jnp.float32) * row_gate[:, None]
    out = jax.ops.segment_sum(ys, row_tok, num_segments=T + 1)[:T]
    return out.astype(h.dtype).reshape(bsz, seq, d)


def setup_inputs(seed: int = 0) -> dict:
    key = jax.random.key(seed)
    ks = jax.random.split(key, 20)
    f32 = jnp.float32
    nrm = lambda k, shape: jax.random.normal(k, shape, dtype=f32)
    L, D, E, F = DEPTH, D_MODEL, N_EXPERTS, D_FF
    return {
        "x": nrm(ks[0], (BATCH, SEQ, D)),
        "g_mix": 1.0 + 0.02 * nrm(ks[1], (L, D)),
        "w_in": nrm(ks[2], (L, D, IN_COLS)) * D ** -0.5,
        "qa_norm": 1.0 + 0.02 * nrm(ks[3], (L, HEAD_DIM)),
        "ka_norm": 1.0 + 0.02 * nrm(ks[4], (L, HEAD_DIM)),
        "sink_b": 0.5 * nrm(ks[5], (L, B_Q_HEADS)),
        "w_branch_a": nrm(ks[6], (L, A_Q, D)) * A_Q ** -0.5,
        "w_branch_b": nrm(ks[7], (L, B_Q, D)) * B_Q ** -0.5,
        "w_out": nrm(ks[8], (L, D, D)) * D ** -0.5,
        "g_ffn": 1.0 + 0.02 * nrm(ks[9], (L, D)),
        "w_router": nrm(ks[10], (L, D, E)) * D ** -0.5,
        "b_router": 0.01 * nrm(ks[11], (L, E)),
        "w_gate_up": nrm(ks[12], (L, E, D, 2 * F)) * D ** -0.5,
        "b_gate_up": 0.01 * nrm(ks[13], (L, E, 2 * F)),
        "w_down": nrm(ks[14], (L, E, F, D)) * F ** -0.5,
        "b_down": 0.01 * nrm(ks[15], (L, E, D)),
        "g_final": 1.0 + 0.02 * nrm(ks[16], (D,)),
    }


def reference(x, g_mix, w_in, qa_norm, ka_norm, sink_b, w_branch_a, w_branch_b, w_out,
              g_ffn, w_router, b_router, w_gate_up, b_gate_up, w_down, b_down, g_final):
    bsz, seq, _ = x.shape
    cos_ax, sin_ax = rope_axial_tables(seq)
    cos_1d, sin_1d = rope_1d_tables(seq)
    splits = np.cumsum([A_Q, A_KV, A_KV, B_Q, B_KV, B_KV, D_MODEL]).tolist()
    for l in range(DEPTH):
        h = rmsnorm(x, g_mix[l])
        proj = h @ w_in[l]
        qa, ka, va, qb, kb, vb, gate_a, gate_b = jnp.split(proj, splits, axis=-1)
        qa = qa.reshape(bsz, seq, A_Q_HEADS, HEAD_DIM)
        ka = ka.reshape(bsz, seq, A_KV_HEADS, HEAD_DIM)
        va = va.reshape(bsz, seq, A_KV_HEADS, HEAD_DIM)
        qb = qb.reshape(bsz, seq, B_Q_HEADS, HEAD_DIM)
        kb = kb.reshape(bsz, seq, B_KV_HEADS, HEAD_DIM)
        vb = vb.reshape(bsz, seq, B_KV_HEADS, HEAD_DIM)
        oa = global_axial_gqa(qa, ka, va, qa_norm[l], ka_norm[l], cos_ax, sin_ax)
        ob = window_sink_gqa(qb, kb, vb, sink_b[l], cos_1d, sin_1d)
        merged = jax.nn.sigmoid(gate_a) * (oa @ w_branch_a[l]) \
            + jax.nn.sigmoid(gate_b) * (ob @ w_branch_b[l])
        x = x + merged @ w_out[l]
        h2 = rmsnorm(x, g_ffn[l])
        x = x + moe(h2, w_router[l], b_router[l], w_gate_up[l], b_gate_up[l], w_down[l], b_down[l])
    return rmsnorm(x, g_final)
```

```python
import functools

import jax
import jax.numpy as jnp
from jax import lax
from jax.experimental import pallas as pl
from jax.experimental.pallas import tpu as pltpu

HEAD_DIM = 64
Q_HEADS = 8
KV_HEADS = 2
GROUPS = Q_HEADS // KV_HEADS
Q_COLS = Q_HEADS * HEAD_DIM
KV_COLS = KV_HEADS * HEAD_DIM
GROUP_COLS = GROUPS * HEAD_DIM
WINDOW = 128
GRID_W = 64
ROPE_THETA = 10000.0
N_EXPERTS = 32
TOP_K = 4
D_FF = 1024
SWIGLU_LIMIT = 7.0
SWIGLU_ALPHA = 1.702
EPS = 1e-6
NEG_INF = -1e30

LANES = 128
SUBLANES = 8
V_ROWS = 80
VMEM_LIMIT = 56 * 1024 * 1024

F32 = jnp.float32
BF16 = jnp.bfloat16


def _cparams(sem, vmem=VMEM_LIMIT):
    return pltpu.CompilerParams(dimension_semantics=sem, vmem_limit_bytes=vmem)


def _rope(y, cos, sin_signed):
    lane = lax.broadcasted_iota(jnp.int32, y.shape, 1)
    first_half = (lane & 32) == 0
    rot = jnp.where(first_half, pltpu.roll(y, 96, 1), pltpu.roll(y, 32, 1))
    return y * cos + rot * sin_signed


def _head_rmsnorm(q, ones_blockdiag, gain):
    q2 = q * q
    hi = q2.astype(BF16)
    lo = (q2 - hi.astype(F32)).astype(BF16)
    ss = (jnp.dot(hi, ones_blockdiag, preferred_element_type=F32)
          + jnp.dot(lo, ones_blockdiag, preferred_element_type=F32))
    return q * lax.rsqrt(ss * (1.0 / HEAD_DIM) + EPS) * gain


def _proj_kernel(x_ref, g_ref, w_ref, cosa_ref, sina_ref, cos1_ref, sin1_ref, qg_ref, kg_ref, bd_ref,
                 qa_ref, ka_ref, va_ref, qb_ref, kb_ref, vb_ref, sga_ref, sgb_ref, *, d_model):
    x = x_ref[...]
    ms = jnp.mean(x * x, axis=-1, keepdims=True)
    h = (x * lax.rsqrt(ms + EPS) * g_ref[...]).astype(BF16)
    scale = HEAD_DIM ** -0.5

    def proj(c0, width):
        return jnp.dot(h, w_ref[:, c0:c0 + width], preferred_element_type=F32)

    cosa, sina = cosa_ref[...], sina_ref[...]
    cos1, sin1 = cos1_ref[...], sin1_ref[...]
    bd = bd_ref[...]
    c = 0
    for s in range(Q_COLS // LANES):
        q = _head_rmsnorm(proj(c, LANES), bd, qg_ref[...])
        qa_ref[:, s * LANES:(s + 1) * LANES] = (_rope(q, cosa, sina) * scale).astype(BF16)
        c += LANES
    k = _head_rmsnorm(proj(c, LANES), bd, kg_ref[...])
    ka_ref[...] = _rope(k, cosa, sina).astype(BF16)
    c += LANES
    va_ref[...] = proj(c, LANES).astype(BF16)
    c += LANES
    for s in range(Q_COLS // LANES):
        qb_ref[:, s * LANES:(s + 1) * LANES] = (_rope(proj(c, LANES), cos1, sin1) * scale).astype(BF16)
        c += LANES
    kb_ref[...] = _rope(proj(c, LANES), cos1, sin1).astype(BF16)
    c += LANES
    vb_ref[...] = proj(c, LANES).astype(BF16)
    c += LANES
    sga_ref[...] = jax.nn.sigmoid(proj(c, d_model)).astype(BF16)
    c += d_model
    sgb_ref[...] = jax.nn.sigmoid(proj(c, d_model)).astype(BF16)


def _rope_tables(seq):
    def slab(ang):
        cos, sin = jnp.cos(ang), jnp.sin(ang)
        cos64 = jnp.concatenate([cos, cos], axis=-1)
        sin64 = jnp.concatenate([-sin, sin], axis=-1)
        return jnp.tile(cos64, (1, 2)), jnp.tile(sin64, (1, 2))

    pos = jnp.arange(seq, dtype=F32)
    inv1 = ROPE_THETA ** (-jnp.arange(0, HEAD_DIM, 2, dtype=F32) / HEAD_DIM)
    ang1 = pos[:, None] * inv1[None, :]
    rows = seq // GRID_W
    row = jnp.broadcast_to(jnp.arange(rows, dtype=F32)[:, None], (rows, GRID_W)).reshape(-1)
    col = jnp.broadcast_to(jnp.arange(GRID_W, dtype=F32)[None, :], (rows, GRID_W)).reshape(-1)
    axis_dim = HEAD_DIM // 2
    inva = ROPE_THETA ** (-jnp.arange(0, axis_dim, 2, dtype=F32) / axis_dim)
    anga = jnp.concatenate([row[:, None] * inva[None, :], col[:, None] * inva[None, :]], axis=-1)
    return slab(anga), slab(ang1)


def _project(x2d, g_mix, w_in, qa_norm, ka_norm, seq, tm):
    tokens, d_model = x2d.shape
    in_cols = w_in.shape[1]
    (cosa, sina), (cos1, sin1) = _rope_tables(seq)
    head_id = jnp.arange(LANES) // HEAD_DIM
    ones_blockdiag = (head_id[:, None] == head_id[None, :]).astype(BF16)
    qg = jnp.tile(qa_norm.astype(F32), 2)[None, :]
    kg = jnp.tile(ka_norm.astype(F32), 2)[None, :]
    seq_tiles = seq // tm
    row = lambda i: (i, 0)
    pos = lambda i: (i % seq_tiles, 0)
    fixed = lambda i: (0, 0)
    bf = lambda cols: jax.ShapeDtypeStruct((tokens, cols), BF16)
    out_cols = [Q_COLS, KV_COLS, KV_COLS, Q_COLS, KV_COLS, KV_COLS, d_model, d_model]
    return pl.pallas_call(
        functools.partial(_proj_kernel, d_model=d_model),
        grid=(tokens // tm,),
        in_specs=[
            pl.BlockSpec((tm, d_model), row),
            pl.BlockSpec((1, d_model), fixed),
            pl.BlockSpec((d_model, in_cols), fixed),
            pl.BlockSpec((tm, LANES), pos), pl.BlockSpec((tm, LANES), pos),
            pl.BlockSpec((tm, LANES), pos), pl.BlockSpec((tm, LANES), pos),
            pl.BlockSpec((1, LANES), fixed), pl.BlockSpec((1, LANES), fixed),
            pl.BlockSpec((LANES, LANES), fixed),
        ],
        out_specs=[pl.BlockSpec((tm, c), row) for c in out_cols],
        out_shape=[bf(c) for c in out_cols],
        compiler_params=_cparams(("arbitrary",)),
        name="proj",
    )(x2d, g_mix.astype(F32)[None, :], w_in.astype(BF16), cosa, sina, cos1, sin1, qg, kg, ones_blockdiag)


def _stack_group_queries(q):
    return jnp.concatenate([q[:, g * HEAD_DIM:(g + 1) * HEAD_DIM] for g in range(GROUPS)], axis=0)


def _unstack_group_outputs(o_t, tq):
    pad = jnp.zeros((LANES - HEAD_DIM, o_t.shape[1]), o_t.dtype)
    o_pad = jnp.concatenate([o_t, pad], axis=0)
    heads = [o_pad[:, g * tq:(g + 1) * tq].T[:, :HEAD_DIM] for g in range(GROUPS)]
    return jnp.concatenate(heads, axis=1)


def _attn_global_kernel(q_ref, k_ref, vt_ref, o_ref, m_ref, acc_ref, *, tq, tk, n_chunks):
    q_st = _stack_group_queries(q_ref[...])
    m_ref[...] = jnp.full(m_ref.shape, NEG_INF, F32)
    acc_ref[...] = jnp.zeros(acc_ref.shape, F32)

    def chunk(c, carry):
        k_c = k_ref[pl.ds(pl.multiple_of(c * tk, tk), tk), :]
        s = lax.dot_general(k_c, q_st, (((1,), (1,)), ((), ())), preferred_element_type=F32)
        m_old = m_ref[...]
        m_new = jnp.maximum(m_old, jnp.max(s, axis=0, keepdims=True))
        p = jnp.exp(s - m_new).astype(BF16)
        acc_ref[...] = acc_ref[...] * jnp.exp(m_old - m_new) + jnp.dot(
            vt_ref[c], p, preferred_element_type=F32)
        m_ref[...] = m_new
        return carry

    lax.fori_loop(0, n_chunks, chunk, 0)
    acc = acc_ref[...]
    o_t = acc[:HEAD_DIM, :] / acc[HEAD_DIM:HEAD_DIM + 1, :]
    o_ref[...] = _unstack_group_outputs(o_t, tq).astype(o_ref.dtype)


def _attn_window_kernel(q_ref, k_ref, vt_ref, sink_ref, o_ref, *, tq, n_blocks):
    n = pl.program_id(2)
    c0 = jnp.clip(n - 1, 0, n_blocks - 3)
    start = pl.multiple_of(c0 * tq, tq)
    span = 3 * tq
    q_st = _stack_group_queries(q_ref[...])
    k_w = k_ref[pl.ds(start, span), :]
    s = lax.dot_general(k_w, q_st, (((1,), (1,)), ((), ())), preferred_element_type=F32)
    kpos = start + lax.broadcasted_iota(jnp.int32, s.shape, 0)
    qpos = n * tq + (lax.broadcasted_iota(jnp.int32, s.shape, 1) & (tq - 1))
    s = jnp.where(jnp.abs(kpos - qpos) <= WINDOW, s, NEG_INF)
    sink = sink_ref[...]
    m = jnp.maximum(jnp.max(s, axis=0, keepdims=True), sink)
    p = jnp.exp(s - m).astype(BF16)
    acc = jnp.dot(vt_ref[c0], p[:tq], preferred_element_type=F32)
    for j in range(1, 3):
        acc += jnp.dot(vt_ref[c0 + j], p[j * tq:(j + 1) * tq], preferred_element_type=F32)
    denom = acc[HEAD_DIM:HEAD_DIM + 1, :] + jnp.exp(sink - m)
    o_ref[...] = _unstack_group_outputs(acc[:HEAD_DIM, :] / denom, tq).astype(o_ref.dtype)


def _kv_layouts(k2d, v2d, batch, seq, chunk):
    k_hm = k2d.reshape(batch, seq, KV_HEADS, HEAD_DIM).transpose(0, 2, 1, 3)
    v_t = v2d.reshape(batch, seq // chunk, chunk, KV_HEADS, HEAD_DIM).transpose(0, 3, 1, 4, 2)
    ones = jnp.ones(v_t.shape[:3] + (V_ROWS - HEAD_DIM, chunk), v_t.dtype)
    return k_hm, jnp.concatenate([v_t, ones], axis=3)


def _attention_global(q2d, k2d, v2d, batch, seq, tq, tk):
    tokens = q2d.shape[0]
    n_chunks = seq // tk
    k_hm, v_t = _kv_layouts(k2d, v2d, batch, seq, tk)
    q_tiles = seq // tq
    m_rows = GROUPS * tq
    qmap = lambda b, h, i: (b * q_tiles + i, h)
    return pl.pallas_call(
        functools.partial(_attn_global_kernel, tq=tq, tk=tk, n_chunks=n_chunks),
        grid=(batch, KV_HEADS, q_tiles),
        in_specs=[
            pl.BlockSpec((tq, GROUP_COLS), qmap),
            pl.BlockSpec((None, None, seq, HEAD_DIM), lambda b, h, i: (b, h, 0, 0)),
            pl.BlockSpec((None, None, n_chunks, V_ROWS, tk), lambda b, h, i: (b, h, 0, 0, 0)),
        ],
        out_specs=pl.BlockSpec((tq, GROUP_COLS), qmap),
        out_shape=jax.ShapeDtypeStruct((tokens, Q_COLS), BF16),
        scratch_shapes=[pltpu.VMEM((1, m_rows), F32), pltpu.VMEM((V_ROWS, m_rows), F32)],
        compiler_params=_cparams(("arbitrary", "arbitrary", "arbitrary")),
        name="attn_global",
    )(q2d, k_hm, v_t)


def _attention_window(q2d, k2d, v2d, sink, batch, seq):
    tokens = q2d.shape[0]
    tq = WINDOW
    n_blocks = seq // tq
    assert n_blocks >= 3
    k_hm, v_t = _kv_layouts(k2d, v2d, batch, seq, tq)
    m_rows = GROUPS * tq
    sink_rows = jnp.repeat(sink.astype(F32).reshape(KV_HEADS, GROUPS), tq, axis=1)[:, None, :]
    qmap = lambda b, h, i: (b * n_blocks + i, h)
    return pl.pallas_call(
        functools.partial(_attn_window_kernel, tq=tq, n_blocks=n_blocks),
        grid=(batch, KV_HEADS, n_blocks),
        in_specs=[
            pl.BlockSpec((tq, GROUP_COLS), qmap),
            pl.BlockSpec((None, None, seq, HEAD_DIM), lambda b, h, i: (b, h, 0, 0)),
            pl.BlockSpec((None, None, n_blocks, V_ROWS, tq), lambda b, h, i: (b, h, 0, 0, 0)),
            pl.BlockSpec((None, 1, m_rows), lambda b, h, i: (h, 0, 0)),
        ],
        out_specs=pl.BlockSpec((tq, GROUP_COLS), qmap),
        out_shape=jax.ShapeDtypeStruct((tokens, Q_COLS), BF16),
        compiler_params=_cparams(("arbitrary", "arbitrary", "arbitrary")),
        name="attn_window",
    )(q2d, k_hm, v_t, sink_rows)


def _store_token_tiles(ref, val):
    rows = val.shape[0]
    for j in range(val.shape[1] // LANES):
        ref[pl.ds(j, rows, stride=SUBLANES), :] = val[:, j * LANES:(j + 1) * LANES]


def _load_token_tiles(ref, rows):
    return jnp.concatenate([ref[pl.ds(j, rows, stride=SUBLANES), :] for j in range(SUBLANES)], axis=1)


def _mid_kernel(oa_ref, ob_ref, sga_ref, sgb_ref, x_ref, wa_ref, wb_ref, wo_ref, g_ref,
                wr_hi_ref, wr_lo_ref, br_ref, tri_ref,
                x1_ref, h2_ref, eidx_ref, rank_ref, gate_ref, cnt_ref):
    @pl.when(pl.program_id(0) == 0)
    def _():
        cnt_ref[...] = jnp.zeros(cnt_ref.shape, F32)

    a = jnp.dot(oa_ref[...], wa_ref[...], preferred_element_type=F32)
    b = jnp.dot(ob_ref[...], wb_ref[...], preferred_element_type=F32)
    merged = sga_ref[...].astype(F32) * a + sgb_ref[...].astype(F32) * b
    x1 = x_ref[...] + jnp.dot(merged.astype(BF16), wo_ref[...], preferred_element_type=F32)
    x1_ref[...] = x1
    ms = jnp.mean(x1 * x1, axis=-1, keepdims=True)
    h2 = x1 * lax.rsqrt(ms + EPS) * g_ref[...]
    _store_token_tiles(h2_ref, h2)

    hi = h2.astype(BF16)
    lo = (h2 - hi.astype(F32)).astype(BF16)
    wr_hi = wr_hi_ref[...]
    logits = (jnp.dot(hi, wr_hi, preferred_element_type=F32)
              + jnp.dot(lo, wr_hi, preferred_element_type=F32)
              + jnp.dot(hi, wr_lo_ref[...], preferred_element_type=F32)) + br_ref[...]
    lane = lax.broadcasted_iota(jnp.int32, logits.shape, 1)
    work = jnp.where(lane < N_EXPERTS, logits, -jnp.inf)
    vals, idxs, sels = [], [], []
    for _ in range(TOP_K):
        mx = jnp.max(work, axis=1, keepdims=True)
        ix = jnp.min(jnp.where(work == mx, lane, LANES), axis=1, keepdims=True)
        sel = lane == ix
        work = jnp.where(sel, -jnp.inf, work)
        vals.append(mx)
        idxs.append(ix)
        sels.append(sel)
    exps = [jnp.exp(v - vals[0]) for v in vals]
    denom = exps[0] + exps[1] + exps[2] + exps[3]
    chosen = sels[0] | sels[1] | sels[2] | sels[3]
    onehot = jnp.where(chosen, 1.0, 0.0)
    before = jnp.dot(tri_ref[...], onehot.astype(BF16), preferred_element_type=F32) + cnt_ref[0:1, :]
    eidx = jnp.zeros(logits.shape, jnp.int32)
    rank = jnp.zeros(logits.shape, F32)
    gate = jnp.zeros(logits.shape, F32)
    for k in range(TOP_K):
        here = lane == k
        eidx = jnp.where(here, idxs[k], eidx)
        rank = jnp.where(here, jnp.sum(jnp.where(sels[k], before, 0.0), axis=1, keepdims=True), rank)
        gate = jnp.where(here, exps[k] / denom, gate)
    eidx_ref[...] = eidx
    rank_ref[...] = rank.astype(jnp.int32)
    gate_ref[...] = gate
    cnt_ref[...] = cnt_ref[...] + jnp.sum(onehot, axis=0, keepdims=True)


def _mid(oa, ob, sga, sgb, x2d, w_a, w_b, w_o, g_ffn, w_router, b_router, tm):
    tokens, d_model = x2d.shape
    wr = jnp.zeros((d_model, LANES), F32).at[:, :N_EXPERTS].set(w_router.astype(F32))
    wr_hi = wr.astype(BF16)
    wr_lo = (wr - wr_hi.astype(F32)).astype(BF16)
    br = jnp.zeros((1, LANES), F32).at[0, :N_EXPERTS].set(b_router.astype(F32))
    tri = (jnp.arange(tm)[:, None] > jnp.arange(tm)[None, :]).astype(BF16)
    row = lambda i: (i, 0)
    fixed = lambda i: (0, 0)
    return pl.pallas_call(
        _mid_kernel,
        grid=(tokens // tm,),
        in_specs=[
            pl.BlockSpec((tm, Q_COLS), row), pl.BlockSpec((tm, Q_COLS), row),
            pl.BlockSpec((tm, d_model), row), pl.BlockSpec((tm, d_model), row),
            pl.BlockSpec((tm, d_model), row),
            pl.BlockSpec((Q_COLS, d_model), fixed), pl.BlockSpec((Q_COLS, d_model), fixed),
            pl.BlockSpec((d_model, d_model), fixed),
            pl.BlockSpec((1, d_model), fixed),
            pl.BlockSpec((d_model, LANES), fixed), pl.BlockSpec((d_model, LANES), fixed),
            pl.BlockSpec((1, LANES), fixed),
            pl.BlockSpec((tm, tm), fixed),
        ],
        out_specs=[
            pl.BlockSpec((tm, d_model), row),
            pl.BlockSpec((tm * SUBLANES, LANES), row),
            pl.BlockSpec((tm, LANES), row), pl.BlockSpec((tm, LANES), row), pl.BlockSpec((tm, LANES), row),
            pl.BlockSpec((SUBLANES, LANES), fixed),
        ],
        out_shape=[
            jax.ShapeDtypeStruct((tokens, d_model), F32),
            jax.ShapeDtypeStruct((tokens * SUBLANES, LANES), F32),
            jax.ShapeDtypeStruct((tokens, LANES), jnp.int32),
            jax.ShapeDtypeStruct((tokens, LANES), jnp.int32),
            jax.ShapeDtypeStruct((tokens, LANES), F32),
            jax.ShapeDtypeStruct((SUBLANES, LANES), F32),
        ],
        compiler_params=_cparams(("arbitrary",)),
        name="mid",
    )(oa, ob, sga, sgb, x2d, w_a.astype(BF16), w_b.astype(BF16), w_o.astype(BF16),
      g_ffn.astype(F32)[None, :], wr_hi, wr_lo, br, tri)


def _row_gather(idx_smem, slot, n_rows, src_ref, dst_ref, sem):
    def start(r, carry):
        tok = idx_smem[slot, r]
        pltpu.make_async_copy(
            src_ref.at[pl.ds(pl.multiple_of(tok * SUBLANES, SUBLANES), SUBLANES)],
            dst_ref.at[pl.ds(pl.multiple_of(r * SUBLANES, SUBLANES), SUBLANES)],
            sem).start()
        return carry
    lax.fori_loop(0, n_rows, start, 0, unroll=8)


def _row_gather_wait(n_rows, src_ref, dst_ref, sem):
    def wait(r, carry):
        pltpu.make_async_copy(src_ref.at[pl.ds(0, SUBLANES)], dst_ref.at[pl.ds(0, SUBLANES)], sem).wait()
        return carry
    lax.fori_loop(0, n_rows, wait, 0, unroll=8)


def _moe_kernel(bexp_ref, nused_ref, rowtok_ref, h2_ref, wgu_ref, bgu_ref, wd_ref, bd_ref,
                y_ref, idx_smem, xbuf, wgu_bf, wd_bf, isem, gsem, *, rows):
    i = pl.program_id(0)
    n_used = nused_ref[0]
    slot = lax.rem(i, 2)
    nxt = 1 - slot

    def idx_copy(blk, s):
        return pltpu.make_async_copy(rowtok_ref.at[pl.ds(blk, 1)], idx_smem.at[pl.ds(s, 1)], isem.at[s])

    @pl.when(i == 0)
    def _():
        idx_copy(0, 0).start()
        idx_copy(0, 0).wait()
        _row_gather(idx_smem, 0, rows, h2_ref, xbuf.at[0], gsem.at[0])

    @pl.when(i + 1 < n_used)
    def _():
        idx_copy(i + 1, nxt).start()

    @pl.when(i < n_used)
    def _():
        _row_gather_wait(rows, h2_ref, xbuf.at[slot], gsem.at[slot])

    @pl.when(i + 1 < n_used)
    def _():
        idx_copy(i + 1, nxt).wait()
        _row_gather(idx_smem, nxt, rows, h2_ref, xbuf.at[nxt], gsem.at[nxt])

    new_expert = jnp.logical_or(i == 0, bexp_ref[i] != bexp_ref[jnp.maximum(i - 1, 0)])

    @pl.when(jnp.logical_and(new_expert, i < n_used))
    def _():
        wgu_bf[...] = wgu_ref[...].astype(BF16)
        wd_bf[...] = wd_ref[...].astype(BF16)

    @pl.when(i < n_used)
    def _():
        x = _load_token_tiles(xbuf.at[slot], rows).astype(BF16)
        gu = jnp.dot(x, wgu_bf[...], preferred_element_type=F32) + bgu_ref[...]
        gate = jnp.minimum(gu[:, :D_FF], SWIGLU_LIMIT)
        up = jnp.clip(gu[:, D_FF:], -SWIGLU_LIMIT, SWIGLU_LIMIT)
        act = (up + 1.0) * (gate * jax.nn.sigmoid(SWIGLU_ALPHA * gate))
        y = jnp.dot(act.astype(BF16), wd_bf[...], preferred_element_type=F32) + bd_ref[...]
        _store_token_tiles(y_ref, y)

    @pl.when(i >= n_used)
    def _():
        y_ref[...] = jnp.zeros(y_ref.shape, F32)


def _moe(block_exp, n_used, row_tok, h2_tiles, w_gate_up, b_gate_up, w_down, b_down, rows):
    n_blocks = row_tok.shape[0]
    n_exp, d_model, ff2 = w_gate_up.shape
    grid_spec = pltpu.PrefetchScalarGridSpec(
        num_scalar_prefetch=2,
        grid=(n_blocks,),
        in_specs=[
            pl.BlockSpec(memory_space=pl.ANY),
            pl.BlockSpec(memory_space=pl.ANY),
            pl.BlockSpec((None, d_model, ff2), lambda i, be, nu: (be[i], 0, 0)),
            pl.BlockSpec((None, 1, ff2), lambda i, be, nu: (be[i], 0, 0)),
            pl.BlockSpec((None, ff2 // 2, d_model), lambda i, be, nu: (be[i], 0, 0)),
            pl.BlockSpec((None, 1, d_model), lambda i, be, nu: (be[i], 0, 0)),
        ],
        out_specs=pl.BlockSpec((rows * SUBLANES, LANES), lambda i, be, nu: (i, 0)),
        scratch_shapes=[
            pltpu.SMEM((2, rows), jnp.int32),
            pltpu.VMEM((2, rows * SUBLANES, LANES), F32),
            pltpu.VMEM((d_model, ff2), BF16),
            pltpu.VMEM((ff2 // 2, d_model), BF16),
            pltpu.SemaphoreType.DMA((2,)),
            pltpu.SemaphoreType.DMA((2,)),
        ],
    )
    return pl.pallas_call(
        functools.partial(_moe_kernel, rows=rows),
        grid_spec=grid_spec,
        out_shape=jax.ShapeDtypeStruct((n_blocks * rows * SUBLANES, LANES), F32),
        compiler_params=_cparams(("arbitrary",)),
        name="moe",
    )(block_exp, n_used, row_tok, h2_tiles, w_gate_up, b_gate_up[:, None, :], w_down, b_down[:, None, :])


def _combine_kernel(dest_ref, y_ref, gate_ref, x1_ref, g_ref, o_ref, idx_smem, ybuf, isem, gsem, *, tm):
    i = pl.program_id(0)
    n_tiles = pl.num_programs(0)
    slot = lax.rem(i, 2)
    nxt = 1 - slot
    n_rows = TOP_K * tm

    def idx_copy(blk, s):
        return pltpu.make_async_copy(dest_ref.at[pl.ds(blk, 1)], idx_smem.at[pl.ds(s, 1)], isem.at[s])

    @pl.when(i == 0)
    def _():
        idx_copy(0, 0).start()
        idx_copy(0, 0).wait()
        _row_gather(idx_smem, 0, n_rows, y_ref, ybuf.at[0], gsem.at[0])

    @pl.when(i + 1 < n_tiles)
    def _():
        idx_copy(i + 1, nxt).start()

    _row_gather_wait(n_rows, y_ref, ybuf.at[slot], gsem.at[slot])

    @pl.when(i + 1 < n_tiles)
    def _():
        idx_copy(i + 1, nxt).wait()
        _row_gather(idx_smem, nxt, n_rows, y_ref, ybuf.at[nxt], gsem.at[nxt])

    gate = gate_ref[...]
    out = x1_ref[...]
    for k in range(TOP_K):
        y_k = jnp.concatenate(
            [ybuf[slot, pl.ds(k * tm * SUBLANES + j, tm, stride=SUBLANES), :] for j in range(SUBLANES)], axis=1)
        out = out + gate[:, k:k + 1] * y_k
    ms = jnp.mean(out * out, axis=-1, keepdims=True)
    o_ref[...] = out * lax.rsqrt(ms + EPS) * g_ref[...]


def _combine(dest_tiles, y_tiles, gate, x1, g_final, tm):
    tokens, d_model = x1.shape
    row = lambda i: (i, 0)
    return pl.pallas_call(
        functools.partial(_combine_kernel, tm=tm),
        grid=(tokens // tm,),
        in_specs=[
            pl.BlockSpec(memory_space=pl.ANY),
            pl.BlockSpec(memory_space=pl.ANY),
            pl.BlockSpec((tm, LANES), row),
            pl.BlockSpec((tm, d_model), row),
            pl.BlockSpec((1, d_model), lambda i: (0, 0)),
        ],
        out_specs=pl.BlockSpec((tm, d_model), row),
        out_shape=jax.ShapeDtypeStruct((tokens, d_model), F32),
        scratch_shapes=[
            pltpu.SMEM((2, TOP_K * tm), jnp.int32),
            pltpu.VMEM((2, TOP_K * tm * SUBLANES, LANES), F32),
            pltpu.SemaphoreType.DMA((2,)),
            pltpu.SemaphoreType.DMA((2,)),
        ],
        compiler_params=_cparams(("arbitrary",)),
        name="combine",
    )(dest_tiles, y_tiles, gate, x1, g_final.astype(F32)[None, :])


PROJ_ROWS = 256
MID_ROWS = 512
ATTN_Q_ROWS = 128
ATTN_K_ROWS = 512
MOE_ROWS = 256
COMBINE_ROWS = 128


def _layer(x2d, batch, seq, g_mix, w_in, qa_norm, ka_norm, sink_b, w_branch_a, w_branch_b, w_out,
           g_ffn, w_router, b_router, w_gate_up, b_gate_up, w_down, b_down, g_final):
    tokens = x2d.shape[0]
    qa, ka, va, qb, kb, vb, sga, sgb = _project(x2d, g_mix, w_in, qa_norm, ka_norm, seq, min(PROJ_ROWS, seq))
    oa = _attention_global(qa, ka, va, batch, seq, ATTN_Q_ROWS, min(ATTN_K_ROWS, seq))
    ob = _attention_window(qb, kb, vb, sink_b, batch, seq)
    x1, h2_tiles, eidx, rank, gate, cnt = _mid(
        oa, ob, sga, sgb, x2d, w_branch_a, w_branch_b, w_out, g_ffn, w_router, b_router, min(MID_ROWS, tokens))

    rows = MOE_ROWS
    n_assign = tokens * TOP_K
    n_blocks = n_assign // rows + N_EXPERTS
    counts = cnt[0, :N_EXPERTS].astype(jnp.int32)
    padded = (counts + rows - 1) // rows * rows
    ends = jnp.cumsum(padded)
    starts = ends - padded
    dest = starts[eidx[:, :TOP_K]] + rank[:, :TOP_K]
    row_tok = jnp.zeros((n_blocks * rows,), jnp.int32).at[dest.reshape(-1)].set(
        jnp.arange(n_assign, dtype=jnp.int32) // TOP_K)
    block_start = jnp.arange(n_blocks, dtype=jnp.int32) * rows
    block_exp = jnp.minimum(
        jnp.sum((ends[None, :] <= block_start[:, None]).astype(jnp.int32), axis=1), N_EXPERTS - 1)
    n_used = (ends[-1:] // rows).astype(jnp.int32)

    y_tiles = _moe(block_exp, n_used, row_tok.reshape(n_blocks, rows), h2_tiles,
                   w_gate_up, b_gate_up, w_down, b_down, rows)
    tm = min(COMBINE_ROWS, tokens)
    dest_tiles = dest.reshape(tokens // tm, tm, TOP_K).transpose(0, 2, 1).reshape(tokens // tm, TOP_K * tm)
    return _combine(dest_tiles, y_tiles, gate, x1, g_final, tm)


def kernel(x, g_mix, w_in, qa_norm, ka_norm, sink_b, w_branch_a, w_branch_b, w_out, g_ffn, w_router, b_router,
           w_gate_up, b_gate_up, w_down, b_down, g_final):
    batch, seq, d_model = x.shape
    assert g_mix.shape[0] == 1, "single-layer block"
    out = _layer(x.reshape(batch * seq, d_model), batch, seq, g_mix[0], w_in[0], qa_norm[0], ka_norm[0],
                 sink_b[0], w_branch_a[0], w_branch_b[0], w_out[0], g_ffn[0], w_router[0], b_router[0],
                 w_gate_up[0], b_gate_up[0], w_down[0], b_down[0], g_final)
    return out.reshape(batch, seq, d_model)
```

```python
import functools

import jax
import jax.numpy as jnp
from jax import lax
from jax.experimental import pallas as pl
from jax.experimental.pallas import tpu as pltpu

HEAD_DIM = 64
Q_HEADS = 8
KV_HEADS = 2
GROUPS = Q_HEADS // KV_HEADS
Q_COLS = Q_HEADS * HEAD_DIM
KV_COLS = KV_HEADS * HEAD_DIM
GROUP_COLS = GROUPS * HEAD_DIM
WINDOW = 128
GRID_W = 64
ROPE_THETA = 10000.0
N_EXPERTS = 32
TOP_K = 4
D_FF = 1024
SWIGLU_LIMIT = 7.0
SWIGLU_ALPHA = 1.702
EPS = 1e-6
NEG_INF = -1e30
LOG2_E = 1.4426950408889634

LANES = 128
SUBLANES = 8
V_ROWS = 80
VMEM_LIMIT = 56 * 1024 * 1024

F32 = jnp.float32
BF16 = jnp.bfloat16


def _cparams(sem, vmem=VMEM_LIMIT):
    return pltpu.CompilerParams(dimension_semantics=sem, vmem_limit_bytes=vmem)


def _rope(y, cos, sin_signed):
    lane = lax.broadcasted_iota(jnp.int32, y.shape, 1)
    first_half = (lane & 32) == 0
    rot = jnp.where(first_half, pltpu.roll(y, 96, 1), pltpu.roll(y, 32, 1))
    return y * cos + rot * sin_signed


def _head_rmsnorm(q, ones_blockdiag, gain):
    q2 = q * q
    hi = q2.astype(BF16)
    lo = (q2 - hi.astype(F32)).astype(BF16)
    ss = (jnp.dot(hi, ones_blockdiag, preferred_element_type=F32)
          + jnp.dot(lo, ones_blockdiag, preferred_element_type=F32))
    return q * lax.rsqrt(ss * (1.0 / HEAD_DIM) + EPS) * gain


def _proj_kernel(x_ref, g_ref, w_ref, cosa_ref, sina_ref, cos1_ref, sin1_ref, qg_ref, kg_ref, bd_ref,
                 qa_ref, ka_ref, va_ref, qb_ref, kb_ref, vb_ref, sga_ref, sgb_ref, *, d_model):
    x = x_ref[...]
    ms = jnp.mean(x * x, axis=-1, keepdims=True)
    h = (x * lax.rsqrt(ms + EPS) * g_ref[...]).astype(BF16)
    scale = HEAD_DIM ** -0.5 * LOG2_E

    def proj(c0, width):
        return jnp.dot(h, w_ref[:, c0:c0 + width], preferred_element_type=F32)

    cosa, sina = cosa_ref[...], sina_ref[...]
    cos1, sin1 = cos1_ref[...], sin1_ref[...]
    bd = bd_ref[...]
    c = 0
    for s in range(Q_COLS // LANES):
        q = _head_rmsnorm(proj(c, LANES), bd, qg_ref[...])
        qa_ref[:, s * LANES:(s + 1) * LANES] = (_rope(q, cosa, sina) * scale).astype(BF16)
        c += LANES
    k = _head_rmsnorm(proj(c, LANES), bd, kg_ref[...])
    ka_ref[...] = _rope(k, cosa, sina).astype(BF16)
    c += LANES
    va_ref[...] = proj(c, LANES).astype(BF16)
    c += LANES
    for s in range(Q_COLS // LANES):
        qb_ref[:, s * LANES:(s + 1) * LANES] = (_rope(proj(c, LANES), cos1, sin1) * scale).astype(BF16)
        c += LANES
    kb_ref[...] = _rope(proj(c, LANES), cos1, sin1).astype(BF16)
    c += LANES
    vb_ref[...] = proj(c, LANES).astype(BF16)
    c += LANES
    sga_ref[...] = jax.nn.sigmoid(proj(c, d_model)).astype(BF16)
    c += d_model
    sgb_ref[...] = jax.nn.sigmoid(proj(c, d_model)).astype(BF16)


def _rope_tables(seq):
    def slab(ang):
        cos, sin = jnp.cos(ang), jnp.sin(ang)
        cos64 = jnp.concatenate([cos, cos], axis=-1)
        sin64 = jnp.concatenate([-sin, sin], axis=-1)
        return jnp.tile(cos64, (1, 2)), jnp.tile(sin64, (1, 2))

    pos = jnp.arange(seq, dtype=F32)
    inv1 = ROPE_THETA ** (-jnp.arange(0, HEAD_DIM, 2, dtype=F32) / HEAD_DIM)
    ang1 = pos[:, None] * inv1[None, :]
    rows = seq // GRID_W
    row = jnp.broadcast_to(jnp.arange(rows, dtype=F32)[:, None], (rows, GRID_W)).reshape(-1)
    col = jnp.broadcast_to(jnp.arange(GRID_W, dtype=F32)[None, :], (rows, GRID_W)).reshape(-1)
    axis_dim = HEAD_DIM // 2
    inva = ROPE_THETA ** (-jnp.arange(0, axis_dim, 2, dtype=F32) / axis_dim)
    anga = jnp.concatenate([row[:, None] * inva[None, :], col[:, None] * inva[None, :]], axis=-1)
    return slab(anga), slab(ang1)


def _project(x2d, g_mix, w_in, qa_norm, ka_norm, seq, tm):
    tokens, d_model = x2d.shape
    in_cols = w_in.shape[1]
    (cosa, sina), (cos1, sin1) = _rope_tables(seq)
    head_id = jnp.arange(LANES) // HEAD_DIM
    ones_blockdiag = (head_id[:, None] == head_id[None, :]).astype(BF16)
    qg = jnp.tile(qa_norm.astype(F32), 2)[None, :]
    kg = jnp.tile(ka_norm.astype(F32), 2)[None, :]
    seq_tiles = seq // tm
    row = lambda i: (i, 0)
    pos = lambda i: (i % seq_tiles, 0)
    fixed = lambda i: (0, 0)
    bf = lambda cols: jax.ShapeDtypeStruct((tokens, cols), BF16)
    out_cols = [Q_COLS, KV_COLS, KV_COLS, Q_COLS, KV_COLS, KV_COLS, d_model, d_model]
    return pl.pallas_call(
        functools.partial(_proj_kernel, d_model=d_model),
        grid=(tokens // tm,),
        in_specs=[
            pl.BlockSpec((tm, d_model), row),
            pl.BlockSpec((1, d_model), fixed),
            pl.BlockSpec((d_model, in_cols), fixed),
            pl.BlockSpec((tm, LANES), pos), pl.BlockSpec((tm, LANES), pos),
            pl.BlockSpec((tm, LANES), pos), pl.BlockSpec((tm, LANES), pos),
            pl.BlockSpec((1, LANES), fixed), pl.BlockSpec((1, LANES), fixed),
            pl.BlockSpec((LANES, LANES), fixed),
        ],
        out_specs=[pl.BlockSpec((tm, c), row) for c in out_cols],
        out_shape=[bf(c) for c in out_cols],
        compiler_params=_cparams(("arbitrary",)),
        name="proj",
    )(x2d, g_mix.astype(F32)[None, :], w_in.astype(BF16), cosa, sina, cos1, sin1, qg, kg, ones_blockdiag)


def _stack_group_queries(q):
    return jnp.concatenate([q[:, g * HEAD_DIM:(g + 1) * HEAD_DIM] for g in range(GROUPS)], axis=0)


def _unstack_group_outputs(o_t, tq):
    pad = jnp.zeros((LANES - HEAD_DIM, o_t.shape[1]), o_t.dtype)
    o_pad = jnp.concatenate([o_t, pad], axis=0)
    heads = [o_pad[:, g * tq:(g + 1) * tq].T[:, :HEAD_DIM] for g in range(GROUPS)]
    return jnp.concatenate(heads, axis=1)


def _attn_global_kernel(q_ref, k_ref, vt_ref, o_ref, s0, s1, p0, p1, acc_ref, *, tq, tk, n_chunks):
    q_st = _stack_group_queries(q_ref[...])
    m_rows = q_st.shape[0]

    def scores(c, s_ref):
        k_c = k_ref[pl.ds(pl.multiple_of(c * tk, tk), tk), :]
        s_ref[...] = lax.dot_general(k_c, q_st, (((1,), (1,)), ((), ())), preferred_element_type=F32)

    def softmax(s_ref, p_ref, m_prev):
        s = s_ref[...]
        m_new = jnp.maximum(m_prev, jnp.max(s, axis=0, keepdims=True))
        p_ref[...] = jnp.exp2(s - m_new).astype(BF16)
        return m_new, jnp.exp2(m_prev - m_new)

    def values(c, p_ref, alpha):
        acc_ref[...] = acc_ref[...] * alpha + jnp.dot(vt_ref[c], p_ref[...], preferred_element_type=F32)

    scores(0, s0)
    p1[...] = jnp.zeros(p1.shape, BF16)
    acc_ref[...] = jnp.zeros(acc_ref.shape, F32)

    def pair(j, carry):
        m, alpha = carry
        c = 2 * j
        scores(c + 1, s1)
        m_a, alpha_a = softmax(s0, p0, m)
        values(jnp.maximum(c - 1, 0), p1, alpha)
        scores(jnp.minimum(c + 2, n_chunks - 1), s0)
        m_b, alpha_b = softmax(s1, p1, m_a)
        values(c, p0, alpha_a)
        return m_b, alpha_b

    init = (jnp.full((1, m_rows), NEG_INF, F32), jnp.ones((1, m_rows), F32))
    _, alpha = lax.fori_loop(0, n_chunks // 2, pair, init)
    values(n_chunks - 1, p1, alpha)
    acc = acc_ref[...]
    o_t = acc[:HEAD_DIM, :] / acc[HEAD_DIM:HEAD_DIM + 1, :]
    o_ref[...] = _unstack_group_outputs(o_t, tq).astype(o_ref.dtype)


def _attn_window_kernel(q_ref, k_ref, vt_ref, sink_ref, o_ref, *, tq, n_blocks, blocks_per_step):
    span = 3 * tq
    sink = sink_ref[...]
    for i in range(blocks_per_step):
        n = pl.program_id(2) * blocks_per_step + i
        c0 = jnp.clip(n - 1, 0, n_blocks - 3)
        start = pl.multiple_of(c0 * tq, tq)
        q_st = _stack_group_queries(q_ref[i * tq:(i + 1) * tq, :])
        k_w = k_ref[pl.ds(start, span), :]
        s = lax.dot_general(k_w, q_st, (((1,), (1,)), ((), ())), preferred_element_type=F32)
        kpos = start + lax.broadcasted_iota(jnp.int32, s.shape, 0)
        qpos = n * tq + (lax.broadcasted_iota(jnp.int32, s.shape, 1) & (tq - 1))
        s = jnp.where(jnp.abs(kpos - qpos) <= WINDOW, s, NEG_INF)
        m = jnp.maximum(jnp.max(s, axis=0, keepdims=True), sink)
        p = jnp.exp2(s - m).astype(BF16)
        acc = jnp.dot(vt_ref[c0], p[:tq], preferred_element_type=F32)
        for j in range(1, 3):
            acc += jnp.dot(vt_ref[c0 + j], p[j * tq:(j + 1) * tq], preferred_element_type=F32)
        denom = acc[HEAD_DIM:HEAD_DIM + 1, :] + jnp.exp2(sink - m)
        o_ref[i * tq:(i + 1) * tq, :] = _unstack_group_outputs(acc[:HEAD_DIM, :] / denom, tq).astype(o_ref.dtype)


def _kv_layouts(k2d, v2d, batch, seq, chunk):
    k_hm = k2d.reshape(batch, seq, KV_HEADS, HEAD_DIM).transpose(0, 2, 1, 3)
    v_t = v2d.reshape(batch, seq // chunk, chunk, KV_HEADS, HEAD_DIM).transpose(0, 3, 1, 4, 2)
    ones = jnp.ones(v_t.shape[:3] + (V_ROWS - HEAD_DIM, chunk), v_t.dtype)
    return k_hm, jnp.concatenate([v_t, ones], axis=3)


def _attention_global(q2d, k2d, v2d, batch, seq, tq, tk):
    tokens = q2d.shape[0]
    n_chunks = seq // tk
    assert n_chunks % 2 == 0, "the chunk pipeline advances two chunks per step"
    k_hm, v_t = _kv_layouts(k2d, v2d, batch, seq, tk)
    q_tiles = seq // tq
    m_rows = GROUPS * tq
    qmap = lambda b, h, i: (b * q_tiles + i, h)
    return pl.pallas_call(
        functools.partial(_attn_global_kernel, tq=tq, tk=tk, n_chunks=n_chunks),
        grid=(batch, KV_HEADS, q_tiles),
        in_specs=[
            pl.BlockSpec((tq, GROUP_COLS), qmap),
            pl.BlockSpec((None, None, seq, HEAD_DIM), lambda b, h, i: (b, h, 0, 0)),
            pl.BlockSpec((None, None, n_chunks, V_ROWS, tk), lambda b, h, i: (b, h, 0, 0, 0)),
        ],
        out_specs=pl.BlockSpec((tq, GROUP_COLS), qmap),
        out_shape=jax.ShapeDtypeStruct((tokens, Q_COLS), BF16),
        scratch_shapes=[pltpu.VMEM((tk, m_rows), F32), pltpu.VMEM((tk, m_rows), F32),
                        pltpu.VMEM((tk, m_rows), BF16), pltpu.VMEM((tk, m_rows), BF16),
                        pltpu.VMEM((V_ROWS, m_rows), F32)],
        compiler_params=_cparams(("arbitrary", "arbitrary", "arbitrary")),
        name="attn_global",
    )(q2d, k_hm, v_t)


def _attention_window(q2d, k2d, v2d, sink, batch, seq):
    tokens = q2d.shape[0]
    tq = WINDOW
    n_blocks = seq // tq
    assert n_blocks >= 3
    k_hm, v_t = _kv_layouts(k2d, v2d, batch, seq, tq)
    m_rows = GROUPS * tq
    sink_rows = jnp.repeat(sink.astype(F32).reshape(KV_HEADS, GROUPS) * LOG2_E, tq, axis=1)[:, None, :]
    per_step = WINDOW_BLOCKS_PER_STEP if n_blocks % WINDOW_BLOCKS_PER_STEP == 0 else 1
    steps = n_blocks // per_step
    qmap = lambda b, h, i: (b * steps + i, h)
    return pl.pallas_call(
        functools.partial(_attn_window_kernel, tq=tq, n_blocks=n_blocks, blocks_per_step=per_step),
        grid=(batch, KV_HEADS, steps),
        in_specs=[
            pl.BlockSpec((per_step * tq, GROUP_COLS), qmap),
            pl.BlockSpec((None, None, seq, HEAD_DIM), lambda b, h, i: (b, h, 0, 0)),
            pl.BlockSpec((None, None, n_blocks, V_ROWS, tq), lambda b, h, i: (b, h, 0, 0, 0)),
            pl.BlockSpec((None, 1, m_rows), lambda b, h, i: (h, 0, 0)),
        ],
        out_specs=pl.BlockSpec((per_step * tq, GROUP_COLS), qmap),
        out_shape=jax.ShapeDtypeStruct((tokens, Q_COLS), BF16),
        compiler_params=_cparams(("arbitrary", "arbitrary", "arbitrary")),
        name="attn_window",
    )(q2d, k_hm, v_t, sink_rows)


def _store_token_tiles(ref, val):
    rows = val.shape[0]
    for j in range(val.shape[1] // LANES):
        ref[pl.ds(j, rows, stride=SUBLANES), :] = val[:, j * LANES:(j + 1) * LANES]


def _load_token_tiles(ref, rows):
    return jnp.concatenate([ref[pl.ds(j, rows, stride=SUBLANES), :] for j in range(SUBLANES)], axis=1)


def _mid_kernel(oa_ref, ob_ref, sga_ref, sgb_ref, x_ref, wa_ref, wb_ref, wo_ref, g_ref,
                wr_hi_ref, wr_lo_ref, br_ref, tri_ref,
                x1_ref, h2_ref, eidx_ref, rank_ref, gate_ref, cnt_ref):
    @pl.when(pl.program_id(0) == 0)
    def _():
        cnt_ref[...] = jnp.zeros(cnt_ref.shape, F32)

    a = jnp.dot(oa_ref[...], wa_ref[...], preferred_element_type=F32)
    b = jnp.dot(ob_ref[...], wb_ref[...], preferred_element_type=F32)
    merged = sga_ref[...].astype(F32) * a + sgb_ref[...].astype(F32) * b
    x1 = x_ref[...] + jnp.dot(merged.astype(BF16), wo_ref[...], preferred_element_type=F32)
    x1_ref[...] = x1
    ms = jnp.mean(x1 * x1, axis=-1, keepdims=True)
    h2 = x1 * lax.rsqrt(ms + EPS) * g_ref[...]
    _store_token_tiles(h2_ref, h2)

    hi = h2.astype(BF16)
    lo = (h2 - hi.astype(F32)).astype(BF16)
    wr_hi = wr_hi_ref[...]
    logits = (jnp.dot(hi, wr_hi, preferred_element_type=F32)
              + jnp.dot(lo, wr_hi, preferred_element_type=F32)
              + jnp.dot(hi, wr_lo_ref[...], preferred_element_type=F32)) + br_ref[...]
    lane = lax.broadcasted_iota(jnp.int32, logits.shape, 1)
    work = jnp.where(lane < N_EXPERTS, logits, -jnp.inf)
    vals, idxs, sels = [], [], []
    for _ in range(TOP_K):
        mx = jnp.max(work, axis=1, keepdims=True)
        ix = jnp.min(jnp.where(work == mx, lane, LANES), axis=1, keepdims=True)
        sel = lane == ix
        work = jnp.where(sel, -jnp.inf, work)
        vals.append(mx)
        idxs.append(ix)
        sels.append(sel)
    exps = [jnp.exp(v - vals[0]) for v in vals]
    denom = exps[0] + exps[1] + exps[2] + exps[3]
    chosen = sels[0] | sels[1] | sels[2] | sels[3]
    onehot = jnp.where(chosen, 1.0, 0.0)
    before = jnp.dot(tri_ref[...], onehot.astype(BF16), preferred_element_type=F32) + cnt_ref[0:1, :]
    eidx = jnp.zeros(logits.shape, jnp.int32)
    rank = jnp.zeros(logits.shape, F32)
    gate = jnp.zeros(logits.shape, F32)
    for k in range(TOP_K):
        here = lane == k
        eidx = jnp.where(here, idxs[k], eidx)
        rank = jnp.where(here, jnp.sum(jnp.where(sels[k], before, 0.0), axis=1, keepdims=True), rank)
        gate = jnp.where(here, exps[k] / denom, gate)
    eidx_ref[...] = eidx
    rank_ref[...] = rank.astype(jnp.int32)
    gate_ref[...] = gate
    cnt_ref[...] = cnt_ref[...] + jnp.sum(onehot, axis=0, keepdims=True)


def _mid(oa, ob, sga, sgb, x2d, w_a, w_b, w_o, g_ffn, w_router, b_router, tm):
    tokens, d_model = x2d.shape
    wr = jnp.zeros((d_model, LANES), F32).at[:, :N_EXPERTS].set(w_router.astype(F32))
    wr_hi = wr.astype(BF16)
    wr_lo = (wr - wr_hi.astype(F32)).astype(BF16)
    br = jnp.zeros((1, LANES), F32).at[0, :N_EXPERTS].set(b_router.astype(F32))
    tri = (jnp.arange(tm)[:, None] > jnp.arange(tm)[None, :]).astype(BF16)
    row = lambda i: (i, 0)
    fixed = lambda i: (0, 0)
    return pl.pallas_call(
        _mid_kernel,
        grid=(tokens // tm,),
        in_specs=[
            pl.BlockSpec((tm, Q_COLS), row), pl.BlockSpec((tm, Q_COLS), row),
            pl.BlockSpec((tm, d_model), row), pl.BlockSpec((tm, d_model), row),
            pl.BlockSpec((tm, d_model), row),
            pl.BlockSpec((Q_COLS, d_model), fixed), pl.BlockSpec((Q_COLS, d_model), fixed),
            pl.BlockSpec((d_model, d_model), fixed),
            pl.BlockSpec((1, d_model), fixed),
            pl.BlockSpec((d_model, LANES), fixed), pl.BlockSpec((d_model, LANES), fixed),
            pl.BlockSpec((1, LANES), fixed),
            pl.BlockSpec((tm, tm), fixed),
        ],
        out_specs=[
            pl.BlockSpec((tm, d_model), row),
            pl.BlockSpec((tm * SUBLANES, LANES), row),
            pl.BlockSpec((tm, LANES), row), pl.BlockSpec((tm, LANES), row), pl.BlockSpec((tm, LANES), row),
            pl.BlockSpec((SUBLANES, LANES), fixed),
        ],
        out_shape=[
            jax.ShapeDtypeStruct((tokens, d_model), F32),
            jax.ShapeDtypeStruct((tokens * SUBLANES, LANES), F32),
            jax.ShapeDtypeStruct((tokens, LANES), jnp.int32),
            jax.ShapeDtypeStruct((tokens, LANES), jnp.int32),
            jax.ShapeDtypeStruct((tokens, LANES), F32),
            jax.ShapeDtypeStruct((SUBLANES, LANES), F32),
        ],
        compiler_params=_cparams(("arbitrary",)),
        name="mid",
    )(oa, ob, sga, sgb, x2d, w_a.astype(BF16), w_b.astype(BF16), w_o.astype(BF16),
      g_ffn.astype(F32)[None, :], wr_hi, wr_lo, br, tri)


def _row_gather(idx_smem, slot, n_rows, src_ref, dst_ref, sem):
    def start(r, carry):
        tok = idx_smem[slot, r]
        pltpu.make_async_copy(
            src_ref.at[pl.ds(pl.multiple_of(tok * SUBLANES, SUBLANES), SUBLANES)],
            dst_ref.at[pl.ds(pl.multiple_of(r * SUBLANES, SUBLANES), SUBLANES)],
            sem).start()
        return carry
    lax.fori_loop(0, n_rows, start, 0, unroll=8)


def _row_gather_wait(n_rows, src_ref, dst_ref, sem):
    def wait(r, carry):
        pltpu.make_async_copy(src_ref.at[pl.ds(0, SUBLANES)], dst_ref.at[pl.ds(0, SUBLANES)], sem).wait()
        return carry
    lax.fori_loop(0, n_rows, wait, 0, unroll=8)


def _moe_kernel(bexp_ref, nused_ref, rowtok_ref, h2_ref, wgu_ref, bgu_ref, wd_ref, bd_ref,
                y_ref, idx_smem, xbuf, wgu_bf, wd_bf, isem, gsem, *, rows):
    i = pl.program_id(0)
    n_used = nused_ref[0]
    slot = lax.rem(i, 2)
    nxt = 1 - slot

    def idx_copy(blk, s):
        return pltpu.make_async_copy(rowtok_ref.at[pl.ds(blk, 1)], idx_smem.at[pl.ds(s, 1)], isem.at[s])

    @pl.when(i == 0)
    def _():
        idx_copy(0, 0).start()
        idx_copy(0, 0).wait()
        _row_gather(idx_smem, 0, rows, h2_ref, xbuf.at[0], gsem.at[0])

    @pl.when(i + 1 < n_used)
    def _():
        idx_copy(i + 1, nxt).start()

    @pl.when(i < n_used)
    def _():
        _row_gather_wait(rows, h2_ref, xbuf.at[slot], gsem.at[slot])

    @pl.when(i + 1 < n_used)
    def _():
        idx_copy(i + 1, nxt).wait()
        _row_gather(idx_smem, nxt, rows, h2_ref, xbuf.at[nxt], gsem.at[nxt])

    new_expert = jnp.logical_or(i == 0, bexp_ref[i] != bexp_ref[jnp.maximum(i - 1, 0)])

    @pl.when(jnp.logical_and(new_expert, i < n_used))
    def _():
        wgu_bf[...] = wgu_ref[...].astype(BF16)
        wd_bf[...] = wd_ref[...].astype(BF16)

    @pl.when(i < n_used)
    def _():
        x = _load_token_tiles(xbuf.at[slot], rows).astype(BF16)
        gu = jnp.dot(x, wgu_bf[...], preferred_element_type=F32) + bgu_ref[...]
        gate = jnp.minimum(gu[:, :D_FF], SWIGLU_LIMIT)
        up = jnp.clip(gu[:, D_FF:], -SWIGLU_LIMIT, SWIGLU_LIMIT)
        act = (up + 1.0) * (gate * jax.nn.sigmoid(SWIGLU_ALPHA * gate))
        y = jnp.dot(act.astype(BF16), wd_bf[...], preferred_element_type=F32) + bd_ref[...]
        _store_token_tiles(y_ref, y)

    @pl.when(i >= n_used)
    def _():
        y_ref[...] = jnp.zeros(y_ref.shape, F32)


def _moe(block_exp, n_used, row_tok, h2_tiles, w_gate_up, b_gate_up, w_down, b_down, rows):
    n_blocks = row_tok.shape[0]
    n_exp, d_model, ff2 = w_gate_up.shape
    grid_spec = pltpu.PrefetchScalarGridSpec(
        num_scalar_prefetch=2,
        grid=(n_blocks,),
        in_specs=[
            pl.BlockSpec(memory_space=pl.ANY),
            pl.BlockSpec(memory_space=pl.ANY),
            pl.BlockSpec((None, d_model, ff2), lambda i, be, nu: (be[i], 0, 0)),
            pl.BlockSpec((None, 1, ff2), lambda i, be, nu: (be[i], 0, 0)),
            pl.BlockSpec((None, ff2 // 2, d_model), lambda i, be, nu: (be[i], 0, 0)),
            pl.BlockSpec((None, 1, d_model), lambda i, be, nu: (be[i], 0, 0)),
        ],
        out_specs=pl.BlockSpec((rows * SUBLANES, LANES), lambda i, be, nu: (i, 0)),
        scratch_shapes=[
            pltpu.SMEM((2, rows), jnp.int32),
            pltpu.VMEM((2, rows * SUBLANES, LANES), F32),
            pltpu.VMEM((d_model, ff2), BF16),
            pltpu.VMEM((ff2 // 2, d_model), BF16),
            pltpu.SemaphoreType.DMA((2,)),
            pltpu.SemaphoreType.DMA((2,)),
        ],
    )
    return pl.pallas_call(
        functools.partial(_moe_kernel, rows=rows),
        grid_spec=grid_spec,
        out_shape=jax.ShapeDtypeStruct((n_blocks * rows * SUBLANES, LANES), F32),
        compiler_params=_cparams(("arbitrary",)),
        name="moe",
    )(block_exp, n_used, row_tok, h2_tiles, w_gate_up, b_gate_up[:, None, :], w_down, b_down[:, None, :])


def _combine_kernel(dest_ref, y_ref, gate_ref, x1_ref, g_ref, o_ref, idx_smem, ybuf, isem, gsem, *, tm):
    i = pl.program_id(0)
    n_tiles = pl.num_programs(0)
    slot = lax.rem(i, 2)
    nxt = 1 - slot
    n_rows = TOP_K * tm

    def idx_copy(blk, s):
        return pltpu.make_async_copy(dest_ref.at[pl.ds(blk, 1)], idx_smem.at[pl.ds(s, 1)], isem.at[s])

    @pl.when(i == 0)
    def _():
        idx_copy(0, 0).start()
        idx_copy(0, 0).wait()
        _row_gather(idx_smem, 0, n_rows, y_ref, ybuf.at[0], gsem.at[0])

    @pl.when(i + 1 < n_tiles)
    def _():
        idx_copy(i + 1, nxt).start()

    _row_gather_wait(n_rows, y_ref, ybuf.at[slot], gsem.at[slot])

    @pl.when(i + 1 < n_tiles)
    def _():
        idx_copy(i + 1, nxt).wait()
        _row_gather(idx_smem, nxt, n_rows, y_ref, ybuf.at[nxt], gsem.at[nxt])

    gate = gate_ref[...]
    out = x1_ref[...]
    for k in range(TOP_K):
        y_k = jnp.concatenate(
            [ybuf[slot, pl.ds(k * tm * SUBLANES + j, tm, stride=SUBLANES), :] for j in range(SUBLANES)], axis=1)
        out = out + gate[:, k:k + 1] * y_k
    ms = jnp.mean(out * out, axis=-1, keepdims=True)
    o_ref[...] = out * lax.rsqrt(ms + EPS) * g_ref[...]


def _combine(dest_tiles, y_tiles, gate, x1, g_final, tm):
    tokens, d_model = x1.shape
    row = lambda i: (i, 0)
    return pl.pallas_call(
        functools.partial(_combine_kernel, tm=tm),
        grid=(tokens // tm,),
        in_specs=[
            pl.BlockSpec(memory_space=pl.ANY),
            pl.BlockSpec(memory_space=pl.ANY),
            pl.BlockSpec((tm, LANES), row),
            pl.BlockSpec((tm, d_model), row),
            pl.BlockSpec((1, d_model), lambda i: (0, 0)),
        ],
        out_specs=pl.BlockSpec((tm, d_model), row),
        out_shape=jax.ShapeDtypeStruct((tokens, d_model), F32),
        scratch_shapes=[
            pltpu.SMEM((2, TOP_K * tm), jnp.int32),
            pltpu.VMEM((2, TOP_K * tm * SUBLANES, LANES), F32),
            pltpu.SemaphoreType.DMA((2,)),
            pltpu.SemaphoreType.DMA((2,)),
        ],
        compiler_params=_cparams(("arbitrary",)),
        name="combine",
    )(dest_tiles, y_tiles, gate, x1, g_final.astype(F32)[None, :])


PROJ_ROWS = 256
MID_ROWS = 512
ATTN_Q_ROWS = 512
ATTN_K_ROWS = 512
WINDOW_BLOCKS_PER_STEP = 4
MOE_ROWS = 256
COMBINE_ROWS = 128


def _layer(x2d, batch, seq, g_mix, w_in, qa_norm, ka_norm, sink_b, w_branch_a, w_branch_b, w_out,
           g_ffn, w_router, b_router, w_gate_up, b_gate_up, w_down, b_down, g_final):
    tokens = x2d.shape[0]
    qa, ka, va, qb, kb, vb, sga, sgb = _project(x2d, g_mix, w_in, qa_norm, ka_norm, seq, min(PROJ_ROWS, seq))
    oa = _attention_global(qa, ka, va, batch, seq, ATTN_Q_ROWS, min(ATTN_K_ROWS, seq // 2))
    ob = _attention_window(qb, kb, vb, sink_b, batch, seq)
    x1, h2_tiles, eidx, rank, gate, cnt = _mid(
        oa, ob, sga, sgb, x2d, w_branch_a, w_branch_b, w_out, g_ffn, w_router, b_router, min(MID_ROWS, tokens))

    rows = MOE_ROWS
    n_assign = tokens * TOP_K
    n_blocks = n_assign // rows + N_EXPERTS
    counts = cnt[0, :N_EXPERTS].astype(jnp.int32)
    padded = (counts + rows - 1) // rows * rows
    ends = jnp.cumsum(padded)
    starts = ends - padded
    dest = starts[eidx[:, :TOP_K]] + rank[:, :TOP_K]
    row_tok = jnp.zeros((n_blocks * rows,), jnp.int32).at[dest.reshape(-1)].set(
        jnp.arange(n_assign, dtype=jnp.int32) // TOP_K)
    block_start = jnp.arange(n_blocks, dtype=jnp.int32) * rows
    block_exp = jnp.minimum(
        jnp.sum((ends[None, :] <= block_start[:, None]).astype(jnp.int32), axis=1), N_EXPERTS - 1)
    n_used = (ends[-1:] // rows).astype(jnp.int32)

    y_tiles = _moe(block_exp, n_used, row_tok.reshape(n_blocks, rows), h2_tiles,
                   w_gate_up, b_gate_up, w_down, b_down, rows)
    tm = min(COMBINE_ROWS, tokens)
    dest_tiles = dest.reshape(tokens // tm, tm, TOP_K).transpose(0, 2, 1).reshape(tokens // tm, TOP_K * tm)
    return _combine(dest_tiles, y_tiles, gate, x1, g_final, tm)


def kernel(x, g_mix, w_in, qa_norm, ka_norm, sink_b, w_branch_a, w_branch_b, w_out, g_ffn, w_router, b_router,
           w_gate_up, b_gate_up, w_down, b_down, g_final):
    batch, seq, d_model = x.shape
    assert g_mix.shape[0] == 1, "single-layer block"
    out = _layer(x.reshape(batch * seq, d_model), batch, seq, g_mix[0], w_in[0], qa_norm[0], ka_norm[0],
                 sink_b[0], w_branch_a[0], w_branch_b[0], w_out[0], g_ffn[0], w_router[0], b_router[0],
                 w_gate_up[0], b_gate_up[0], w_down[0], b_down[0], g_final)
    return out.reshape(batch, seq, d_model)
```

```python
import functools

import jax
import jax.numpy as jnp
from jax import lax
from jax.experimental import pallas as pl
from jax.experimental.pallas import tpu as pltpu

HEAD_DIM = 64
Q_HEADS = 8
KV_HEADS = 2
GROUPS = Q_HEADS // KV_HEADS
Q_COLS = Q_HEADS * HEAD_DIM
KV_COLS = KV_HEADS * HEAD_DIM
GROUP_COLS = GROUPS * HEAD_DIM
WINDOW = 128
GRID_W = 64
ROPE_THETA = 10000.0
N_EXPERTS = 32
TOP_K = 4
D_FF = 1024
SWIGLU_LIMIT = 7.0
SWIGLU_ALPHA = 1.702
EPS = 1e-6
NEG_INF = -1e30
LOG2_E = 1.4426950408889634

LANES = 128
SUBLANES = 8
V_ROWS = 80
VMEM_LIMIT = 56 * 1024 * 1024

F32 = jnp.float32
BF16 = jnp.bfloat16


def _cparams(sem, vmem=VMEM_LIMIT):
    return pltpu.CompilerParams(dimension_semantics=sem, vmem_limit_bytes=vmem)


def _rope(y, cos, sin_signed):
    lane = lax.broadcasted_iota(jnp.int32, y.shape, 1)
    first_half = (lane & 32) == 0
    rot = jnp.where(first_half, pltpu.roll(y, 96, 1), pltpu.roll(y, 32, 1))
    return y * cos + rot * sin_signed


def _head_rmsnorm(q, ones_blockdiag, gain):
    q2 = q * q
    hi = q2.astype(BF16)
    lo = (q2 - hi.astype(F32)).astype(BF16)
    ss = (jnp.dot(hi, ones_blockdiag, preferred_element_type=F32)
          + jnp.dot(lo, ones_blockdiag, preferred_element_type=F32))
    return q * lax.rsqrt(ss * (1.0 / HEAD_DIM) + EPS) * gain


def _proj_kernel(x_ref, g_ref, w_ref, cosa_ref, sina_ref, cos1_ref, sin1_ref, qg_ref, kg_ref, bd_ref,
                 qa_ref, ka_ref, va_ref, qb_ref, kb_ref, vb_ref, sga_ref, sgb_ref, *, d_model):
    x = x_ref[...]
    ms = jnp.mean(x * x, axis=-1, keepdims=True)
    h = (x * lax.rsqrt(ms + EPS) * g_ref[...]).astype(BF16)
    scale = HEAD_DIM ** -0.5 * LOG2_E

    def proj(c0, width):
        return jnp.dot(h, w_ref[:, c0:c0 + width], preferred_element_type=F32)

    cosa, sina = cosa_ref[...], sina_ref[...]
    cos1, sin1 = cos1_ref[...], sin1_ref[...]
    bd = bd_ref[...]
    c = 0
    for s in range(Q_COLS // LANES):
        q = _head_rmsnorm(proj(c, LANES), bd, qg_ref[...])
        qa_ref[:, s * LANES:(s + 1) * LANES] = (_rope(q, cosa, sina) * scale).astype(BF16)
        c += LANES
    k = _head_rmsnorm(proj(c, LANES), bd, kg_ref[...])
    ka_ref[...] = _rope(k, cosa, sina).astype(BF16)
    c += LANES
    va_ref[...] = proj(c, LANES).astype(BF16)
    c += LANES
    for s in range(Q_COLS // LANES):
        qb_ref[:, s * LANES:(s + 1) * LANES] = (_rope(proj(c, LANES), cos1, sin1) * scale).astype(BF16)
        c += LANES
    kb_ref[...] = _rope(proj(c, LANES), cos1, sin1).astype(BF16)
    c += LANES
    vb_ref[...] = proj(c, LANES).astype(BF16)
    c += LANES
    sga_ref[...] = jax.nn.sigmoid(proj(c, d_model)).astype(BF16)
    c += d_model
    sgb_ref[...] = jax.nn.sigmoid(proj(c, d_model)).astype(BF16)


def _rope_tables(seq):
    def slab(ang):
        cos, sin = jnp.cos(ang), jnp.sin(ang)
        cos64 = jnp.concatenate([cos, cos], axis=-1)
        sin64 = jnp.concatenate([-sin, sin], axis=-1)
        return jnp.tile(cos64, (1, 2)), jnp.tile(sin64, (1, 2))

    pos = jnp.arange(seq, dtype=F32)
    inv1 = ROPE_THETA ** (-jnp.arange(0, HEAD_DIM, 2, dtype=F32) / HEAD_DIM)
    ang1 = pos[:, None] * inv1[None, :]
    rows = seq // GRID_W
    row = jnp.broadcast_to(jnp.arange(rows, dtype=F32)[:, None], (rows, GRID_W)).reshape(-1)
    col = jnp.broadcast_to(jnp.arange(GRID_W, dtype=F32)[None, :], (rows, GRID_W)).reshape(-1)
    axis_dim = HEAD_DIM // 2
    inva = ROPE_THETA ** (-jnp.arange(0, axis_dim, 2, dtype=F32) / axis_dim)
    anga = jnp.concatenate([row[:, None] * inva[None, :], col[:, None] * inva[None, :]], axis=-1)
    return slab(anga), slab(ang1)


def _project(x2d, g_mix, w_in, qa_norm, ka_norm, seq, tm):
    tokens, d_model = x2d.shape
    in_cols = w_in.shape[1]
    (cosa, sina), (cos1, sin1) = _rope_tables(seq)
    head_id = jnp.arange(LANES) // HEAD_DIM
    ones_blockdiag = (head_id[:, None] == head_id[None, :]).astype(BF16)
    qg = jnp.tile(qa_norm.astype(F32), 2)[None, :]
    kg = jnp.tile(ka_norm.astype(F32), 2)[None, :]
    seq_tiles = seq // tm
    row = lambda i: (i, 0)
    pos = lambda i: (i % seq_tiles, 0)
    fixed = lambda i: (0, 0)
    bf = lambda cols: jax.ShapeDtypeStruct((tokens, cols), BF16)
    out_cols = [Q_COLS, KV_COLS, KV_COLS, Q_COLS, KV_COLS, KV_COLS, d_model, d_model]
    return pl.pallas_call(
        functools.partial(_proj_kernel, d_model=d_model),
        grid=(tokens // tm,),
        in_specs=[
            pl.BlockSpec((tm, d_model), row),
            pl.BlockSpec((1, d_model), fixed),
            pl.BlockSpec((d_model, in_cols), fixed),
            pl.BlockSpec((tm, LANES), pos), pl.BlockSpec((tm, LANES), pos),
            pl.BlockSpec((tm, LANES), pos), pl.BlockSpec((tm, LANES), pos),
            pl.BlockSpec((1, LANES), fixed), pl.BlockSpec((1, LANES), fixed),
            pl.BlockSpec((LANES, LANES), fixed),
        ],
        out_specs=[pl.BlockSpec((tm, c), row) for c in out_cols],
        out_shape=[bf(c) for c in out_cols],
        compiler_params=_cparams(("arbitrary",)),
        name="proj",
    )(x2d, g_mix.astype(F32)[None, :], w_in.astype(BF16), cosa, sina, cos1, sin1, qg, kg, ones_blockdiag)


def _stack_group_queries(q):
    return jnp.concatenate([q[:, g * HEAD_DIM:(g + 1) * HEAD_DIM] for g in range(GROUPS)], axis=0)


def _unstack_group_outputs(o_t, tq):
    pad = jnp.zeros((LANES - HEAD_DIM, o_t.shape[1]), o_t.dtype)
    o_pad = jnp.concatenate([o_t, pad], axis=0)
    heads = [o_pad[:, g * tq:(g + 1) * tq].T[:, :HEAD_DIM] for g in range(GROUPS)]
    return jnp.concatenate(heads, axis=1)


def _attn_global_kernel(q_ref, k_ref, vt_ref, o_ref, s0, s1, p0, p1, acc_ref, *, tq, tk, n_chunks):
    q_st = _stack_group_queries(q_ref[...])
    m_rows = q_st.shape[0]

    def scores(c, s_ref):
        k_c = k_ref[pl.ds(pl.multiple_of(c * tk, tk), tk), :]
        s_ref[...] = lax.dot_general(k_c, q_st, (((1,), (1,)), ((), ())), preferred_element_type=F32)

    def softmax(s_ref, p_ref, m_prev):
        s = s_ref[...]
        m_new = jnp.maximum(m_prev, jnp.max(s, axis=0, keepdims=True))
        p_ref[...] = jnp.exp2(s - m_new).astype(BF16)
        return m_new, jnp.exp2(m_prev - m_new)

    def values(c, p_ref, alpha):
        acc_ref[...] = acc_ref[...] * alpha + jnp.dot(vt_ref[c], p_ref[...], preferred_element_type=F32)

    scores(0, s0)
    p1[...] = jnp.zeros(p1.shape, BF16)
    acc_ref[...] = jnp.zeros(acc_ref.shape, F32)

    def pair(j, carry):
        m, alpha = carry
        c = 2 * j
        scores(c + 1, s1)
        m_a, alpha_a = softmax(s0, p0, m)
        values(jnp.maximum(c - 1, 0), p1, alpha)
        scores(jnp.minimum(c + 2, n_chunks - 1), s0)
        m_b, alpha_b = softmax(s1, p1, m_a)
        values(c, p0, alpha_a)
        return m_b, alpha_b

    init = (jnp.full((1, m_rows), NEG_INF, F32), jnp.ones((1, m_rows), F32))
    _, alpha = lax.fori_loop(0, n_chunks // 2, pair, init)
    values(n_chunks - 1, p1, alpha)
    acc = acc_ref[...]
    o_t = acc[:HEAD_DIM, :] / acc[HEAD_DIM:HEAD_DIM + 1, :]
    o_ref[...] = _unstack_group_outputs(o_t, tq).astype(o_ref.dtype)


def _attn_window_kernel(q_ref, k_ref, vt_ref, sink_ref, o_ref, *, tq, n_blocks, blocks_per_step):
    span = 3 * tq
    sink = sink_ref[...]
    for i in range(blocks_per_step):
        n = pl.program_id(2) * blocks_per_step + i
        c0 = jnp.clip(n - 1, 0, n_blocks - 3)
        start = pl.multiple_of(c0 * tq, tq)
        q_st = _stack_group_queries(q_ref[i * tq:(i + 1) * tq, :])
        k_w = k_ref[pl.ds(start, span), :]
        s = lax.dot_general(k_w, q_st, (((1,), (1,)), ((), ())), preferred_element_type=F32)
        kpos = start + lax.broadcasted_iota(jnp.int32, s.shape, 0)
        qpos = n * tq + (lax.broadcasted_iota(jnp.int32, s.shape, 1) & (tq - 1))
        s = jnp.where(jnp.abs(kpos - qpos) <= WINDOW, s, NEG_INF)
        m = jnp.maximum(jnp.max(s, axis=0, keepdims=True), sink)
        p = jnp.exp2(s - m).astype(BF16)
        acc = jnp.dot(vt_ref[c0], p[:tq], preferred_element_type=F32)
        for j in range(1, 3):
            acc += jnp.dot(vt_ref[c0 + j], p[j * tq:(j + 1) * tq], preferred_element_type=F32)
        denom = acc[HEAD_DIM:HEAD_DIM + 1, :] + jnp.exp2(sink - m)
        o_ref[i * tq:(i + 1) * tq, :] = _unstack_group_outputs(acc[:HEAD_DIM, :] / denom, tq).astype(o_ref.dtype)


def _kv_layouts(k2d, v2d, batch, seq, chunk):
    k_hm = k2d.reshape(batch, seq, KV_HEADS, HEAD_DIM).transpose(0, 2, 1, 3)
    v_t = v2d.reshape(batch, seq // chunk, chunk, KV_HEADS, HEAD_DIM).transpose(0, 3, 1, 4, 2)
    ones = jnp.ones(v_t.shape[:3] + (V_ROWS - HEAD_DIM, chunk), v_t.dtype)
    return k_hm, jnp.concatenate([v_t, ones], axis=3)


def _attention_global(q2d, k2d, v2d, batch, seq, tq, tk):
    tokens = q2d.shape[0]
    n_chunks = seq // tk
    assert n_chunks % 2 == 0, "the chunk pipeline advances two chunks per step"
    k_hm, v_t = _kv_layouts(k2d, v2d, batch, seq, tk)
    q_tiles = seq // tq
    m_rows = GROUPS * tq
    qmap = lambda b, h, i: (b * q_tiles + i, h)
    return pl.pallas_call(
        functools.partial(_attn_global_kernel, tq=tq, tk=tk, n_chunks=n_chunks),
        grid=(batch, KV_HEADS, q_tiles),
        in_specs=[
            pl.BlockSpec((tq, GROUP_COLS), qmap),
            pl.BlockSpec((None, None, seq, HEAD_DIM), lambda b, h, i: (b, h, 0, 0)),
            pl.BlockSpec((None, None, n_chunks, V_ROWS, tk), lambda b, h, i: (b, h, 0, 0, 0)),
        ],
        out_specs=pl.BlockSpec((tq, GROUP_COLS), qmap),
        out_shape=jax.ShapeDtypeStruct((tokens, Q_COLS), BF16),
        scratch_shapes=[pltpu.VMEM((tk, m_rows), F32), pltpu.VMEM((tk, m_rows), F32),
                        pltpu.VMEM((tk, m_rows), BF16), pltpu.VMEM((tk, m_rows), BF16),
                        pltpu.VMEM((V_ROWS, m_rows), F32)],
        compiler_params=_cparams(("arbitrary", "arbitrary", "arbitrary")),
        name="attn_global",
    )(q2d, k_hm, v_t)


def _attention_window(q2d, k2d, v2d, sink, batch, seq):
    tokens = q2d.shape[0]
    tq = WINDOW
    n_blocks = seq // tq
    assert n_blocks >= 3
    k_hm, v_t = _kv_layouts(k2d, v2d, batch, seq, tq)
    m_rows = GROUPS * tq
    sink_rows = jnp.repeat(sink.astype(F32).reshape(KV_HEADS, GROUPS) * LOG2_E, tq, axis=1)[:, None, :]
    per_step = WINDOW_BLOCKS_PER_STEP if n_blocks % WINDOW_BLOCKS_PER_STEP == 0 else 1
    steps = n_blocks // per_step
    qmap = lambda b, h, i: (b * steps + i, h)
    return pl.pallas_call(
        functools.partial(_attn_window_kernel, tq=tq, n_blocks=n_blocks, blocks_per_step=per_step),
        grid=(batch, KV_HEADS, steps),
        in_specs=[
            pl.BlockSpec((per_step * tq, GROUP_COLS), qmap),
            pl.BlockSpec((None, None, seq, HEAD_DIM), lambda b, h, i: (b, h, 0, 0)),
            pl.BlockSpec((None, None, n_blocks, V_ROWS, tq), lambda b, h, i: (b, h, 0, 0, 0)),
            pl.BlockSpec((None, 1, m_rows), lambda b, h, i: (h, 0, 0)),
        ],
        out_specs=pl.BlockSpec((per_step * tq, GROUP_COLS), qmap),
        out_shape=jax.ShapeDtypeStruct((tokens, Q_COLS), BF16),
        compiler_params=_cparams(("arbitrary", "arbitrary", "arbitrary")),
        name="attn_window",
    )(q2d, k_hm, v_t, sink_rows)


def _store_token_tiles(ref, val):
    rows = val.shape[0]
    for j in range(val.shape[1] // LANES):
        ref[pl.ds(j, rows, stride=SUBLANES), :] = val[:, j * LANES:(j + 1) * LANES]


def _load_token_tiles(ref, rows):
    return jnp.concatenate([ref[pl.ds(j, rows, stride=SUBLANES), :] for j in range(SUBLANES)], axis=1)


def _mid_kernel(oa_ref, ob_ref, sga_ref, sgb_ref, x_ref, wa_ref, wb_ref, wo_ref, g_ref,
                wr_hi_ref, wr_lo_ref, br_ref, tri_ref,
                x1_ref, h2_ref, eidx_ref, rank_ref, gate_ref, cnt_ref):
    @pl.when(pl.program_id(0) == 0)
    def _():
        cnt_ref[...] = jnp.zeros(cnt_ref.shape, F32)

    a = jnp.dot(oa_ref[...], wa_ref[...], preferred_element_type=F32)
    b = jnp.dot(ob_ref[...], wb_ref[...], preferred_element_type=F32)
    merged = sga_ref[...].astype(F32) * a + sgb_ref[...].astype(F32) * b
    x1 = x_ref[...] + jnp.dot(merged.astype(BF16), wo_ref[...], preferred_element_type=F32)
    x1_ref[...] = x1
    ms = jnp.mean(x1 * x1, axis=-1, keepdims=True)
    h2 = x1 * lax.rsqrt(ms + EPS) * g_ref[...]
    _store_token_tiles(h2_ref, h2)

    hi = h2.astype(BF16)
    lo = (h2 - hi.astype(F32)).astype(BF16)
    wr_hi = wr_hi_ref[...]
    logits = (jnp.dot(hi, wr_hi, preferred_element_type=F32)
              + jnp.dot(lo, wr_hi, preferred_element_type=F32)
              + jnp.dot(hi, wr_lo_ref[...], preferred_element_type=F32)) + br_ref[...]
    lane = lax.broadcasted_iota(jnp.int32, logits.shape, 1)
    work = jnp.where(lane < N_EXPERTS, logits, -jnp.inf)
    vals, idxs, sels = [], [], []
    for _ in range(TOP_K):
        mx = jnp.max(work, axis=1, keepdims=True)
        ix = jnp.min(jnp.where(work == mx, lane, LANES), axis=1, keepdims=True)
        sel = lane == ix
        work = jnp.where(sel, -jnp.inf, work)
        vals.append(mx)
        idxs.append(ix)
        sels.append(sel)
    exps = [jnp.exp(v - vals[0]) for v in vals]
    denom = exps[0] + exps[1] + exps[2] + exps[3]
    chosen = sels[0] | sels[1] | sels[2] | sels[3]
    onehot = jnp.where(chosen, 1.0, 0.0)
    before = jnp.dot(tri_ref[...], onehot.astype(BF16), preferred_element_type=F32) + cnt_ref[0:1, :]
    eidx = jnp.zeros(logits.shape, jnp.int32)
    rank = jnp.zeros(logits.shape, F32)
    gate = jnp.zeros(logits.shape, F32)
    for k in range(TOP_K):
        here = lane == k
        eidx = jnp.where(here, idxs[k], eidx)
        rank = jnp.where(here, jnp.sum(jnp.where(sels[k], before, 0.0), axis=1, keepdims=True), rank)
        gate = jnp.where(here, exps[k] / denom, gate)
    eidx_ref[...] = eidx
    rank_ref[...] = rank.astype(jnp.int32)
    gate_ref[...] = gate
    cnt_ref[...] = cnt_ref[...] + jnp.sum(onehot, axis=0, keepdims=True)


def _mid(oa, ob, sga, sgb, x2d, w_a, w_b, w_o, g_ffn, w_router, b_router, tm):
    tokens, d_model = x2d.shape
    wr = jnp.zeros((d_model, LANES), F32).at[:, :N_EXPERTS].set(w_router.astype(F32))
    wr_hi = wr.astype(BF16)
    wr_lo = (wr - wr_hi.astype(F32)).astype(BF16)
    br = jnp.zeros((1, LANES), F32).at[0, :N_EXPERTS].set(b_router.astype(F32))
    tri = (jnp.arange(tm)[:, None] > jnp.arange(tm)[None, :]).astype(BF16)
    row = lambda i: (i, 0)
    fixed = lambda i: (0, 0)
    return pl.pallas_call(
        _mid_kernel,
        grid=(tokens // tm,),
        in_specs=[
            pl.BlockSpec((tm, Q_COLS), row), pl.BlockSpec((tm, Q_COLS), row),
            pl.BlockSpec((tm, d_model), row), pl.BlockSpec((tm, d_model), row),
            pl.BlockSpec((tm, d_model), row),
            pl.BlockSpec((Q_COLS, d_model), fixed), pl.BlockSpec((Q_COLS, d_model), fixed),
            pl.BlockSpec((d_model, d_model), fixed),
            pl.BlockSpec((1, d_model), fixed),
            pl.BlockSpec((d_model, LANES), fixed), pl.BlockSpec((d_model, LANES), fixed),
            pl.BlockSpec((1, LANES), fixed),
            pl.BlockSpec((tm, tm), fixed),
        ],
        out_specs=[
            pl.BlockSpec((tm, d_model), row),
            pl.BlockSpec((tm * SUBLANES, LANES), row),
            pl.BlockSpec((tm, LANES), row), pl.BlockSpec((tm, LANES), row), pl.BlockSpec((tm, LANES), row),
            pl.BlockSpec((SUBLANES, LANES), fixed),
        ],
        out_shape=[
            jax.ShapeDtypeStruct((tokens, d_model), F32),
            jax.ShapeDtypeStruct((tokens * SUBLANES, LANES), F32),
            jax.ShapeDtypeStruct((tokens, LANES), jnp.int32),
            jax.ShapeDtypeStruct((tokens, LANES), jnp.int32),
            jax.ShapeDtypeStruct((tokens, LANES), F32),
            jax.ShapeDtypeStruct((SUBLANES, LANES), F32),
        ],
        compiler_params=_cparams(("arbitrary",)),
        name="mid",
    )(oa, ob, sga, sgb, x2d, w_a.astype(BF16), w_b.astype(BF16), w_o.astype(BF16),
      g_ffn.astype(F32)[None, :], wr_hi, wr_lo, br, tri)


def _tile_copy(src_ref, src_row, dst_ref, dst_row, sem):
    return pltpu.make_async_copy(
        src_ref.at[pl.ds(pl.multiple_of(src_row, SUBLANES), SUBLANES)],
        dst_ref.at[pl.ds(pl.multiple_of(dst_row, SUBLANES), SUBLANES)], sem)


def _start_tile_gather(off_smem, base, n_rows, src_ref, dst_ref, sem):
    def start(r, carry):
        _tile_copy(src_ref, off_smem[base + r], dst_ref, r * SUBLANES, sem).start()
        return carry
    lax.fori_loop(0, n_rows, start, 0, unroll=8)


def _start_tile_scatter(off_smem, base, n_rows, src_ref, dst_ref, sem):
    def start(r, carry):
        _tile_copy(src_ref, r * SUBLANES, dst_ref, off_smem[base + r], sem).start()
        return carry
    lax.fori_loop(0, n_rows, start, 0, unroll=8)


def _wait_tiles(n_rows, src_ref, dst_ref, sem):
    window = pl.ds(0, n_rows * SUBLANES)
    pltpu.make_async_copy(src_ref.at[window], dst_ref.at[window], sem).wait()


def _moe_kernel(bexp_ref, nused_ref, src_ref, dst_ref, h2_ref, wgu_ref, bgu_ref, wd_ref, bd_ref,
                y_ref, src_smem, dst_smem, xbuf, ybuf, wgu_bf, wd_bf, isem, gsem, ssem, *, rows, n_assign):
    i = pl.program_id(0)
    n_steps = pl.num_programs(0)
    n_used = nused_ref[0]
    slot = lax.rem(i, 2)
    nxt = 1 - slot

    def idx_copies(blk, s):
        window = pl.ds(pl.multiple_of(s * rows, rows), rows)
        return (pltpu.make_async_copy(src_ref.at[blk], src_smem.at[window], isem.at[0, s]),
                pltpu.make_async_copy(dst_ref.at[blk], dst_smem.at[window], isem.at[1, s]))

    def wait_scatter(s):
        _wait_tiles(rows, ybuf.at[s], y_ref, ssem.at[s])

    @pl.when(i == 0)
    def _():
        ybuf[...] = jnp.zeros(ybuf.shape, F32)
        for s in range(2):
            dump = pl.ds((n_assign + s * rows) * SUBLANES, rows * SUBLANES)
            zero_fill = pltpu.make_async_copy(ybuf.at[s], y_ref.at[dump], ssem.at[s])
            zero_fill.start()
            zero_fill.wait()
        for c in idx_copies(0, 0):
            c.start()
        for c in idx_copies(0, 0):
            c.wait()
        _start_tile_gather(src_smem, 0, rows, h2_ref, xbuf.at[0], gsem.at[0])

    @pl.when(i + 1 < n_used)
    def _():
        for c in idx_copies(i + 1, nxt):
            c.start()

    @pl.when(i < n_used)
    def _():
        _wait_tiles(rows, h2_ref, xbuf.at[slot], gsem.at[slot])

    @pl.when(i + 1 < n_used)
    def _():
        for c in idx_copies(i + 1, nxt):
            c.wait()
        _start_tile_gather(src_smem, nxt * rows, rows, h2_ref, xbuf.at[nxt], gsem.at[nxt])

    @pl.when(jnp.logical_and(i >= 2, i - 2 < n_used))
    def _():
        wait_scatter(slot)

    new_expert = jnp.logical_or(i == 0, bexp_ref[i] != bexp_ref[jnp.maximum(i - 1, 0)])

    @pl.when(jnp.logical_and(new_expert, i < n_used))
    def _():
        wgu_bf[...] = wgu_ref[...].astype(BF16)
        wd_bf[...] = wd_ref[...].astype(BF16)

    @pl.when(i < n_used)
    def _():
        x = _load_token_tiles(xbuf.at[slot], rows).astype(BF16)
        gu = jnp.dot(x, wgu_bf[...], preferred_element_type=F32) + bgu_ref[...]
        gate = jnp.minimum(gu[:, :D_FF], SWIGLU_LIMIT)
        up = jnp.clip(gu[:, D_FF:], -SWIGLU_LIMIT, SWIGLU_LIMIT)
        act = (up + 1.0) * (gate * jax.nn.sigmoid(SWIGLU_ALPHA * gate))
        y = jnp.dot(act.astype(BF16), wd_bf[...], preferred_element_type=F32) + bd_ref[...]
        _store_token_tiles(ybuf.at[slot], y)
        _start_tile_scatter(dst_smem, slot * rows, rows, ybuf.at[slot], y_ref, ssem.at[slot])

    @pl.when(i == n_steps - 1)
    def _():
        @pl.when(jnp.logical_and(i >= 1, i - 1 < n_used))
        def _():
            wait_scatter(nxt)

        @pl.when(i < n_used)
        def _():
            wait_scatter(slot)


def _moe(block_exp, n_used, src_rows, dst_rows, h2_tiles, w_gate_up, b_gate_up, w_down, b_down, rows, n_assign):
    n_blocks = src_rows.shape[0]
    n_exp, d_model, ff2 = w_gate_up.shape
    grid_spec = pltpu.PrefetchScalarGridSpec(
        num_scalar_prefetch=2,
        grid=(n_blocks,),
        in_specs=[
            pl.BlockSpec(memory_space=pl.ANY),
            pl.BlockSpec(memory_space=pl.ANY),
            pl.BlockSpec(memory_space=pl.ANY),
            pl.BlockSpec((None, d_model, ff2), lambda i, be, nu: (be[i], 0, 0)),
            pl.BlockSpec((None, 1, ff2), lambda i, be, nu: (be[i], 0, 0)),
            pl.BlockSpec((None, ff2 // 2, d_model), lambda i, be, nu: (be[i], 0, 0)),
            pl.BlockSpec((None, 1, d_model), lambda i, be, nu: (be[i], 0, 0)),
        ],
        out_specs=pl.BlockSpec(memory_space=pl.ANY),
        scratch_shapes=[
            pltpu.SMEM((2 * rows,), jnp.int32),
            pltpu.SMEM((2 * rows,), jnp.int32),
            pltpu.VMEM((2, rows * SUBLANES, LANES), F32),
            pltpu.VMEM((2, rows * SUBLANES, LANES), F32),
            pltpu.VMEM((d_model, ff2), BF16),
            pltpu.VMEM((ff2 // 2, d_model), BF16),
            pltpu.SemaphoreType.DMA((2, 2)),
            pltpu.SemaphoreType.DMA((2,)),
            pltpu.SemaphoreType.DMA((2,)),
        ],
    )
    return pl.pallas_call(
        functools.partial(_moe_kernel, rows=rows, n_assign=n_assign),
        grid_spec=grid_spec,
        out_shape=jax.ShapeDtypeStruct(((n_assign + 2 * rows) * SUBLANES, LANES), F32),
        compiler_params=_cparams(("arbitrary",)),
        name="moe",
    )(block_exp, n_used, src_rows, dst_rows, h2_tiles, w_gate_up, b_gate_up[:, None, :], w_down,
      b_down[:, None, :])


def _combine_kernel(y_ref, gate_ref, x1_ref, g_ref, o_ref, *, tm):
    gate = gate_ref[...]
    out = x1_ref[...]
    stride = TOP_K * SUBLANES
    for k in range(TOP_K):
        y_k = jnp.concatenate(
            [y_ref[pl.ds(k * SUBLANES + j, tm, stride=stride), :] for j in range(SUBLANES)], axis=1)
        out = out + gate[:, k:k + 1] * y_k
    ms = jnp.mean(out * out, axis=-1, keepdims=True)
    o_ref[...] = out * lax.rsqrt(ms + EPS) * g_ref[...]


def _combine(y_tiles, gate, x1, g_final, tm):
    tokens, d_model = x1.shape
    row = lambda i: (i, 0)
    return pl.pallas_call(
        functools.partial(_combine_kernel, tm=tm),
        grid=(tokens // tm,),
        in_specs=[
            pl.BlockSpec((tm * TOP_K * SUBLANES, LANES), row),
            pl.BlockSpec((tm, LANES), row),
            pl.BlockSpec((tm, d_model), row),
            pl.BlockSpec((1, d_model), lambda i: (0, 0)),
        ],
        out_specs=pl.BlockSpec((tm, d_model), row),
        out_shape=jax.ShapeDtypeStruct((tokens, d_model), F32),
        compiler_params=_cparams(("arbitrary",)),
        name="combine",
    )(y_tiles, gate, x1, g_final.astype(F32)[None, :])


PROJ_ROWS = 256
MID_ROWS = 512
ATTN_Q_ROWS = 512
ATTN_K_ROWS = 512
WINDOW_BLOCKS_PER_STEP = 4
MOE_ROWS = 256
COMBINE_ROWS = 128


def _layer(x2d, batch, seq, g_mix, w_in, qa_norm, ka_norm, sink_b, w_branch_a, w_branch_b, w_out,
           g_ffn, w_router, b_router, w_gate_up, b_gate_up, w_down, b_down, g_final):
    tokens = x2d.shape[0]
    qa, ka, va, qb, kb, vb, sga, sgb = _project(x2d, g_mix, w_in, qa_norm, ka_norm, seq, min(PROJ_ROWS, seq))
    oa = _attention_global(qa, ka, va, batch, seq, ATTN_Q_ROWS, min(ATTN_K_ROWS, seq // 2))
    ob = _attention_window(qb, kb, vb, sink_b, batch, seq)
    x1, h2_tiles, eidx, rank, gate, cnt = _mid(
        oa, ob, sga, sgb, x2d, w_branch_a, w_branch_b, w_out, g_ffn, w_router, b_router, min(MID_ROWS, tokens))

    rows = MOE_ROWS
    n_assign = tokens * TOP_K
    n_blocks = n_assign // rows + N_EXPERTS
    counts = cnt[0, :N_EXPERTS].astype(jnp.int32)
    padded = (counts + rows - 1) // rows * rows
    ends = jnp.cumsum(padded)
    starts = ends - padded
    dest = starts[eidx[:, :TOP_K]] + rank[:, :TOP_K]
    row_assign = jnp.full((n_blocks * rows,), -1, jnp.int32).at[dest.reshape(-1)].set(
        jnp.arange(n_assign, dtype=jnp.int32), unique_indices=True).reshape(n_blocks, rows)
    real = row_assign >= 0
    dump_slot = (n_assign + (jnp.arange(n_blocks, dtype=jnp.int32) % 2)[:, None] * rows
                 + jnp.arange(rows, dtype=jnp.int32)[None, :])
    src_rows = jnp.where(real, row_assign // TOP_K, 0) * SUBLANES
    dst_rows = jnp.where(real, row_assign, dump_slot) * SUBLANES
    block_start = jnp.arange(n_blocks, dtype=jnp.int32) * rows
    block_exp = jnp.minimum(
        jnp.sum((ends[None, :] <= block_start[:, None]).astype(jnp.int32), axis=1), N_EXPERTS - 1)
    n_used = (ends[-1:] // rows).astype(jnp.int32)

    y_tiles = _moe(block_exp, n_used, src_rows, dst_rows, h2_tiles,
                   w_gate_up, b_gate_up, w_down, b_down, rows, n_assign)
    return _combine(y_tiles, gate, x1, g_final, min(COMBINE_ROWS, tokens))


def kernel(x, g_mix, w_in, qa_norm, ka_norm, sink_b, w_branch_a, w_branch_b, w_out, g_ffn, w_router, b_router,
           w_gate_up, b_gate_up, w_down, b_down, g_final):
    batch, seq, d_model = x.shape
    assert g_mix.shape[0] == 1, "single-layer block"
    out = _layer(x.reshape(batch * seq, d_model), batch, seq, g_mix[0], w_in[0], qa_norm[0], ka_norm[0],
                 sink_b[0], w_branch_a[0], w_branch_b[0], w_out[0], g_ffn[0], w_router[0], b_router[0],
                 w_gate_up[0], b_gate_up[0], w_down[0], b_down[0], g_final)
    return out.reshape(batch, seq, d_model)
```

```python
import functools

import jax
import jax.numpy as jnp
from jax import lax
from jax.experimental import pallas as pl
from jax.experimental.pallas import tpu as pltpu

HEAD_DIM = 64
Q_HEADS = 8
KV_HEADS = 2
GROUPS = Q_HEADS // KV_HEADS
Q_COLS = Q_HEADS * HEAD_DIM
KV_COLS = KV_HEADS * HEAD_DIM
GROUP_COLS = GROUPS * HEAD_DIM
WINDOW = 128
GRID_W = 64
ROPE_THETA = 10000.0
N_EXPERTS = 32
TOP_K = 4
D_FF = 1024
SWIGLU_LIMIT = 7.0
SWIGLU_ALPHA = 1.702
EPS = 1e-6
NEG_INF = -1e30
LOG2_E = 1.4426950408889634

LANES = 128
SUBLANES = 8
V_ROWS = 80
VMEM_LIMIT = 56 * 1024 * 1024

F32 = jnp.float32
BF16 = jnp.bfloat16


def _cparams(sem, vmem=VMEM_LIMIT):
    return pltpu.CompilerParams(dimension_semantics=sem, vmem_limit_bytes=vmem)


def _rope(y, cos, sin_signed):
    lane = lax.broadcasted_iota(jnp.int32, y.shape, 1)
    first_half = (lane & 32) == 0
    rot = jnp.where(first_half, pltpu.roll(y, 96, 1), pltpu.roll(y, 32, 1))
    return y * cos + rot * sin_signed


def _head_rmsnorm(q, ones_blockdiag, gain):
    q2 = q * q
    hi = q2.astype(BF16)
    lo = (q2 - hi.astype(F32)).astype(BF16)
    ss = (jnp.dot(hi, ones_blockdiag, preferred_element_type=F32)
          + jnp.dot(lo, ones_blockdiag, preferred_element_type=F32))
    return q * lax.rsqrt(ss * (1.0 / HEAD_DIM) + EPS) * gain


def _proj_kernel(x_ref, g_ref, w_ref, cosa_ref, sina_ref, cos1_ref, sin1_ref, qg_ref, kg_ref, bd_ref,
                 qa_ref, ka_ref, va_ref, qb_ref, kb_ref, vb_ref, sga_ref, sgb_ref, *, d_model):
    x = x_ref[...]
    ms = jnp.mean(x * x, axis=-1, keepdims=True)
    h = (x * lax.rsqrt(ms + EPS) * g_ref[...]).astype(BF16)
    scale = HEAD_DIM ** -0.5 * LOG2_E

    def proj(c0, width):
        return jnp.dot(h, w_ref[:, c0:c0 + width], preferred_element_type=F32)

    cosa, sina = cosa_ref[...], sina_ref[...]
    cos1, sin1 = cos1_ref[...], sin1_ref[...]
    bd = bd_ref[...]
    slab = lambda v, s: v[:, s * LANES:(s + 1) * LANES]
    c = 0
    qa = proj(c, Q_COLS)
    for s in range(Q_COLS // LANES):
        q = _head_rmsnorm(slab(qa, s), bd, qg_ref[...])
        qa_ref[:, s * LANES:(s + 1) * LANES] = (_rope(q, cosa, sina) * scale).astype(BF16)
    c += Q_COLS
    kva = proj(c, 2 * KV_COLS)
    ka_ref[...] = _rope(_head_rmsnorm(slab(kva, 0), bd, kg_ref[...]), cosa, sina).astype(BF16)
    va_ref[...] = slab(kva, 1).astype(BF16)
    c += 2 * KV_COLS
    qb = proj(c, Q_COLS)
    for s in range(Q_COLS // LANES):
        qb_ref[:, s * LANES:(s + 1) * LANES] = (_rope(slab(qb, s), cos1, sin1) * scale).astype(BF16)
    c += Q_COLS
    kvb = proj(c, 2 * KV_COLS)
    kb_ref[...] = _rope(slab(kvb, 0), cos1, sin1).astype(BF16)
    vb_ref[...] = slab(kvb, 1).astype(BF16)
    c += 2 * KV_COLS
    sga_ref[...] = jax.nn.sigmoid(proj(c, d_model)).astype(BF16)
    c += d_model
    sgb_ref[...] = jax.nn.sigmoid(proj(c, d_model)).astype(BF16)


def _rope_tables(seq):
    def slab(ang):
        cos, sin = jnp.cos(ang), jnp.sin(ang)
        cos64 = jnp.concatenate([cos, cos], axis=-1)
        sin64 = jnp.concatenate([-sin, sin], axis=-1)
        return jnp.tile(cos64, (1, 2)), jnp.tile(sin64, (1, 2))

    pos = jnp.arange(seq, dtype=F32)
    inv1 = ROPE_THETA ** (-jnp.arange(0, HEAD_DIM, 2, dtype=F32) / HEAD_DIM)
    ang1 = pos[:, None] * inv1[None, :]
    rows = seq // GRID_W
    row = jnp.broadcast_to(jnp.arange(rows, dtype=F32)[:, None], (rows, GRID_W)).reshape(-1)
    col = jnp.broadcast_to(jnp.arange(GRID_W, dtype=F32)[None, :], (rows, GRID_W)).reshape(-1)
    axis_dim = HEAD_DIM // 2
    inva = ROPE_THETA ** (-jnp.arange(0, axis_dim, 2, dtype=F32) / axis_dim)
    anga = jnp.concatenate([row[:, None] * inva[None, :], col[:, None] * inva[None, :]], axis=-1)
    return slab(anga), slab(ang1)


def _project(x2d, g_mix, w_in, qa_norm, ka_norm, seq, tm):
    tokens, d_model = x2d.shape
    in_cols = w_in.shape[1]
    (cosa, sina), (cos1, sin1) = _rope_tables(seq)
    head_id = jnp.arange(LANES) // HEAD_DIM
    ones_blockdiag = (head_id[:, None] == head_id[None, :]).astype(BF16)
    qg = jnp.tile(qa_norm.astype(F32), 2)[None, :]
    kg = jnp.tile(ka_norm.astype(F32), 2)[None, :]
    seq_tiles = seq // tm
    row = lambda i: (i, 0)
    pos = lambda i: (i % seq_tiles, 0)
    fixed = lambda i: (0, 0)
    bf = lambda cols: jax.ShapeDtypeStruct((tokens, cols), BF16)
    out_cols = [Q_COLS, KV_COLS, KV_COLS, Q_COLS, KV_COLS, KV_COLS, d_model, d_model]
    return pl.pallas_call(
        functools.partial(_proj_kernel, d_model=d_model),
        grid=(tokens // tm,),
        in_specs=[
            pl.BlockSpec((tm, d_model), row),
            pl.BlockSpec((1, d_model), fixed),
            pl.BlockSpec((d_model, in_cols), fixed),
            pl.BlockSpec((tm, LANES), pos), pl.BlockSpec((tm, LANES), pos),
            pl.BlockSpec((tm, LANES), pos), pl.BlockSpec((tm, LANES), pos),
            pl.BlockSpec((1, LANES), fixed), pl.BlockSpec((1, LANES), fixed),
            pl.BlockSpec((LANES, LANES), fixed),
        ],
        out_specs=[pl.BlockSpec((tm, c), row) for c in out_cols],
        out_shape=[bf(c) for c in out_cols],
        compiler_params=_cparams(("arbitrary",)),
        name="proj",
    )(x2d, g_mix.astype(F32)[None, :], w_in.astype(BF16), cosa, sina, cos1, sin1, qg, kg, ones_blockdiag)


def _stack_group_queries(q):
    return jnp.concatenate([q[:, g * HEAD_DIM:(g + 1) * HEAD_DIM] for g in range(GROUPS)], axis=0)


def _unstack_group_outputs(o_t, tq):
    pad = jnp.zeros((LANES - HEAD_DIM, o_t.shape[1]), o_t.dtype)
    o_pad = jnp.concatenate([o_t, pad], axis=0)
    heads = [o_pad[:, g * tq:(g + 1) * tq].T[:, :HEAD_DIM] for g in range(GROUPS)]
    return jnp.concatenate(heads, axis=1)


def _attn_global_kernel(q_ref, k_ref, vt_ref, o_ref, s0, s1, p0, p1, acc_ref, *, tq, tk, n_chunks):
    q_st = _stack_group_queries(q_ref[...])
    m_rows = q_st.shape[0]

    def scores(c, s_ref):
        k_c = k_ref[pl.ds(pl.multiple_of(c * tk, tk), tk), :]
        s_ref[...] = lax.dot_general(k_c, q_st, (((1,), (1,)), ((), ())), preferred_element_type=F32)

    def softmax(s_ref, p_ref, m_prev):
        m_new = jnp.maximum(m_prev, jnp.max(s_ref[...], axis=0, keepdims=True))
        p_ref[...] = jnp.exp2(s_ref[...] - m_new).astype(BF16)
        return m_new, jnp.exp2(m_prev - m_new)

    def values(c, p_ref, alpha):
        acc_ref[...] = acc_ref[...] * alpha + jnp.dot(vt_ref[c], p_ref[...], preferred_element_type=F32)

    scores(0, s0)
    p1[...] = jnp.zeros(p1.shape, BF16)
    acc_ref[...] = jnp.zeros(acc_ref.shape, F32)

    def pair(j, carry):
        m, alpha = carry
        c = 2 * j
        scores(c + 1, s1)
        m_a, alpha_a = softmax(s0, p0, m)
        values(jnp.maximum(c - 1, 0), p1, alpha)
        scores(jnp.minimum(c + 2, n_chunks - 1), s0)
        m_b, alpha_b = softmax(s1, p1, m_a)
        values(c, p0, alpha_a)
        return m_b, alpha_b

    init = (jnp.full((1, m_rows), NEG_INF, F32), jnp.ones((1, m_rows), F32))
    _, alpha = lax.fori_loop(0, n_chunks // 2, pair, init)
    values(n_chunks - 1, p1, alpha)
    acc = acc_ref[...]
    o_t = acc[:HEAD_DIM, :] / acc[HEAD_DIM:HEAD_DIM + 1, :]
    o_ref[...] = _unstack_group_outputs(o_t, tq).astype(o_ref.dtype)


def _attn_window_kernel(q_ref, k_ref, vt_ref, sink_ref, o_ref, *, tq, n_blocks, blocks_per_step):
    span = 3 * tq
    sink = sink_ref[...]
    for i in range(blocks_per_step):
        n = pl.program_id(2) * blocks_per_step + i
        c0 = jnp.clip(n - 1, 0, n_blocks - 3)
        start = pl.multiple_of(c0 * tq, tq)
        q_st = _stack_group_queries(q_ref[i * tq:(i + 1) * tq, :])
        k_w = k_ref[pl.ds(start, span), :]
        s = lax.dot_general(k_w, q_st, (((1,), (1,)), ((), ())), preferred_element_type=F32)
        kpos = start + lax.broadcasted_iota(jnp.int32, s.shape, 0)
        qpos = n * tq + (lax.broadcasted_iota(jnp.int32, s.shape, 1) & (tq - 1))
        s = jnp.where(jnp.abs(kpos - qpos) <= WINDOW, s, NEG_INF)
        m = jnp.maximum(jnp.max(s, axis=0, keepdims=True), sink)
        p = jnp.exp2(s - m).astype(BF16)
        acc = jnp.dot(vt_ref[c0], p[:tq], preferred_element_type=F32)
        for j in range(1, 3):
            acc += jnp.dot(vt_ref[c0 + j], p[j * tq:(j + 1) * tq], preferred_element_type=F32)
        denom = acc[HEAD_DIM:HEAD_DIM + 1, :] + jnp.exp2(sink - m)
        o_ref[i * tq:(i + 1) * tq, :] = _unstack_group_outputs(acc[:HEAD_DIM, :] / denom, tq).astype(o_ref.dtype)


def _kv_layouts(k2d, v2d, batch, seq, chunk):
    k_hm = k2d.reshape(batch, seq, KV_HEADS, HEAD_DIM).transpose(0, 2, 1, 3)
    v_t = v2d.reshape(batch, seq // chunk, chunk, KV_HEADS, HEAD_DIM).transpose(0, 3, 1, 4, 2)
    ones = jnp.ones(v_t.shape[:3] + (V_ROWS - HEAD_DIM, chunk), v_t.dtype)
    return k_hm, jnp.concatenate([v_t, ones], axis=3)


def _attention_global(q2d, k2d, v2d, batch, seq, tq, tk):
    tokens = q2d.shape[0]
    n_chunks = seq // tk
    assert n_chunks % 2 == 0, "the chunk pipeline advances two chunks per step"
    k_hm, v_t = _kv_layouts(k2d, v2d, batch, seq, tk)
    q_tiles = seq // tq
    m_rows = GROUPS * tq
    qmap = lambda b, h, i: (b * q_tiles + i, h)
    return pl.pallas_call(
        functools.partial(_attn_global_kernel, tq=tq, tk=tk, n_chunks=n_chunks),
        grid=(batch, KV_HEADS, q_tiles),
        in_specs=[
            pl.BlockSpec((tq, GROUP_COLS), qmap),
            pl.BlockSpec((None, None, seq, HEAD_DIM), lambda b, h, i: (b, h, 0, 0)),
            pl.BlockSpec((None, None, n_chunks, V_ROWS, tk), lambda b, h, i: (b, h, 0, 0, 0)),
        ],
        out_specs=pl.BlockSpec((tq, GROUP_COLS), qmap),
        out_shape=jax.ShapeDtypeStruct((tokens, Q_COLS), BF16),
        scratch_shapes=[pltpu.VMEM((tk, m_rows), F32), pltpu.VMEM((tk, m_rows), F32),
                        pltpu.VMEM((tk, m_rows), BF16), pltpu.VMEM((tk, m_rows), BF16),
                        pltpu.VMEM((V_ROWS, m_rows), F32)],
        compiler_params=_cparams(("arbitrary", "arbitrary", "arbitrary")),
        name="attn_global",
    )(q2d, k_hm, v_t)


def _attention_window(q2d, k2d, v2d, sink, batch, seq):
    tokens = q2d.shape[0]
    tq = WINDOW
    n_blocks = seq // tq
    assert n_blocks >= 3
    k_hm, v_t = _kv_layouts(k2d, v2d, batch, seq, tq)
    m_rows = GROUPS * tq
    sink_rows = jnp.repeat(sink.astype(F32).reshape(KV_HEADS, GROUPS) * LOG2_E, tq, axis=1)[:, None, :]
    per_step = WINDOW_BLOCKS_PER_STEP if n_blocks % WINDOW_BLOCKS_PER_STEP == 0 else 1
    steps = n_blocks // per_step
    qmap = lambda b, h, i: (b * steps + i, h)
    return pl.pallas_call(
        functools.partial(_attn_window_kernel, tq=tq, n_blocks=n_blocks, blocks_per_step=per_step),
        grid=(batch, KV_HEADS, steps),
        in_specs=[
            pl.BlockSpec((per_step * tq, GROUP_COLS), qmap),
            pl.BlockSpec((None, None, seq, HEAD_DIM), lambda b, h, i: (b, h, 0, 0)),
            pl.BlockSpec((None, None, n_blocks, V_ROWS, tq), lambda b, h, i: (b, h, 0, 0, 0)),
            pl.BlockSpec((None, 1, m_rows), lambda b, h, i: (h, 0, 0)),
        ],
        out_specs=pl.BlockSpec((per_step * tq, GROUP_COLS), qmap),
        out_shape=jax.ShapeDtypeStruct((tokens, Q_COLS), BF16),
        compiler_params=_cparams(("arbitrary", "arbitrary", "arbitrary")),
        name="attn_window",
    )(q2d, k_hm, v_t, sink_rows)


def _store_token_tiles(ref, val):
    rows = val.shape[0]
    for j in range(val.shape[1] // LANES):
        ref[pl.ds(j, rows, stride=SUBLANES), :] = val[:, j * LANES:(j + 1) * LANES]


def _load_token_tiles(ref, rows):
    return jnp.concatenate([ref[pl.ds(j, rows, stride=SUBLANES), :] for j in range(SUBLANES)], axis=1)


ROUTE_ROWS = 2 * SUBLANES


def _top_k_route(logits_t):
    eid = lax.broadcasted_iota(jnp.int32, logits_t.shape, 0)
    work = logits_t
    vals, idxs, sels = [], [], []
    for _ in range(TOP_K):
        mx = jnp.max(work, axis=0, keepdims=True)
        ix = jnp.min(jnp.where(work == mx, eid, N_EXPERTS), axis=0, keepdims=True)
        sel = eid == ix
        work = jnp.where(sel, -jnp.inf, work)
        vals.append(mx)
        idxs.append(ix)
        sels.append(sel)
    return vals, idxs, sels


def _mid_kernel(oa_ref, ob_ref, sga_ref, sgb_ref, x_ref, wa_ref, wb_ref, wo_ref, g_ref, wr_ref, br_ref, tri_ref,
                x1_ref, h2_ref, route_ref, gate_ref, cnt_ref, *, halves):
    @pl.when(pl.program_id(0) == 0)
    def _():
        cnt_ref[...] = jnp.zeros(cnt_ref.shape, F32)

    tm = x_ref.shape[0]
    hm = tm // halves
    routed = []
    for hf in range(halves):
        rs = slice(hf * hm, (hf + 1) * hm)
        a = jnp.dot(oa_ref[rs, :], wa_ref[...], preferred_element_type=F32)
        b = jnp.dot(ob_ref[rs, :], wb_ref[...], preferred_element_type=F32)
        merged = sga_ref[rs, :].astype(F32) * a + sgb_ref[rs, :].astype(F32) * b
        x1 = x_ref[rs, :] + jnp.dot(merged.astype(BF16), wo_ref[...], preferred_element_type=F32)
        x1_ref[rs, :] = x1
        ms = jnp.mean(x1 * x1, axis=-1, keepdims=True)
        h2 = x1 * lax.rsqrt(ms + EPS) * g_ref[...]
        _store_token_tiles(h2_ref.at[pl.ds(hf * hm * SUBLANES, hm * SUBLANES)], h2)
        hi = h2.astype(BF16)
        lo = (h2 - hi.astype(F32)).astype(BF16)
        nt = (((1,), (1,)), ((), ()))
        r2 = (lax.dot_general(wr_ref[...], hi, nt, preferred_element_type=F32)
              + lax.dot_general(wr_ref[...], lo, nt, preferred_element_type=F32))
        bias = jnp.concatenate([br_ref[...]] * (hm // LANES), axis=1)
        routed.append(_top_k_route(r2[:N_EXPERTS] + r2[N_EXPERTS:] + bias))

    onehots = []
    for vals, idxs, sels in routed:
        chosen = sels[0] | sels[1] | sels[2] | sels[3]
        onehots.append(jnp.where(chosen, 1.0, 0.0))
    onehot = jnp.concatenate(onehots, axis=1)
    counted = jnp.concatenate([cnt_ref[...]] * (tm // LANES), axis=1)
    before = jnp.dot(onehot.astype(BF16), tri_ref[...], preferred_element_type=F32) + counted
    zeros4 = jnp.zeros((SUBLANES - TOP_K, hm), jnp.int32)
    for hf, (vals, idxs, sels) in enumerate(routed):
        rs = slice(hf * hm, (hf + 1) * hm)
        exps = [jnp.exp(v - vals[0]) for v in vals]
        denom = exps[0] + exps[1] + exps[2] + exps[3]
        ranks = [jnp.sum(jnp.where(sels[k], before[:, rs], 0.0), axis=0, keepdims=True).astype(jnp.int32)
                 for k in range(TOP_K)]
        route_ref[:, rs] = jnp.concatenate(idxs + [zeros4] + ranks + [zeros4], axis=0)
        gates_t = jnp.concatenate([e / denom for e in exps] + [jnp.zeros((LANES - TOP_K, hm), F32)], axis=0)
        gate_ref[rs, :] = gates_t.T
    cnt_ref[...] = cnt_ref[...] + jnp.sum(onehot, axis=1, keepdims=True)


def _mid(oa, ob, sga, sgb, x2d, w_a, w_b, w_o, g_ffn, w_router, b_router, tm):
    tokens, d_model = x2d.shape
    wr_t = w_router.astype(F32).T
    wr_hi = wr_t.astype(BF16)
    wr_lo = (wr_t - wr_hi.astype(F32)).astype(BF16)
    wr_split = jnp.concatenate([wr_hi, wr_lo], axis=0)
    br = jnp.broadcast_to(b_router.astype(F32)[:, None], (N_EXPERTS, LANES))
    tri = (jnp.arange(tm)[:, None] < jnp.arange(tm)[None, :]).astype(BF16)
    halves = 2 if tm % (2 * LANES) == 0 else 1
    row = lambda i: (i, 0)
    fixed = lambda i: (0, 0)
    return pl.pallas_call(
        functools.partial(_mid_kernel, halves=halves),
        grid=(tokens // tm,),
        in_specs=[
            pl.BlockSpec((tm, Q_COLS), row), pl.BlockSpec((tm, Q_COLS), row),
            pl.BlockSpec((tm, d_model), row), pl.BlockSpec((tm, d_model), row),
            pl.BlockSpec((tm, d_model), row),
            pl.BlockSpec((Q_COLS, d_model), fixed), pl.BlockSpec((Q_COLS, d_model), fixed),
            pl.BlockSpec((d_model, d_model), fixed),
            pl.BlockSpec((1, d_model), fixed),
            pl.BlockSpec((2 * N_EXPERTS, d_model), fixed),
            pl.BlockSpec((N_EXPERTS, LANES), fixed),
            pl.BlockSpec((tm, tm), fixed),
        ],
        out_specs=[
            pl.BlockSpec((tm, d_model), row),
            pl.BlockSpec((tm * SUBLANES, LANES), row),
            pl.BlockSpec((ROUTE_ROWS, tm), lambda i: (0, i)),
            pl.BlockSpec((tm, LANES), row),
            pl.BlockSpec((N_EXPERTS, LANES), fixed),
        ],
        out_shape=[
            jax.ShapeDtypeStruct((tokens, d_model), F32),
            jax.ShapeDtypeStruct((tokens * SUBLANES, LANES), F32),
            jax.ShapeDtypeStruct((ROUTE_ROWS, tokens), jnp.int32),
            jax.ShapeDtypeStruct((tokens, LANES), F32),
            jax.ShapeDtypeStruct((N_EXPERTS, LANES), F32),
        ],
        compiler_params=_cparams(("arbitrary",)),
        name="mid",
    )(oa, ob, sga, sgb, x2d, w_a.astype(BF16), w_b.astype(BF16), w_o.astype(BF16),
      g_ffn.astype(F32)[None, :], wr_split, br, tri)


def _tile_copy(src_ref, src_row, dst_ref, dst_row, sem):
    return pltpu.make_async_copy(
        src_ref.at[pl.ds(pl.multiple_of(src_row, SUBLANES), SUBLANES)],
        dst_ref.at[pl.ds(pl.multiple_of(dst_row, SUBLANES), SUBLANES)], sem)


def _start_tile_gather(off_smem, base, n_rows, src_ref, dst_ref, sem):
    def start(r, carry):
        _tile_copy(src_ref, off_smem[base + r], dst_ref, r * SUBLANES, sem).start()
        return carry
    lax.fori_loop(0, n_rows, start, 0, unroll=8)


def _start_tile_scatter(off_smem, base, n_rows, src_ref, dst_ref, sem):
    def start(r, carry):
        _tile_copy(src_ref, r * SUBLANES, dst_ref, off_smem[base + r], sem).start()
        return carry
    lax.fori_loop(0, n_rows, start, 0, unroll=8)


def _wait_tiles(n_rows, src_ref, dst_ref, sem):
    window = pl.ds(0, n_rows * SUBLANES)
    pltpu.make_async_copy(src_ref.at[window], dst_ref.at[window], sem).wait()


def _moe_kernel(bexp_ref, nused_ref, src_ref, dst_ref, h2_ref, wgu_ref, bgu_ref, wd_ref, bd_ref,
                y_ref, src_smem, dst_smem, xbuf, ybuf, wgu_bf, wd_bf, isem, gsem, ssem, *, rows, n_assign):
    i = pl.program_id(0)
    n_steps = pl.num_programs(0)
    n_used = nused_ref[0]
    slot = lax.rem(i, 2)
    nxt = 1 - slot

    def idx_copies(blk, s):
        window = pl.ds(pl.multiple_of(s * rows, rows), rows)
        return (pltpu.make_async_copy(src_ref.at[blk], src_smem.at[window], isem.at[0, s]),
                pltpu.make_async_copy(dst_ref.at[blk], dst_smem.at[window], isem.at[1, s]))

    def wait_scatter(s):
        _wait_tiles(rows, ybuf.at[s], y_ref, ssem.at[s])

    @pl.when(i == 0)
    def _():
        ybuf[...] = jnp.zeros(ybuf.shape, F32)
        for s in range(2):
            dump = pl.ds((n_assign + s * rows) * SUBLANES, rows * SUBLANES)
            zero_fill = pltpu.make_async_copy(ybuf.at[s], y_ref.at[dump], ssem.at[s])
            zero_fill.start()
            zero_fill.wait()
        for c in idx_copies(0, 0):
            c.start()
        for c in idx_copies(0, 0):
            c.wait()
        _start_tile_gather(src_smem, 0, rows, h2_ref, xbuf.at[0], gsem.at[0])

    @pl.when(i + 1 < n_used)
    def _():
        for c in idx_copies(i + 1, nxt):
            c.start()

    @pl.when(i < n_used)
    def _():
        _wait_tiles(rows, h2_ref, xbuf.at[slot], gsem.at[slot])

    @pl.when(i + 1 < n_used)
    def _():
        for c in idx_copies(i + 1, nxt):
            c.wait()
        _start_tile_gather(src_smem, nxt * rows, rows, h2_ref, xbuf.at[nxt], gsem.at[nxt])

    @pl.when(jnp.logical_and(i >= 2, i - 2 < n_used))
    def _():
        wait_scatter(slot)

    new_expert = jnp.logical_or(i == 0, bexp_ref[i] != bexp_ref[jnp.maximum(i - 1, 0)])

    @pl.when(jnp.logical_and(new_expert, i < n_used))
    def _():
        wgu_bf[...] = wgu_ref[...].astype(BF16)
        wd_bf[...] = wd_ref[...].astype(BF16)

    @pl.when(i < n_used)
    def _():
        x = _load_token_tiles(xbuf.at[slot], rows).astype(BF16)
        gu = jnp.dot(x, wgu_bf[...], preferred_element_type=F32) + bgu_ref[...]
        gate = jnp.minimum(gu[:, :D_FF], SWIGLU_LIMIT)
        up = jnp.clip(gu[:, D_FF:], -SWIGLU_LIMIT, SWIGLU_LIMIT)
        act = (up + 1.0) * (gate * jax.nn.sigmoid(SWIGLU_ALPHA * gate))
        y = jnp.dot(act.astype(BF16), wd_bf[...], preferred_element_type=F32) + bd_ref[...]
        _store_token_tiles(ybuf.at[slot], y)
        _start_tile_scatter(dst_smem, slot * rows, rows, ybuf.at[slot], y_ref, ssem.at[slot])

    @pl.when(i == n_steps - 1)
    def _():
        @pl.when(jnp.logical_and(i >= 1, i - 1 < n_used))
        def _():
            wait_scatter(nxt)

        @pl.when(i < n_used)
        def _():
            wait_scatter(slot)


def _moe(block_exp, n_used, src_rows, dst_rows, h2_tiles, w_gate_up, b_gate_up, w_down, b_down, rows, n_assign):
    n_blocks = src_rows.shape[0]
    n_exp, d_model, ff2 = w_gate_up.shape
    grid_spec = pltpu.PrefetchScalarGridSpec(
        num_scalar_prefetch=2,
        grid=(n_blocks,),
        in_specs=[
            pl.BlockSpec(memory_space=pl.ANY),
            pl.BlockSpec(memory_space=pl.ANY),
            pl.BlockSpec(memory_space=pl.ANY),
            pl.BlockSpec((None, d_model, ff2), lambda i, be, nu: (be[i], 0, 0)),
            pl.BlockSpec((None, 1, ff2), lambda i, be, nu: (be[i], 0, 0)),
            pl.BlockSpec((None, ff2 // 2, d_model), lambda i, be, nu: (be[i], 0, 0)),
            pl.BlockSpec((None, 1, d_model), lambda i, be, nu: (be[i], 0, 0)),
        ],
        out_specs=pl.BlockSpec(memory_space=pl.ANY),
        scratch_shapes=[
            pltpu.SMEM((2 * rows,), jnp.int32),
            pltpu.SMEM((2 * rows,), jnp.int32),
            pltpu.VMEM((2, rows * SUBLANES, LANES), F32),
            pltpu.VMEM((2, rows * SUBLANES, LANES), F32),
            pltpu.VMEM((d_model, ff2), BF16),
            pltpu.VMEM((ff2 // 2, d_model), BF16),
            pltpu.SemaphoreType.DMA((2, 2)),
            pltpu.SemaphoreType.DMA((2,)),
            pltpu.SemaphoreType.DMA((2,)),
        ],
    )
    return pl.pallas_call(
        functools.partial(_moe_kernel, rows=rows, n_assign=n_assign),
        grid_spec=grid_spec,
        out_shape=jax.ShapeDtypeStruct(((n_assign + 2 * rows) * SUBLANES, LANES), F32),
        compiler_params=_cparams(("arbitrary",)),
        name="moe",
    )(block_exp, n_used, src_rows, dst_rows, h2_tiles, w_gate_up, b_gate_up[:, None, :], w_down,
      b_down[:, None, :])


def _combine_kernel(y_ref, gate_ref, x1_ref, g_ref, o_ref, *, tm):
    gate = gate_ref[...]
    out = x1_ref[...]
    stride = TOP_K * SUBLANES
    for k in range(TOP_K):
        y_k = jnp.concatenate(
            [y_ref[pl.ds(k * SUBLANES + j, tm, stride=stride), :] for j in range(SUBLANES)], axis=1)
        out = out + gate[:, k:k + 1] * y_k
    ms = jnp.mean(out * out, axis=-1, keepdims=True)
    o_ref[...] = out * lax.rsqrt(ms + EPS) * g_ref[...]


def _combine(y_tiles, gate, x1, g_final, tm):
    tokens, d_model = x1.shape
    row = lambda i: (i, 0)
    return pl.pallas_call(
        functools.partial(_combine_kernel, tm=tm),
        grid=(tokens // tm,),
        in_specs=[
            pl.BlockSpec((tm * TOP_K * SUBLANES, LANES), row),
            pl.BlockSpec((tm, LANES), row),
            pl.BlockSpec((tm, d_model), row),
            pl.BlockSpec((1, d_model), lambda i: (0, 0)),
        ],
        out_specs=pl.BlockSpec((tm, d_model), row),
        out_shape=jax.ShapeDtypeStruct((tokens, d_model), F32),
        compiler_params=_cparams(("arbitrary",)),
        name="combine",
    )(y_tiles, gate, x1, g_final.astype(F32)[None, :])


PROJ_ROWS = 256
MID_ROWS = 512
ATTN_Q_ROWS = 512
ATTN_K_ROWS = 512
WINDOW_BLOCKS_PER_STEP = 4
MOE_ROWS = 256
COMBINE_ROWS = 256


def _layer(x2d, batch, seq, g_mix, w_in, qa_norm, ka_norm, sink_b, w_branch_a, w_branch_b, w_out,
           g_ffn, w_router, b_router, w_gate_up, b_gate_up, w_down, b_down, g_final):
    tokens = x2d.shape[0]
    qa, ka, va, qb, kb, vb, sga, sgb = _project(x2d, g_mix, w_in, qa_norm, ka_norm, seq, min(PROJ_ROWS, seq))
    oa = _attention_global(qa, ka, va, batch, seq, ATTN_Q_ROWS, min(ATTN_K_ROWS, seq // 2))
    ob = _attention_window(qb, kb, vb, sink_b, batch, seq)
    x1, h2_tiles, route, gate, cnt = _mid(
        oa, ob, sga, sgb, x2d, w_branch_a, w_branch_b, w_out, g_ffn, w_router, b_router, min(MID_ROWS, tokens))

    rows = MOE_ROWS
    n_assign = tokens * TOP_K
    n_blocks = n_assign // rows + N_EXPERTS
    counts = cnt[:, 0].astype(jnp.int32)
    padded = (counts + rows - 1) // rows * rows
    ends = jnp.cumsum(padded)
    starts = ends - padded
    eidx, rank = route[:TOP_K], route[SUBLANES:SUBLANES + TOP_K]
    seg_start = jnp.sum(jnp.where(eidx[None] == jnp.arange(N_EXPERTS)[:, None, None], starts[:, None, None], 0),
                        axis=0)
    dest = seg_start + rank
    assign = jnp.arange(tokens, dtype=jnp.int32)[None, :] * TOP_K + jnp.arange(TOP_K, dtype=jnp.int32)[:, None]
    row_assign = jnp.full((n_blocks * rows,), -1, jnp.int32).at[dest.reshape(-1)].set(
        assign.reshape(-1), unique_indices=True).reshape(n_blocks, rows)
    real = row_assign >= 0
    dump_slot = (n_assign + (jnp.arange(n_blocks, dtype=jnp.int32) % 2)[:, None] * rows
                 + jnp.arange(rows, dtype=jnp.int32)[None, :])
    src_rows = jnp.where(real, row_assign // TOP_K, 0) * SUBLANES
    dst_rows = jnp.where(real, row_assign, dump_slot) * SUBLANES
    block_start = jnp.arange(n_blocks, dtype=jnp.int32) * rows
    block_exp = jnp.minimum(
        jnp.sum((ends[None, :] <= block_start[:, None]).astype(jnp.int32), axis=1), N_EXPERTS - 1)
    n_used = (ends[-1:] // rows).astype(jnp.int32)

    y_tiles = _moe(block_exp, n_used, src_rows, dst_rows, h2_tiles,
                   w_gate_up, b_gate_up, w_down, b_down, rows, n_assign)
    return _combine(y_tiles, gate, x1, g_final, min(COMBINE_ROWS, tokens))


def kernel(x, g_mix, w_in, qa_norm, ka_norm, sink_b, w_branch_a, w_branch_b, w_out, g_ffn, w_router, b_router,
           w_gate_up, b_gate_up, w_down, b_down, g_final):
    batch, seq, d_model = x.shape
    assert g_mix.shape[0] == 1, "single-layer block"
    out = _layer(x.reshape(batch * seq, d_model), batch, seq, g_mix[0], w_in[0], qa_norm[0], ka_norm[0],
                 sink_b[0], w_branch_a[0], w_branch_b[0], w_out[0], g_ffn[0], w_router[0], b_router[0],
                 w_gate_up[0], b_gate_up[0], w_down[0], b_down[0], g_final)
    return out.reshape(batch, seq, d_model)
```

```python
import functools

import jax
import jax.numpy as jnp
from jax import lax
from jax.experimental import pallas as pl
from jax.experimental.pallas import tpu as pltpu

HEAD_DIM = 64
Q_HEADS = 8
KV_HEADS = 2
GROUPS = Q_HEADS // KV_HEADS
Q_COLS = Q_HEADS * HEAD_DIM
KV_COLS = KV_HEADS * HEAD_DIM
GROUP_COLS = GROUPS * HEAD_DIM
WINDOW = 128
GRID_W = 64
ROPE_THETA = 10000.0
N_EXPERTS = 32
TOP_K = 4
D_FF = 1024
SWIGLU_LIMIT = 7.0
SWIGLU_ALPHA = 1.702
EPS = 1e-6
NEG_INF = -1e30
LOG2_E = 1.4426950408889634

LANES = 128
SUBLANES = 8
V_ROWS = 80
VMEM_LIMIT = 56 * 1024 * 1024

F32 = jnp.float32
BF16 = jnp.bfloat16


def _cparams(sem, vmem=VMEM_LIMIT):
    return pltpu.CompilerParams(dimension_semantics=sem, vmem_limit_bytes=vmem)


def _rope(y, cos, sin_signed):
    lane = lax.broadcasted_iota(jnp.int32, y.shape, 1)
    first_half = (lane & 32) == 0
    rot = jnp.where(first_half, pltpu.roll(y, 96, 1), pltpu.roll(y, 32, 1))
    return y * cos + rot * sin_signed


def _head_rmsnorm(q, ones_blockdiag, gain):
    q2 = q * q
    hi = q2.astype(BF16)
    lo = (q2 - hi.astype(F32)).astype(BF16)
    ss = (jnp.dot(hi, ones_blockdiag, preferred_element_type=F32)
          + jnp.dot(lo, ones_blockdiag, preferred_element_type=F32))
    return q * lax.rsqrt(ss * (1.0 / HEAD_DIM) + EPS) * gain


def _proj_kernel(x_ref, g_ref, w_ref, cosa_ref, sina_ref, cos1_ref, sin1_ref, qg_ref, kg_ref, bd_ref,
                 qa_ref, ka_ref, vta_ref, qb_ref, kb_ref, vtb_ref, sga_ref, sgb_ref, *, d_model):
    x = x_ref[...]
    ms = jnp.mean(x * x, axis=-1, keepdims=True)
    h = (x * lax.rsqrt(ms + EPS) * g_ref[...]).astype(BF16)
    scale = HEAD_DIM ** -0.5 * LOG2_E

    def proj(c0, width):
        return jnp.dot(h, w_ref[:, c0:c0 + width], preferred_element_type=F32)

    cosa, sina = cosa_ref[...], sina_ref[...]
    cos1, sin1 = cos1_ref[...], sin1_ref[...]
    bd = bd_ref[...]
    slab = lambda v, s: v[:, s * LANES:(s + 1) * LANES]
    c = 0
    qa = proj(c, Q_COLS)
    for s in range(Q_COLS // LANES):
        q = _head_rmsnorm(slab(qa, s), bd, qg_ref[...])
        qa_ref[:, s * LANES:(s + 1) * LANES] = (_rope(q, cosa, sina) * scale).astype(BF16)
    c += Q_COLS
    kva = proj(c, 2 * KV_COLS)
    ka = _rope(_head_rmsnorm(slab(kva, 0), bd, kg_ref[...]), cosa, sina).astype(BF16)
    va_t = slab(kva, 1).T
    ones = jnp.ones((V_ROWS - HEAD_DIM, va_t.shape[1]), BF16)
    for hd in range(KV_HEADS):
        ka_ref[hd] = ka[:, hd * HEAD_DIM:(hd + 1) * HEAD_DIM]
        vta_ref[hd, :HEAD_DIM, :] = va_t[hd * HEAD_DIM:(hd + 1) * HEAD_DIM, :].astype(BF16)
        vta_ref[hd, HEAD_DIM:, :] = ones
    c += 2 * KV_COLS
    qb = proj(c, Q_COLS)
    for s in range(Q_COLS // LANES):
        qb_ref[:, s * LANES:(s + 1) * LANES] = (_rope(slab(qb, s), cos1, sin1) * scale).astype(BF16)
    c += Q_COLS
    kvb = proj(c, 2 * KV_COLS)
    kb = _rope(slab(kvb, 0), cos1, sin1).astype(BF16)
    vb_t = slab(kvb, 1).T
    ones = jnp.ones((V_ROWS - HEAD_DIM, WINDOW), BF16)
    for hd in range(KV_HEADS):
        kb_ref[hd] = kb[:, hd * HEAD_DIM:(hd + 1) * HEAD_DIM]
        for ch in range(vtb_ref.shape[1]):
            vtb_ref[hd, ch, :HEAD_DIM, :] = vb_t[hd * HEAD_DIM:(hd + 1) * HEAD_DIM,
                                                 ch * WINDOW:(ch + 1) * WINDOW].astype(BF16)
            vtb_ref[hd, ch, HEAD_DIM:, :] = ones
    c += 2 * KV_COLS
    sga_ref[...] = jax.nn.sigmoid(proj(c, d_model)).astype(BF16)
    c += d_model
    sgb_ref[...] = jax.nn.sigmoid(proj(c, d_model)).astype(BF16)


def _rope_tables(seq):
    def slab(ang):
        cos, sin = jnp.cos(ang), jnp.sin(ang)
        cos64 = jnp.concatenate([cos, cos], axis=-1)
        sin64 = jnp.concatenate([-sin, sin], axis=-1)
        return jnp.tile(cos64, (1, 2)), jnp.tile(sin64, (1, 2))

    pos = jnp.arange(seq, dtype=F32)
    inv1 = ROPE_THETA ** (-jnp.arange(0, HEAD_DIM, 2, dtype=F32) / HEAD_DIM)
    ang1 = pos[:, None] * inv1[None, :]
    rows = seq // GRID_W
    row = jnp.broadcast_to(jnp.arange(rows, dtype=F32)[:, None], (rows, GRID_W)).reshape(-1)
    col = jnp.broadcast_to(jnp.arange(GRID_W, dtype=F32)[None, :], (rows, GRID_W)).reshape(-1)
    axis_dim = HEAD_DIM // 2
    inva = ROPE_THETA ** (-jnp.arange(0, axis_dim, 2, dtype=F32) / axis_dim)
    anga = jnp.concatenate([row[:, None] * inva[None, :], col[:, None] * inva[None, :]], axis=-1)
    return slab(anga), slab(ang1)


def _project(x2d, g_mix, w_in, qa_norm, ka_norm, seq, tm, tk):
    tokens, d_model = x2d.shape
    in_cols = w_in.shape[1]
    (cosa, sina), (cos1, sin1) = _rope_tables(seq)
    head_id = jnp.arange(LANES) // HEAD_DIM
    ones_blockdiag = (head_id[:, None] == head_id[None, :]).astype(BF16)
    qg = jnp.tile(qa_norm.astype(F32), 2)[None, :]
    kg = jnp.tile(ka_norm.astype(F32), 2)[None, :]
    seq_tiles = seq // tm
    row = lambda i: (i, 0)
    pos = lambda i: (i % seq_tiles, 0)
    fixed = lambda i: (0, 0)
    batch = tokens // seq
    assert tk % tm == 0 and tm % WINDOW == 0
    sub = tk // tm
    bf = lambda *shape: jax.ShapeDtypeStruct(shape, BF16)
    tok_spec = lambda cols: pl.BlockSpec((tm, cols), row)
    k_spec = pl.BlockSpec((None, KV_HEADS, tm, HEAD_DIM), lambda i: (i // seq_tiles, 0, i % seq_tiles, 0))
    vta_spec = pl.BlockSpec((None, KV_HEADS, None, V_ROWS, tm),
                            lambda i: (i // seq_tiles, 0, (i % seq_tiles) // sub, 0, (i % seq_tiles) % sub))
    vtb_spec = pl.BlockSpec((None, KV_HEADS, tm // WINDOW, V_ROWS, WINDOW),
                            lambda i: (i // seq_tiles, 0, i % seq_tiles, 0, 0))
    k_shape = bf(batch, KV_HEADS, seq, HEAD_DIM)
    return pl.pallas_call(
        functools.partial(_proj_kernel, d_model=d_model),
        grid=(tokens // tm,),
        in_specs=[
            pl.BlockSpec((tm, d_model), row),
            pl.BlockSpec((1, d_model), fixed),
            pl.BlockSpec((d_model, in_cols), fixed),
            pl.BlockSpec((tm, LANES), pos), pl.BlockSpec((tm, LANES), pos),
            pl.BlockSpec((tm, LANES), pos), pl.BlockSpec((tm, LANES), pos),
            pl.BlockSpec((1, LANES), fixed), pl.BlockSpec((1, LANES), fixed),
            pl.BlockSpec((LANES, LANES), fixed),
        ],
        out_specs=[tok_spec(Q_COLS), k_spec, vta_spec, tok_spec(Q_COLS), k_spec, vtb_spec,
                   tok_spec(d_model), tok_spec(d_model)],
        out_shape=[bf(tokens, Q_COLS), k_shape, bf(batch, KV_HEADS, seq // tk, V_ROWS, tk),
                   bf(tokens, Q_COLS), k_shape, bf(batch, KV_HEADS, seq // WINDOW, V_ROWS, WINDOW),
                   bf(tokens, d_model), bf(tokens, d_model)],
        compiler_params=_cparams(("arbitrary",)),
        name="proj",
    )(x2d, g_mix.astype(F32)[None, :], w_in.astype(BF16), cosa, sina, cos1, sin1, qg, kg, ones_blockdiag)


def _stack_group_queries(q):
    return jnp.concatenate([q[:, g * HEAD_DIM:(g + 1) * HEAD_DIM] for g in range(GROUPS)], axis=0)


def _unstack_group_outputs(o_t, tq):
    pad = jnp.zeros((LANES - HEAD_DIM, o_t.shape[1]), o_t.dtype)
    o_pad = jnp.concatenate([o_t, pad], axis=0)
    heads = [o_pad[:, g * tq:(g + 1) * tq].T[:, :HEAD_DIM] for g in range(GROUPS)]
    return jnp.concatenate(heads, axis=1)


def _attn_global_kernel(q_ref, k_ref, vt_ref, o_ref, s0, s1, p0, p1, acc_ref, *, tq, tk, n_chunks):
    q_st = _stack_group_queries(q_ref[...])
    m_rows = q_st.shape[0]

    def scores(c, s_ref):
        k_c = k_ref[pl.ds(pl.multiple_of(c * tk, tk), tk), :]
        s_ref[...] = lax.dot_general(k_c, q_st, (((1,), (1,)), ((), ())), preferred_element_type=F32)

    def softmax(s_ref, p_ref, m_prev):
        m_new = jnp.maximum(m_prev, jnp.max(s_ref[...], axis=0, keepdims=True))
        p_ref[...] = jnp.exp2(s_ref[...] - m_new).astype(BF16)
        return m_new, jnp.exp2(m_prev - m_new)

    def values(c, p_ref, alpha):
        acc_ref[...] = acc_ref[...] * alpha + jnp.dot(vt_ref[c], p_ref[...], preferred_element_type=F32)

    scores(0, s0)
    p1[...] = jnp.zeros(p1.shape, BF16)
    acc_ref[...] = jnp.zeros(acc_ref.shape, F32)

    def pair(j, carry):
        m, alpha = carry
        c = 2 * j
        scores(c + 1, s1)
        m_a, alpha_a = softmax(s0, p0, m)
        values(jnp.maximum(c - 1, 0), p1, alpha)
        scores(jnp.minimum(c + 2, n_chunks - 1), s0)
        m_b, alpha_b = softmax(s1, p1, m_a)
        values(c, p0, alpha_a)
        return m_b, alpha_b

    init = (jnp.full((1, m_rows), NEG_INF, F32), jnp.ones((1, m_rows), F32))
    _, alpha = lax.fori_loop(0, n_chunks // 2, pair, init)
    values(n_chunks - 1, p1, alpha)
    acc = acc_ref[...]
    o_t = acc[:HEAD_DIM, :] / acc[HEAD_DIM:HEAD_DIM + 1, :]
    o_ref[...] = _unstack_group_outputs(o_t, tq).astype(o_ref.dtype)


def _attn_window_kernel(q_ref, k_ref, vt_ref, sink_ref, o_ref, *, tq, n_blocks, blocks_per_step):
    span = 3 * tq
    sink = sink_ref[...]
    for i in range(blocks_per_step):
        n = pl.program_id(2) * blocks_per_step + i
        c0 = jnp.clip(n - 1, 0, n_blocks - 3)
        start = pl.multiple_of(c0 * tq, tq)
        q_st = _stack_group_queries(q_ref[i * tq:(i + 1) * tq, :])
        k_w = k_ref[pl.ds(start, span), :]
        s = lax.dot_general(k_w, q_st, (((1,), (1,)), ((), ())), preferred_element_type=F32)
        kpos = start + lax.broadcasted_iota(jnp.int32, s.shape, 0)
        qpos = n * tq + (lax.broadcasted_iota(jnp.int32, s.shape, 1) & (tq - 1))
        s = jnp.where(jnp.abs(kpos - qpos) <= WINDOW, s, NEG_INF)
        m = jnp.maximum(jnp.max(s, axis=0, keepdims=True), sink)
        p = jnp.exp2(s - m).astype(BF16)
        acc = jnp.dot(vt_ref[c0], p[:tq], preferred_element_type=F32)
        for j in range(1, 3):
            acc += jnp.dot(vt_ref[c0 + j], p[j * tq:(j + 1) * tq], preferred_element_type=F32)
        denom = acc[HEAD_DIM:HEAD_DIM + 1, :] + jnp.exp2(sink - m)
        o_ref[i * tq:(i + 1) * tq, :] = _unstack_group_outputs(acc[:HEAD_DIM, :] / denom, tq).astype(o_ref.dtype)


def _attention_global(q2d, k_hm, v_t, batch, seq, tq):
    tokens = q2d.shape[0]
    n_chunks, tk = v_t.shape[2], v_t.shape[4]
    assert n_chunks % 2 == 0, "the chunk pipeline advances two chunks per step"
    q_tiles = seq // tq
    m_rows = GROUPS * tq
    qmap = lambda b, h, i: (b * q_tiles + i, h)
    return pl.pallas_call(
        functools.partial(_attn_global_kernel, tq=tq, tk=tk, n_chunks=n_chunks),
        grid=(batch, KV_HEADS, q_tiles),
        in_specs=[
            pl.BlockSpec((tq, GROUP_COLS), qmap),
            pl.BlockSpec((None, None, seq, HEAD_DIM), lambda b, h, i: (b, h, 0, 0)),
            pl.BlockSpec((None, None, n_chunks, V_ROWS, tk), lambda b, h, i: (b, h, 0, 0, 0)),
        ],
        out_specs=pl.BlockSpec((tq, GROUP_COLS), qmap),
        out_shape=jax.ShapeDtypeStruct((tokens, Q_COLS), BF16),
        scratch_shapes=[pltpu.VMEM((tk, m_rows), F32), pltpu.VMEM((tk, m_rows), F32),
                        pltpu.VMEM((tk, m_rows), BF16), pltpu.VMEM((tk, m_rows), BF16),
                        pltpu.VMEM((V_ROWS, m_rows), F32)],
        compiler_params=_cparams(("arbitrary", "arbitrary", "arbitrary")),
        name="attn_global",
    )(q2d, k_hm, v_t)


def _attention_window(q2d, k_hm, v_t, sink, batch, seq):
    tokens = q2d.shape[0]
    tq = WINDOW
    n_blocks = seq // tq
    assert n_blocks >= 3
    m_rows = GROUPS * tq
    sink_rows = jnp.repeat(sink.astype(F32).reshape(KV_HEADS, GROUPS) * LOG2_E, tq, axis=1)[:, None, :]
    per_step = WINDOW_BLOCKS_PER_STEP if n_blocks % WINDOW_BLOCKS_PER_STEP == 0 else 1
    steps = n_blocks // per_step
    qmap = lambda b, h, i: (b * steps + i, h)
    return pl.pallas_call(
        functools.partial(_attn_window_kernel, tq=tq, n_blocks=n_blocks, blocks_per_step=per_step),
        grid=(batch, KV_HEADS, steps),
        in_specs=[
            pl.BlockSpec((per_step * tq, GROUP_COLS), qmap),
            pl.BlockSpec((None, None, seq, HEAD_DIM), lambda b, h, i: (b, h, 0, 0)),
            pl.BlockSpec((None, None, n_blocks, V_ROWS, tq), lambda b, h, i: (b, h, 0, 0, 0)),
            pl.BlockSpec((None, 1, m_rows), lambda b, h, i: (h, 0, 0)),
        ],
        out_specs=pl.BlockSpec((per_step * tq, GROUP_COLS), qmap),
        out_shape=jax.ShapeDtypeStruct((tokens, Q_COLS), BF16),
        compiler_params=_cparams(("arbitrary", "arbitrary", "arbitrary")),
        name="attn_window",
    )(q2d, k_hm, v_t, sink_rows)


def _store_token_tiles(ref, val):
    rows = val.shape[0]
    for j in range(val.shape[1] // LANES):
        ref[pl.ds(j, rows, stride=SUBLANES), :] = val[:, j * LANES:(j + 1) * LANES]


def _load_token_tiles(ref, rows):
    return jnp.concatenate([ref[pl.ds(j, rows, stride=SUBLANES), :] for j in range(SUBLANES)], axis=1)


ROUTE_ROWS = 2 * SUBLANES


def _top_k_route(logits_t):
    eid = lax.broadcasted_iota(jnp.int32, logits_t.shape, 0)
    work = logits_t
    vals, idxs, sels = [], [], []
    for _ in range(TOP_K):
        mx = jnp.max(work, axis=0, keepdims=True)
        ix = jnp.min(jnp.where(work == mx, eid, N_EXPERTS), axis=0, keepdims=True)
        sel = eid == ix
        work = jnp.where(sel, -jnp.inf, work)
        vals.append(mx)
        idxs.append(ix)
        sels.append(sel)
    return vals, idxs, sels


def _mid_kernel(oa_ref, ob_ref, sga_ref, sgb_ref, x_ref, wa_ref, wb_ref, wo_ref, g_ref, wr_ref, br_ref, tri_ref,
                x1_ref, h2_ref, route_ref, gate_ref, cnt_ref, *, halves):
    @pl.when(pl.program_id(0) == 0)
    def _():
        cnt_ref[...] = jnp.zeros(cnt_ref.shape, F32)

    tm = x_ref.shape[0]
    hm = tm // halves
    routed = []
    for hf in range(halves):
        rs = slice(hf * hm, (hf + 1) * hm)
        a = jnp.dot(oa_ref[rs, :], wa_ref[...], preferred_element_type=F32)
        b = jnp.dot(ob_ref[rs, :], wb_ref[...], preferred_element_type=F32)
        merged = sga_ref[rs, :].astype(F32) * a + sgb_ref[rs, :].astype(F32) * b
        x1 = x_ref[rs, :] + jnp.dot(merged.astype(BF16), wo_ref[...], preferred_element_type=F32)
        x1_ref[rs, :] = x1
        ms = jnp.mean(x1 * x1, axis=-1, keepdims=True)
        h2 = x1 * lax.rsqrt(ms + EPS) * g_ref[...]
        _store_token_tiles(h2_ref.at[pl.ds(hf * hm * SUBLANES, hm * SUBLANES)], h2)
        hi = h2.astype(BF16)
        lo = (h2 - hi.astype(F32)).astype(BF16)
        nt = (((1,), (1,)), ((), ()))
        r2 = (lax.dot_general(wr_ref[...], hi, nt, preferred_element_type=F32)
              + lax.dot_general(wr_ref[...], lo, nt, preferred_element_type=F32))
        bias = jnp.concatenate([br_ref[...]] * (hm // LANES), axis=1)
        routed.append(_top_k_route(r2[:N_EXPERTS] + r2[N_EXPERTS:] + bias))

    onehots = []
    for vals, idxs, sels in routed:
        chosen = sels[0] | sels[1] | sels[2] | sels[3]
        onehots.append(jnp.where(chosen, 1.0, 0.0))
    onehot = jnp.concatenate(onehots, axis=1)
    counted = jnp.concatenate([cnt_ref[...]] * (tm // LANES), axis=1)
    before = jnp.dot(onehot.astype(BF16), tri_ref[...], preferred_element_type=F32) + counted
    zeros4 = jnp.zeros((SUBLANES - TOP_K, hm), jnp.int32)
    for hf, (vals, idxs, sels) in enumerate(routed):
        rs = slice(hf * hm, (hf + 1) * hm)
        exps = [jnp.exp(v - vals[0]) for v in vals]
        denom = exps[0] + exps[1] + exps[2] + exps[3]
        ranks = [jnp.sum(jnp.where(sels[k], before[:, rs], 0.0), axis=0, keepdims=True).astype(jnp.int32)
                 for k in range(TOP_K)]
        route_ref[:, rs] = jnp.concatenate(idxs + [zeros4] + ranks + [zeros4], axis=0)
        gates_t = jnp.concatenate([e / denom for e in exps] + [jnp.zeros((LANES - TOP_K, hm), F32)], axis=0)
        gate_ref[rs, :] = gates_t.T
    cnt_ref[...] = cnt_ref[...] + jnp.sum(onehot, axis=1, keepdims=True)


def _mid(oa, ob, sga, sgb, x2d, w_a, w_b, w_o, g_ffn, w_router, b_router, tm):
    tokens, d_model = x2d.shape
    wr_t = w_router.astype(F32).T
    wr_hi = wr_t.astype(BF16)
    wr_lo = (wr_t - wr_hi.astype(F32)).astype(BF16)
    wr_split = jnp.concatenate([wr_hi, wr_lo], axis=0)
    br = jnp.broadcast_to(b_router.astype(F32)[:, None], (N_EXPERTS, LANES))
    tri = (jnp.arange(tm)[:, None] < jnp.arange(tm)[None, :]).astype(BF16)
    halves = 2 if tm % (2 * LANES) == 0 else 1
    row = lambda i: (i, 0)
    fixed = lambda i: (0, 0)
    return pl.pallas_call(
        functools.partial(_mid_kernel, halves=halves),
        grid=(tokens // tm,),
        in_specs=[
            pl.BlockSpec((tm, Q_COLS), row), pl.BlockSpec((tm, Q_COLS), row),
            pl.BlockSpec((tm, d_model), row), pl.BlockSpec((tm, d_model), row),
            pl.BlockSpec((tm, d_model), row),
            pl.BlockSpec((Q_COLS, d_model), fixed), pl.BlockSpec((Q_COLS, d_model), fixed),
            pl.BlockSpec((d_model, d_model), fixed),
            pl.BlockSpec((1, d_model), fixed),
            pl.BlockSpec((2 * N_EXPERTS, d_model), fixed),
            pl.BlockSpec((N_EXPERTS, LANES), fixed),
            pl.BlockSpec((tm, tm), fixed),
        ],
        out_specs=[
            pl.BlockSpec((tm, d_model), row),
            pl.BlockSpec((tm * SUBLANES, LANES), row),
            pl.BlockSpec((ROUTE_ROWS, tm), lambda i: (0, i)),
            pl.BlockSpec((tm, LANES), row),
            pl.BlockSpec((N_EXPERTS, LANES), fixed),
        ],
        out_shape=[
            jax.ShapeDtypeStruct((tokens, d_model), F32),
            jax.ShapeDtypeStruct((tokens * SUBLANES, LANES), F32),
            jax.ShapeDtypeStruct((ROUTE_ROWS, tokens), jnp.int32),
            jax.ShapeDtypeStruct((tokens, LANES), F32),
            jax.ShapeDtypeStruct((N_EXPERTS, LANES), F32),
        ],
        compiler_params=_cparams(("arbitrary",)),
        name="mid",
    )(oa, ob, sga, sgb, x2d, w_a.astype(BF16), w_b.astype(BF16), w_o.astype(BF16),
      g_ffn.astype(F32)[None, :], wr_split, br, tri)


def _tile_copy(src_ref, src_row, dst_ref, dst_row, sem):
    return pltpu.make_async_copy(
        src_ref.at[pl.ds(pl.multiple_of(src_row, SUBLANES), SUBLANES)],
        dst_ref.at[pl.ds(pl.multiple_of(dst_row, SUBLANES), SUBLANES)], sem)


def _start_tile_gather(off_smem, base, n_rows, src_ref, dst_ref, sem):
    def start(r, carry):
        _tile_copy(src_ref, off_smem[base + r], dst_ref, r * SUBLANES, sem).start()
        return carry
    lax.fori_loop(0, n_rows, start, 0, unroll=8)


def _start_tile_scatter(off_smem, base, n_rows, src_ref, dst_ref, sem):
    def start(r, carry):
        _tile_copy(src_ref, r * SUBLANES, dst_ref, off_smem[base + r], sem).start()
        return carry
    lax.fori_loop(0, n_rows, start, 0, unroll=8)


def _wait_tiles(n_rows, src_ref, dst_ref, sem):
    window = pl.ds(0, n_rows * SUBLANES)
    pltpu.make_async_copy(src_ref.at[window], dst_ref.at[window], sem).wait()


SRC_WINDOWS = 3
DST_WINDOWS = 4
MOE_COL_TILES = 4


def _moe_kernel(bexp_ref, nused_ref, src_ref, dst_ref, h2_ref, wgu_ref, bgu_ref, wd_ref, bd_ref,
                y_ref, src_smem, dst_smem, xbuf, ybuf, wgu_bf, wd_bf, isem, gsem, ssem, *, rows, n_assign):
    i = pl.program_id(0)
    n_used = nused_ref[0]
    slot = lax.rem(i, 2)
    nxt = 1 - slot
    src_base = lambda blk: lax.rem(blk, SRC_WINDOWS) * rows
    dst_base = lambda blk: lax.rem(blk, DST_WINDOWS) * rows

    def idx_copies(blk):
        sw = pl.ds(pl.multiple_of(src_base(blk), rows), rows)
        dw = pl.ds(pl.multiple_of(dst_base(blk), rows), rows)
        return (pltpu.make_async_copy(src_ref.at[blk], src_smem.at[sw], isem.at[0, lax.rem(blk, SRC_WINDOWS)]),
                pltpu.make_async_copy(dst_ref.at[blk], dst_smem.at[dw], isem.at[1, lax.rem(blk, DST_WINDOWS)]))

    def fetch_idx_now(blk):
        for c in idx_copies(blk):
            c.start()
        for c in idx_copies(blk):
            c.wait()

    def gather_copy(blk_base, r, s):
        return _tile_copy(h2_ref, src_smem[blk_base + r], xbuf.at[s], r * SUBLANES, gsem.at[s])

    def scatter_copy(blk_base, r, s):
        return _tile_copy(ybuf.at[s], r * SUBLANES, y_ref, dst_smem[blk_base + r], ssem.at[s])

    def swiglu(g, u):
        gate = jnp.minimum(g, SWIGLU_LIMIT)
        up = jnp.clip(u, -SWIGLU_LIMIT, SWIGLU_LIMIT)
        return ((up + 1.0) * (gate * jax.nn.sigmoid(SWIGLU_ALPHA * gate))).astype(BF16)

    @pl.when(i == 0)
    def _():
        ybuf[...] = jnp.zeros(ybuf.shape, F32)
        for s in range(2):
            dump = pl.ds((n_assign + s * rows) * SUBLANES, rows * SUBLANES)
            zero_fill = pltpu.make_async_copy(ybuf.at[s], y_ref.at[dump], ssem.at[s])
            zero_fill.start()
            zero_fill.wait()
        fetch_idx_now(0)

        @pl.when(1 < n_used)
        def _():
            fetch_idx_now(1)

        _start_tile_gather(src_smem, 0, rows, h2_ref, xbuf.at[0], gsem.at[0])

    @pl.when(i + 2 < n_used)
    def _():
        for c in idx_copies(i + 2):
            c.start()

    @pl.when(i < n_used)
    def _():
        _wait_tiles(rows, h2_ref, xbuf.at[slot], gsem.at[slot])

    @pl.when(jnp.logical_and(i >= 2, i <= n_used))
    def _():
        _wait_tiles(rows, ybuf.at[slot], y_ref, ssem.at[slot])

    new_expert = jnp.logical_or(i == 0, bexp_ref[i] != bexp_ref[jnp.maximum(i - 1, 0)])

    @pl.when(jnp.logical_and(new_expert, i < n_used))
    def _():
        wgu_bf[...] = wgu_ref[...].astype(BF16)
        wd_bf[...] = wd_ref[...].astype(BF16)

    steady = jnp.logical_and(i >= 1, i + 1 < n_used)

    @pl.when(steady)
    def _():
        gbase, sbase = src_base(i + 1), dst_base(i - 1)
        burst = rows // MOE_COL_TILES
        width = D_FF // MOE_COL_TILES
        x = _load_token_tiles(xbuf.at[slot], rows).astype(BF16)
        acts = []
        for t in range(MOE_COL_TILES):
            cg = slice(t * width, (t + 1) * width)
            cu = slice(D_FF + t * width, D_FF + (t + 1) * width)
            g = jnp.dot(x, wgu_bf[:, cg], preferred_element_type=F32) + bgu_ref[:, cg]
            u = jnp.dot(x, wgu_bf[:, cu], preferred_element_type=F32) + bgu_ref[:, cu]
            for r in range(t * burst, (t + 1) * burst):
                gather_copy(gbase, r, nxt).start()
            acts.append(swiglu(g, u))
        act = jnp.concatenate(acts, axis=1)
        out_width = wd_bf.shape[1] // MOE_COL_TILES
        for t in range(MOE_COL_TILES):
            co = slice(t * out_width, (t + 1) * out_width)
            y_t = jnp.dot(act, wd_bf[:, co], preferred_element_type=F32) + bd_ref[:, co]
            for r in range(t * burst, (t + 1) * burst):
                scatter_copy(sbase, r, nxt).start()
            for j in range(out_width // LANES):
                ybuf[slot, pl.ds(t * (out_width // LANES) + j, rows, stride=SUBLANES), :] = (
                    y_t[:, j * LANES:(j + 1) * LANES])

    @pl.when(jnp.logical_and(i < n_used, jnp.logical_not(steady)))
    def _():
        @pl.when(i + 1 < n_used)
        def _():
            _start_tile_gather(src_smem, src_base(i + 1), rows, h2_ref, xbuf.at[nxt], gsem.at[nxt])

        x = _load_token_tiles(xbuf.at[slot], rows).astype(BF16)
        gu = jnp.dot(x, wgu_bf[...], preferred_element_type=F32) + bgu_ref[...]
        act = swiglu(gu[:, :D_FF], gu[:, D_FF:])
        y = jnp.dot(act, wd_bf[...], preferred_element_type=F32) + bd_ref[...]
        _store_token_tiles(ybuf.at[slot], y)

        @pl.when(i >= 1)
        def _():
            _start_tile_scatter(dst_smem, dst_base(i - 1), rows, ybuf.at[nxt], y_ref, ssem.at[nxt])

    @pl.when(i == n_used)
    def _():
        _start_tile_scatter(dst_smem, dst_base(i - 1), rows, ybuf.at[nxt], y_ref, ssem.at[nxt])
        _wait_tiles(rows, ybuf.at[nxt], y_ref, ssem.at[nxt])

    @pl.when(i + 2 < n_used)
    def _():
        for c in idx_copies(i + 2):
            c.wait()


def _moe(block_exp, n_used, src_rows, dst_rows, h2_tiles, w_gate_up, b_gate_up, w_down, b_down, rows, n_assign):
    n_blocks = src_rows.shape[0]
    n_exp, d_model, ff2 = w_gate_up.shape
    grid_spec = pltpu.PrefetchScalarGridSpec(
        num_scalar_prefetch=2,
        grid=(n_blocks,),
        in_specs=[
            pl.BlockSpec(memory_space=pl.ANY),
            pl.BlockSpec(memory_space=pl.ANY),
            pl.BlockSpec(memory_space=pl.ANY),
            pl.BlockSpec((None, d_model, ff2), lambda i, be, nu: (be[i], 0, 0)),
            pl.BlockSpec((None, 1, ff2), lambda i, be, nu: (be[i], 0, 0)),
            pl.BlockSpec((None, ff2 // 2, d_model), lambda i, be, nu: (be[i], 0, 0)),
            pl.BlockSpec((None, 1, d_model), lambda i, be, nu: (be[i], 0, 0)),
        ],
        out_specs=pl.BlockSpec(memory_space=pl.ANY),
        scratch_shapes=[
            pltpu.SMEM((SRC_WINDOWS * rows,), jnp.int32),
            pltpu.SMEM((DST_WINDOWS * rows,), jnp.int32),
            pltpu.VMEM((2, rows * SUBLANES, LANES), F32),
            pltpu.VMEM((2, rows * SUBLANES, LANES), F32),
            pltpu.VMEM((d_model, ff2), BF16),
            pltpu.VMEM((ff2 // 2, d_model), BF16),
            pltpu.SemaphoreType.DMA((2, DST_WINDOWS)),
            pltpu.SemaphoreType.DMA((2,)),
            pltpu.SemaphoreType.DMA((2,)),
        ],
    )
    return pl.pallas_call(
        functools.partial(_moe_kernel, rows=rows, n_assign=n_assign),
        grid_spec=grid_spec,
        out_shape=jax.ShapeDtypeStruct(((n_assign + 2 * rows) * SUBLANES, LANES), F32),
        compiler_params=_cparams(("arbitrary",)),
        name="moe",
    )(block_exp, n_used, src_rows, dst_rows, h2_tiles, w_gate_up, b_gate_up[:, None, :], w_down,
      b_down[:, None, :])


def _combine_kernel(y_ref, gate_ref, x1_ref, g_ref, o_ref, *, tm):
    gate = gate_ref[...]
    out = x1_ref[...]
    stride = TOP_K * SUBLANES
    for k in range(TOP_K):
        y_k = jnp.concatenate(
            [y_ref[pl.ds(k * SUBLANES + j, tm, stride=stride), :] for j in range(SUBLANES)], axis=1)
        out = out + gate[:, k:k + 1] * y_k
    ms = jnp.mean(out * out, axis=-1, keepdims=True)
    o_ref[...] = out * lax.rsqrt(ms + EPS) * g_ref[...]


def _combine(y_tiles, gate, x1, g_final, tm):
    tokens, d_model = x1.shape
    row = lambda i: (i, 0)
    return pl.pallas_call(
        functools.partial(_combine_kernel, tm=tm),
        grid=(tokens // tm,),
        in_specs=[
            pl.BlockSpec((tm * TOP_K * SUBLANES, LANES), row),
            pl.BlockSpec((tm, LANES), row),
            pl.BlockSpec((tm, d_model), row),
            pl.BlockSpec((1, d_model), lambda i: (0, 0)),
        ],
        out_specs=pl.BlockSpec((tm, d_model), row),
        out_shape=jax.ShapeDtypeStruct((tokens, d_model), F32),
        compiler_params=_cparams(("arbitrary",)),
        name="combine",
    )(y_tiles, gate, x1, g_final.astype(F32)[None, :])


PROJ_ROWS = 256
MID_ROWS = 512
ATTN_Q_ROWS = 512
ATTN_K_ROWS = 512
WINDOW_BLOCKS_PER_STEP = 4
MOE_ROWS = 256
COMBINE_ROWS = 256


def _layer(x2d, batch, seq, g_mix, w_in, qa_norm, ka_norm, sink_b, w_branch_a, w_branch_b, w_out,
           g_ffn, w_router, b_router, w_gate_up, b_gate_up, w_down, b_down, g_final):
    tokens = x2d.shape[0]
    qa, ka, vta, qb, kb, vtb, sga, sgb = _project(x2d, g_mix, w_in, qa_norm, ka_norm, seq,
                                                  min(PROJ_ROWS, seq), min(ATTN_K_ROWS, seq // 2))
    oa = _attention_global(qa, ka, vta, batch, seq, ATTN_Q_ROWS)
    ob = _attention_window(qb, kb, vtb, sink_b, batch, seq)
    x1, h2_tiles, route, gate, cnt = _mid(
        oa, ob, sga, sgb, x2d, w_branch_a, w_branch_b, w_out, g_ffn, w_router, b_router, min(MID_ROWS, tokens))

    rows = MOE_ROWS
    n_assign = tokens * TOP_K
    n_blocks = n_assign // rows + N_EXPERTS
    counts = cnt[:, 0].astype(jnp.int32)
    padded = (counts + rows - 1) // rows * rows
    ends = jnp.cumsum(padded)
    starts = ends - padded
    eidx, rank = route[:TOP_K], route[SUBLANES:SUBLANES + TOP_K]
    seg_start = jnp.sum(jnp.where(eidx[None] == jnp.arange(N_EXPERTS)[:, None, None], starts[:, None, None], 0),
                        axis=0)
    dest = seg_start + rank
    assign = jnp.arange(tokens, dtype=jnp.int32)[None, :] * TOP_K + jnp.arange(TOP_K, dtype=jnp.int32)[:, None]
    row_assign = jnp.full((n_blocks * rows,), -1, jnp.int32).at[dest.reshape(-1)].set(
        assign.reshape(-1), unique_indices=True).reshape(n_blocks, rows)
    real = row_assign >= 0
    dump_slot = (n_assign + (jnp.arange(n_blocks, dtype=jnp.int32) % 2)[:, None] * rows
                 + jnp.arange(rows, dtype=jnp.int32)[None, :])
    src_rows = jnp.where(real, row_assign // TOP_K, 0) * SUBLANES
    dst_rows = jnp.where(real, row_assign, dump_slot) * SUBLANES
    block_start = jnp.arange(n_blocks, dtype=jnp.int32) * rows
    block_exp = jnp.minimum(
        jnp.sum((ends[None, :] <= block_start[:, None]).astype(jnp.int32), axis=1), N_EXPERTS - 1)
    n_used = (ends[-1:] // rows).astype(jnp.int32)

    y_tiles = _moe(block_exp, n_used, src_rows, dst_rows, h2_tiles,
                   w_gate_up, b_gate_up, w_down, b_down, rows, n_assign)
    return _combine(y_tiles, gate, x1, g_final, min(COMBINE_ROWS, tokens))


def kernel(x, g_mix, w_in, qa_norm, ka_norm, sink_b, w_branch_a, w_branch_b, w_out, g_ffn, w_router, b_router,
           w_gate_up, b_gate_up, w_down, b_down, g_final):
    batch, seq, d_model = x.shape
    assert g_mix.shape[0] == 1, "single-layer block"
    out = _layer(x.reshape(batch * seq, d_model), batch, seq, g_mix[0], w_in[0], qa_norm[0], ka_norm[0],
                 sink_b[0], w_branch_a[0], w_branch_b[0], w_out[0], g_ffn[0], w_router[0], b_router[0],
                 w_gate_up[0], b_gate_up[0], w_down[0], b_down[0], g_final)
    return out.reshape(batch, seq, d_model)
```

```python
import functools

import jax
import jax.numpy as jnp
from jax import lax
from jax.experimental import pallas as pl
from jax.experimental.pallas import tpu as pltpu

HEAD_DIM = 64
Q_HEADS = 8
KV_HEADS = 2
GROUPS = Q_HEADS // KV_HEADS
Q_COLS = Q_HEADS * HEAD_DIM
KV_COLS = KV_HEADS * HEAD_DIM
GROUP_COLS = GROUPS * HEAD_DIM
WINDOW = 128
GRID_W = 64
ROPE_THETA = 10000.0
N_EXPERTS = 32
TOP_K = 4
D_FF = 1024
SWIGLU_LIMIT = 7.0
SWIGLU_ALPHA = 1.702
EPS = 1e-6
NEG_INF = -1e30
LOG2_E = 1.4426950408889634

LANES = 128
SUBLANES = 8
V_ROWS = 80
VMEM_LIMIT = 56 * 1024 * 1024

F32 = jnp.float32
BF16 = jnp.bfloat16


def _cparams(sem, vmem=VMEM_LIMIT):
    return pltpu.CompilerParams(dimension_semantics=sem, vmem_limit_bytes=vmem)


def _rope(y, cos, sin_signed):
    lane = lax.broadcasted_iota(jnp.int32, y.shape, 1)
    first_half = (lane & 32) == 0
    rot = jnp.where(first_half, pltpu.roll(y, 96, 1), pltpu.roll(y, 32, 1))
    return y * cos + rot * sin_signed


def _head_rmsnorm(q, ones_blockdiag, gain):
    q2 = q * q
    hi = q2.astype(BF16)
    lo = (q2 - hi.astype(F32)).astype(BF16)
    ss = (jnp.dot(hi, ones_blockdiag, preferred_element_type=F32)
          + jnp.dot(lo, ones_blockdiag, preferred_element_type=F32))
    return q * lax.rsqrt(ss * (1.0 / HEAD_DIM) + EPS) * gain


def _proj_kernel(x_ref, g_ref, w_ref, cosa_ref, sina_ref, cos1_ref, sin1_ref, qg_ref, kg_ref, bd_ref,
                 qa_ref, ka_ref, vta_ref, qb_ref, kb_ref, vtb_ref, sga_ref, sgb_ref, *, d_model):
    x = x_ref[...]
    ms = jnp.mean(x * x, axis=-1, keepdims=True)
    h = (x * lax.rsqrt(ms + EPS) * g_ref[...]).astype(BF16)
    scale = HEAD_DIM ** -0.5 * LOG2_E

    def proj(c0, width):
        return jnp.dot(h, w_ref[:, c0:c0 + width], preferred_element_type=F32)

    cosa, sina = cosa_ref[...], sina_ref[...]
    cos1, sin1 = cos1_ref[...], sin1_ref[...]
    bd = bd_ref[...]
    slab = lambda v, s: v[:, s * LANES:(s + 1) * LANES]
    c = 0
    qa = proj(c, Q_COLS)
    for s in range(Q_COLS // LANES):
        q = _head_rmsnorm(slab(qa, s), bd, qg_ref[...])
        qa_ref[:, s * LANES:(s + 1) * LANES] = (_rope(q, cosa, sina) * scale).astype(BF16)
    c += Q_COLS
    kva = proj(c, 2 * KV_COLS)
    ka = _rope(_head_rmsnorm(slab(kva, 0), bd, kg_ref[...]), cosa, sina).astype(BF16)
    va_t = slab(kva, 1).T
    ones = jnp.ones((V_ROWS - HEAD_DIM, va_t.shape[1]), BF16)
    for hd in range(KV_HEADS):
        ka_ref[hd] = ka[:, hd * HEAD_DIM:(hd + 1) * HEAD_DIM]
        vta_ref[hd, :HEAD_DIM, :] = va_t[hd * HEAD_DIM:(hd + 1) * HEAD_DIM, :].astype(BF16)
        vta_ref[hd, HEAD_DIM:, :] = ones
    c += 2 * KV_COLS
    qb = proj(c, Q_COLS)
    for s in range(Q_COLS // LANES):
        qb_ref[:, s * LANES:(s + 1) * LANES] = (_rope(slab(qb, s), cos1, sin1) * scale).astype(BF16)
    c += Q_COLS
    kvb = proj(c, 2 * KV_COLS)
    kb = _rope(slab(kvb, 0), cos1, sin1).astype(BF16)
    vb_t = slab(kvb, 1).T
    ones = jnp.ones((V_ROWS - HEAD_DIM, WINDOW), BF16)
    for hd in range(KV_HEADS):
        kb_ref[hd] = kb[:, hd * HEAD_DIM:(hd + 1) * HEAD_DIM]
        for ch in range(vtb_ref.shape[1]):
            vtb_ref[hd, ch, :HEAD_DIM, :] = vb_t[hd * HEAD_DIM:(hd + 1) * HEAD_DIM,
                                                 ch * WINDOW:(ch + 1) * WINDOW].astype(BF16)
            vtb_ref[hd, ch, HEAD_DIM:, :] = ones
    c += 2 * KV_COLS
    sga_ref[...] = jax.nn.sigmoid(proj(c, d_model)).astype(BF16)
    c += d_model
    sgb_ref[...] = jax.nn.sigmoid(proj(c, d_model)).astype(BF16)


def _rope_tables(seq):
    def slab(ang):
        cos, sin = jnp.cos(ang), jnp.sin(ang)
        cos64 = jnp.concatenate([cos, cos], axis=-1)
        sin64 = jnp.concatenate([-sin, sin], axis=-1)
        return jnp.tile(cos64, (1, 2)), jnp.tile(sin64, (1, 2))

    pos = jnp.arange(seq, dtype=F32)
    inv1 = ROPE_THETA ** (-jnp.arange(0, HEAD_DIM, 2, dtype=F32) / HEAD_DIM)
    ang1 = pos[:, None] * inv1[None, :]
    rows = seq // GRID_W
    row = jnp.broadcast_to(jnp.arange(rows, dtype=F32)[:, None], (rows, GRID_W)).reshape(-1)
    col = jnp.broadcast_to(jnp.arange(GRID_W, dtype=F32)[None, :], (rows, GRID_W)).reshape(-1)
    axis_dim = HEAD_DIM // 2
    inva = ROPE_THETA ** (-jnp.arange(0, axis_dim, 2, dtype=F32) / axis_dim)
    anga = jnp.concatenate([row[:, None] * inva[None, :], col[:, None] * inva[None, :]], axis=-1)
    return slab(anga), slab(ang1)


def _project(x2d, g_mix, w_in, qa_norm, ka_norm, seq, tm, tk):
    tokens, d_model = x2d.shape
    in_cols = w_in.shape[1]
    (cosa, sina), (cos1, sin1) = _rope_tables(seq)
    head_id = jnp.arange(LANES) // HEAD_DIM
    ones_blockdiag = (head_id[:, None] == head_id[None, :]).astype(BF16)
    qg = jnp.tile(qa_norm.astype(F32), 2)[None, :]
    kg = jnp.tile(ka_norm.astype(F32), 2)[None, :]
    seq_tiles = seq // tm
    row = lambda i: (i, 0)
    pos = lambda i: (i % seq_tiles, 0)
    fixed = lambda i: (0, 0)
    batch = tokens // seq
    assert tk % tm == 0 and tm % WINDOW == 0
    sub = tk // tm
    bf = lambda *shape: jax.ShapeDtypeStruct(shape, BF16)
    tok_spec = lambda cols: pl.BlockSpec((tm, cols), row)
    k_spec = pl.BlockSpec((None, KV_HEADS, tm, HEAD_DIM), lambda i: (i // seq_tiles, 0, i % seq_tiles, 0))
    vta_spec = pl.BlockSpec((None, KV_HEADS, None, V_ROWS, tm),
                            lambda i: (i // seq_tiles, 0, (i % seq_tiles) // sub, 0, (i % seq_tiles) % sub))
    vtb_spec = pl.BlockSpec((None, KV_HEADS, tm // WINDOW, V_ROWS, WINDOW),
                            lambda i: (i // seq_tiles, 0, i % seq_tiles, 0, 0))
    k_shape = bf(batch, KV_HEADS, seq, HEAD_DIM)
    return pl.pallas_call(
        functools.partial(_proj_kernel, d_model=d_model),
        grid=(tokens // tm,),
        in_specs=[
            pl.BlockSpec((tm, d_model), row),
            pl.BlockSpec((1, d_model), fixed),
            pl.BlockSpec((d_model, in_cols), fixed),
            pl.BlockSpec((tm, LANES), pos), pl.BlockSpec((tm, LANES), pos),
            pl.BlockSpec((tm, LANES), pos), pl.BlockSpec((tm, LANES), pos),
            pl.BlockSpec((1, LANES), fixed), pl.BlockSpec((1, LANES), fixed),
            pl.BlockSpec((LANES, LANES), fixed),
        ],
        out_specs=[tok_spec(Q_COLS), k_spec, vta_spec, tok_spec(Q_COLS), k_spec, vtb_spec,
                   tok_spec(d_model), tok_spec(d_model)],
        out_shape=[bf(tokens, Q_COLS), k_shape, bf(batch, KV_HEADS, seq // tk, V_ROWS, tk),
                   bf(tokens, Q_COLS), k_shape, bf(batch, KV_HEADS, seq // WINDOW, V_ROWS, WINDOW),
                   bf(tokens, d_model), bf(tokens, d_model)],
        compiler_params=_cparams(("arbitrary",)),
        name="proj",
    )(x2d, g_mix.astype(F32)[None, :], w_in.astype(BF16), cosa, sina, cos1, sin1, qg, kg, ones_blockdiag)


def _stack_group_queries(q):
    return jnp.concatenate([q[:, g * HEAD_DIM:(g + 1) * HEAD_DIM] for g in range(GROUPS)], axis=0)


def _unstack_group_outputs(o_t, tq):
    pad = jnp.zeros((LANES - HEAD_DIM, o_t.shape[1]), o_t.dtype)
    o_pad = jnp.concatenate([o_t, pad], axis=0)
    heads = [o_pad[:, g * tq:(g + 1) * tq].T[:, :HEAD_DIM] for g in range(GROUPS)]
    return jnp.concatenate(heads, axis=1)


def _attn_global_kernel(q_ref, k_ref, vt_ref, o_ref, s0, s1, p0, p1, acc_ref, *, tq, tk, n_chunks):
    q_st = _stack_group_queries(q_ref[...])
    m_rows = q_st.shape[0]

    def scores(c, s_ref):
        k_c = k_ref[pl.ds(pl.multiple_of(c * tk, tk), tk), :]
        s_ref[...] = lax.dot_general(k_c, q_st, (((1,), (1,)), ((), ())), preferred_element_type=F32)

    def softmax(s_ref, p_ref, m_prev):
        m_new = jnp.maximum(m_prev, jnp.max(s_ref[...], axis=0, keepdims=True))
        p_ref[...] = jnp.exp2(s_ref[...] - m_new).astype(BF16)
        return m_new, jnp.exp2(m_prev - m_new)

    def values(c, p_ref, alpha):
        acc_ref[...] = acc_ref[...] * alpha + jnp.dot(vt_ref[c], p_ref[...], preferred_element_type=F32)

    scores(0, s0)
    p1[...] = jnp.zeros(p1.shape, BF16)
    acc_ref[...] = jnp.zeros(acc_ref.shape, F32)

    def pair(j, carry):
        m, alpha = carry
        c = 2 * j
        scores(c + 1, s1)
        m_a, alpha_a = softmax(s0, p0, m)
        values(jnp.maximum(c - 1, 0), p1, alpha)
        scores(jnp.minimum(c + 2, n_chunks - 1), s0)
        m_b, alpha_b = softmax(s1, p1, m_a)
        values(c, p0, alpha_a)
        return m_b, alpha_b

    init = (jnp.full((1, m_rows), NEG_INF, F32), jnp.ones((1, m_rows), F32))
    _, alpha = lax.fori_loop(0, n_chunks // 2, pair, init)
    values(n_chunks - 1, p1, alpha)
    acc = acc_ref[...]
    o_t = acc[:HEAD_DIM, :] / acc[HEAD_DIM:HEAD_DIM + 1, :]
    o_ref[...] = _unstack_group_outputs(o_t, tq).astype(o_ref.dtype)


def _attn_window_kernel(q_ref, k_ref, vt_ref, sink_ref, o_ref, *, tq, n_blocks, blocks_per_step):
    span = 3 * tq
    sink = sink_ref[...]
    for i in range(blocks_per_step):
        n = pl.program_id(2) * blocks_per_step + i
        c0 = jnp.clip(n - 1, 0, n_blocks - 3)
        start = pl.multiple_of(c0 * tq, tq)
        q_st = _stack_group_queries(q_ref[i * tq:(i + 1) * tq, :])
        k_w = k_ref[pl.ds(start, span), :]
        s = lax.dot_general(k_w, q_st, (((1,), (1,)), ((), ())), preferred_element_type=F32)
        kpos = start + lax.broadcasted_iota(jnp.int32, s.shape, 0)
        qpos = n * tq + (lax.broadcasted_iota(jnp.int32, s.shape, 1) & (tq - 1))
        s = jnp.where(jnp.abs(kpos - qpos) <= WINDOW, s, NEG_INF)
        m = jnp.maximum(jnp.max(s, axis=0, keepdims=True), sink)
        p = jnp.exp2(s - m).astype(BF16)
        acc = jnp.dot(vt_ref[c0], p[:tq], preferred_element_type=F32)
        for j in range(1, 3):
            acc += jnp.dot(vt_ref[c0 + j], p[j * tq:(j + 1) * tq], preferred_element_type=F32)
        denom = acc[HEAD_DIM:HEAD_DIM + 1, :] + jnp.exp2(sink - m)
        o_ref[i * tq:(i + 1) * tq, :] = _unstack_group_outputs(acc[:HEAD_DIM, :] / denom, tq).astype(o_ref.dtype)


def _attention_global(q2d, k_hm, v_t, batch, seq, tq):
    tokens = q2d.shape[0]
    n_chunks, tk = v_t.shape[2], v_t.shape[4]
    assert n_chunks % 2 == 0, "the chunk pipeline advances two chunks per step"
    q_tiles = seq // tq
    m_rows = GROUPS * tq
    qmap = lambda b, h, i: (b * q_tiles + i, h)
    return pl.pallas_call(
        functools.partial(_attn_global_kernel, tq=tq, tk=tk, n_chunks=n_chunks),
        grid=(batch, KV_HEADS, q_tiles),
        in_specs=[
            pl.BlockSpec((tq, GROUP_COLS), qmap),
            pl.BlockSpec((None, None, seq, HEAD_DIM), lambda b, h, i: (b, h, 0, 0)),
            pl.BlockSpec((None, None, n_chunks, V_ROWS, tk), lambda b, h, i: (b, h, 0, 0, 0)),
        ],
        out_specs=pl.BlockSpec((tq, GROUP_COLS), qmap),
        out_shape=jax.ShapeDtypeStruct((tokens, Q_COLS), BF16),
        scratch_shapes=[pltpu.VMEM((tk, m_rows), F32), pltpu.VMEM((tk, m_rows), F32),
                        pltpu.VMEM((tk, m_rows), BF16), pltpu.VMEM((tk, m_rows), BF16),
                        pltpu.VMEM((V_ROWS, m_rows), F32)],
        compiler_params=_cparams(("arbitrary", "arbitrary", "arbitrary")),
        name="attn_global",
    )(q2d, k_hm, v_t)


def _attention_window(q2d, k_hm, v_t, sink, batch, seq):
    tokens = q2d.shape[0]
    tq = WINDOW
    n_blocks = seq // tq
    assert n_blocks >= 3
    m_rows = GROUPS * tq
    sink_rows = jnp.repeat(sink.astype(F32).reshape(KV_HEADS, GROUPS) * LOG2_E, tq, axis=1)[:, None, :]
    per_step = WINDOW_BLOCKS_PER_STEP if n_blocks % WINDOW_BLOCKS_PER_STEP == 0 else 1
    steps = n_blocks // per_step
    qmap = lambda b, h, i: (b * steps + i, h)
    return pl.pallas_call(
        functools.partial(_attn_window_kernel, tq=tq, n_blocks=n_blocks, blocks_per_step=per_step),
        grid=(batch, KV_HEADS, steps),
        in_specs=[
            pl.BlockSpec((per_step * tq, GROUP_COLS), qmap),
            pl.BlockSpec((None, None, seq, HEAD_DIM), lambda b, h, i: (b, h, 0, 0)),
            pl.BlockSpec((None, None, n_blocks, V_ROWS, tq), lambda b, h, i: (b, h, 0, 0, 0)),
            pl.BlockSpec((None, 1, m_rows), lambda b, h, i: (h, 0, 0)),
        ],
        out_specs=pl.BlockSpec((per_step * tq, GROUP_COLS), qmap),
        out_shape=jax.ShapeDtypeStruct((tokens, Q_COLS), BF16),
        compiler_params=_cparams(("arbitrary", "arbitrary", "arbitrary")),
        name="attn_window",
    )(q2d, k_hm, v_t, sink_rows)


def _store_token_tiles(ref, val):
    rows = val.shape[0]
    for j in range(val.shape[1] // LANES):
        ref[pl.ds(j, rows, stride=SUBLANES), :] = val[:, j * LANES:(j + 1) * LANES]


def _load_token_tiles(ref, rows):
    return jnp.concatenate([ref[pl.ds(j, rows, stride=SUBLANES), :] for j in range(SUBLANES)], axis=1)


ROUTE_ROWS = 2 * SUBLANES


def _top_k_route(logits_t):
    eid = lax.broadcasted_iota(jnp.int32, logits_t.shape, 0)
    work = logits_t
    vals, idxs, sels = [], [], []
    for _ in range(TOP_K):
        mx = jnp.max(work, axis=0, keepdims=True)
        ix = jnp.min(jnp.where(work == mx, eid, N_EXPERTS), axis=0, keepdims=True)
        sel = eid == ix
        work = jnp.where(sel, -jnp.inf, work)
        vals.append(mx)
        idxs.append(ix)
        sels.append(sel)
    return vals, idxs, sels


def _mid_kernel(oa_ref, ob_ref, sga_ref, sgb_ref, x_ref, wa_ref, wb_ref, wo_ref, g_ref, wr_ref, br_ref, tri_ref,
                x1_ref, h2_ref, route_ref, gate_ref, cnt_ref, *, halves):
    @pl.when(pl.program_id(0) == 0)
    def _():
        cnt_ref[...] = jnp.zeros(cnt_ref.shape, F32)

    tm = x_ref.shape[0]
    hm = tm // halves
    routed = []
    for hf in range(halves):
        rs = slice(hf * hm, (hf + 1) * hm)
        a = jnp.dot(oa_ref[rs, :], wa_ref[...], preferred_element_type=F32)
        b = jnp.dot(ob_ref[rs, :], wb_ref[...], preferred_element_type=F32)
        merged = sga_ref[rs, :].astype(F32) * a + sgb_ref[rs, :].astype(F32) * b
        x1 = x_ref[rs, :] + jnp.dot(merged.astype(BF16), wo_ref[...], preferred_element_type=F32)
        x1_ref[rs, :] = x1
        ms = jnp.mean(x1 * x1, axis=-1, keepdims=True)
        h2 = x1 * lax.rsqrt(ms + EPS) * g_ref[...]
        _store_token_tiles(h2_ref.at[pl.ds(hf * hm * SUBLANES, hm * SUBLANES)], h2)
        hi = h2.astype(BF16)
        lo = (h2 - hi.astype(F32)).astype(BF16)
        nt = (((1,), (1,)), ((), ()))
        r2 = (lax.dot_general(wr_ref[...], hi, nt, preferred_element_type=F32)
              + lax.dot_general(wr_ref[...], lo, nt, preferred_element_type=F32))
        bias = jnp.concatenate([br_ref[...]] * (hm // LANES), axis=1)
        routed.append(_top_k_route(r2[:N_EXPERTS] + r2[N_EXPERTS:] + bias))

    onehots = []
    for vals, idxs, sels in routed:
        chosen = sels[0] | sels[1] | sels[2] | sels[3]
        onehots.append(jnp.where(chosen, 1.0, 0.0))
    onehot = jnp.concatenate(onehots, axis=1)
    counted = jnp.concatenate([cnt_ref[...]] * (tm // LANES), axis=1)
    before = jnp.dot(onehot.astype(BF16), tri_ref[...], preferred_element_type=F32) + counted
    zeros4 = jnp.zeros((SUBLANES - TOP_K, hm), jnp.int32)
    for hf, (vals, idxs, sels) in enumerate(routed):
        rs = slice(hf * hm, (hf + 1) * hm)
        exps = [jnp.exp(v - vals[0]) for v in vals]
        denom = exps[0] + exps[1] + exps[2] + exps[3]
        ranks = [jnp.sum(jnp.where(sels[k], before[:, rs], 0.0), axis=0, keepdims=True).astype(jnp.int32)
                 for k in range(TOP_K)]
        route_ref[:, rs] = jnp.concatenate(idxs + [zeros4] + ranks + [zeros4], axis=0)
        gates_t = jnp.concatenate([e / denom for e in exps] + [jnp.zeros((LANES - TOP_K, hm), F32)], axis=0)
        gate_ref[rs, :] = gates_t.T
    cnt_ref[...] = cnt_ref[...] + jnp.sum(onehot, axis=1, keepdims=True)


def _mid(oa, ob, sga, sgb, x2d, w_a, w_b, w_o, g_ffn, w_router, b_router, tm):
    tokens, d_model = x2d.shape
    wr_t = w_router.astype(F32).T
    wr_hi = wr_t.astype(BF16)
    wr_lo = (wr_t - wr_hi.astype(F32)).astype(BF16)
    wr_split = jnp.concatenate([wr_hi, wr_lo], axis=0)
    br = jnp.broadcast_to(b_router.astype(F32)[:, None], (N_EXPERTS, LANES))
    tri = (jnp.arange(tm)[:, None] < jnp.arange(tm)[None, :]).astype(BF16)
    halves = 2 if tm % (2 * LANES) == 0 else 1
    row = lambda i: (i, 0)
    fixed = lambda i: (0, 0)
    return pl.pallas_call(
        functools.partial(_mid_kernel, halves=halves),
        grid=(tokens // tm,),
        in_specs=[
            pl.BlockSpec((tm, Q_COLS), row), pl.BlockSpec((tm, Q_COLS), row),
            pl.BlockSpec((tm, d_model), row), pl.BlockSpec((tm, d_model), row),
            pl.BlockSpec((tm, d_model), row),
            pl.BlockSpec((Q_COLS, d_model), fixed), pl.BlockSpec((Q_COLS, d_model), fixed),
            pl.BlockSpec((d_model, d_model), fixed),
            pl.BlockSpec((1, d_model), fixed),
            pl.BlockSpec((2 * N_EXPERTS, d_model), fixed),
            pl.BlockSpec((N_EXPERTS, LANES), fixed),
            pl.BlockSpec((tm, tm), fixed),
        ],
        out_specs=[
            pl.BlockSpec((tm, d_model), row),
            pl.BlockSpec((tm * SUBLANES, LANES), row),
            pl.BlockSpec((ROUTE_ROWS, tm), lambda i: (0, i)),
            pl.BlockSpec((tm, LANES), row),
            pl.BlockSpec((N_EXPERTS, LANES), fixed),
        ],
        out_shape=[
            jax.ShapeDtypeStruct((tokens, d_model), F32),
            jax.ShapeDtypeStruct((tokens * SUBLANES, LANES), F32),
            jax.ShapeDtypeStruct((ROUTE_ROWS, tokens), jnp.int32),
            jax.ShapeDtypeStruct((tokens, LANES), F32),
            jax.ShapeDtypeStruct((N_EXPERTS, LANES), F32),
        ],
        compiler_params=_cparams(("arbitrary",)),
        name="mid",
    )(oa, ob, sga, sgb, x2d, w_a.astype(BF16), w_b.astype(BF16), w_o.astype(BF16),
      g_ffn.astype(F32)[None, :], wr_split, br, tri)


def _tile_copy(src_ref, src_row, dst_ref, dst_row, sem):
    return pltpu.make_async_copy(
        src_ref.at[pl.ds(pl.multiple_of(src_row, SUBLANES), SUBLANES)],
        dst_ref.at[pl.ds(pl.multiple_of(dst_row, SUBLANES), SUBLANES)], sem)


def _start_tile_gather(off_smem, base, n_rows, src_ref, dst_ref, sem):
    def start(r, carry):
        _tile_copy(src_ref, off_smem[base + r], dst_ref, r * SUBLANES, sem).start()
        return carry
    lax.fori_loop(0, n_rows, start, 0, unroll=8)


def _start_tile_scatter(off_smem, base, n_rows, src_ref, dst_ref, sem):
    def start(r, carry):
        _tile_copy(src_ref, r * SUBLANES, dst_ref, off_smem[base + r], sem).start()
        return carry
    lax.fori_loop(0, n_rows, start, 0, unroll=8)


def _wait_tiles(n_rows, src_ref, dst_ref, sem):
    window = pl.ds(0, n_rows * SUBLANES)
    pltpu.make_async_copy(src_ref.at[window], dst_ref.at[window], sem).wait()


SRC_WINDOWS = 3
DST_WINDOWS = 4
MOE_COL_TILES = 4
DMA_THREADS = 2


def _moe_kernel(bexp_ref, nused_ref, src_ref, dst_ref, h2_ref, wgu_ref, bgu_ref, wd_ref, bd_ref,
                y_ref, src_smem, dst_smem, xbuf, ybuf, wgu_bf, wd_bf, isem, gsem, ssem, *, rows, n_assign):
    i = pl.program_id(0)
    n_used = nused_ref[0]
    slot = lax.rem(i, 2)
    nxt = 1 - slot
    src_base = lambda blk: lax.rem(blk, SRC_WINDOWS) * rows
    dst_base = lambda blk: lax.rem(blk, DST_WINDOWS) * rows

    def idx_copies(blk):
        sw = pl.ds(pl.multiple_of(src_base(blk), rows), rows)
        dw = pl.ds(pl.multiple_of(dst_base(blk), rows), rows)
        return (pltpu.make_async_copy(src_ref.at[blk], src_smem.at[sw], isem.at[0, lax.rem(blk, SRC_WINDOWS)]),
                pltpu.make_async_copy(dst_ref.at[blk], dst_smem.at[dw], isem.at[1, lax.rem(blk, DST_WINDOWS)]))

    def fetch_idx_now(blk):
        for c in idx_copies(blk):
            c.start()
        for c in idx_copies(blk):
            c.wait()

    def gather_copy(blk_base, r, s):
        return _tile_copy(h2_ref, src_smem[blk_base + r], xbuf.at[s], r * SUBLANES, gsem.at[s])

    def scatter_copy(blk_base, r, s):
        return _tile_copy(ybuf.at[s], r * SUBLANES, y_ref, dst_smem[blk_base + r], ssem.at[s])

    def swiglu(g, u):
        gate = jnp.minimum(g, SWIGLU_LIMIT)
        up = jnp.clip(u, -SWIGLU_LIMIT, SWIGLU_LIMIT)
        return ((up + 1.0) * (gate * jax.nn.sigmoid(SWIGLU_ALPHA * gate))).astype(BF16)

    @pl.when(i == 0)
    def _():
        ybuf[...] = jnp.zeros(ybuf.shape, F32)
        for s in range(2):
            dump = pl.ds((n_assign + s * rows) * SUBLANES, rows * SUBLANES)
            zero_fill = pltpu.make_async_copy(ybuf.at[s], y_ref.at[dump], ssem.at[s])
            zero_fill.start()
            zero_fill.wait()
        fetch_idx_now(0)

        @pl.when(1 < n_used)
        def _():
            fetch_idx_now(1)

        _start_tile_gather(src_smem, 0, rows, h2_ref, xbuf.at[0], gsem.at[0])

    @pl.when(i + 2 < n_used)
    def _():
        for c in idx_copies(i + 2):
            c.start()

    @pl.when(i < n_used)
    def _():
        _wait_tiles(rows, h2_ref, xbuf.at[slot], gsem.at[slot])

    @pl.when(jnp.logical_and(i >= 2, i <= n_used))
    def _():
        _wait_tiles(rows, ybuf.at[slot], y_ref, ssem.at[slot])

    new_expert = jnp.logical_or(i == 0, bexp_ref[i] != bexp_ref[jnp.maximum(i - 1, 0)])

    @pl.when(jnp.logical_and(new_expert, i < n_used))
    def _():
        wgu_bf[...] = wgu_ref[...].astype(BF16)
        wd_bf[...] = wd_ref[...].astype(BF16)

    steady = jnp.logical_and(i >= 1, i + 1 < n_used)

    @pl.when(steady)
    def _():
        gbase, sbase = src_base(i + 1), dst_base(i - 1)
        burst = rows // MOE_COL_TILES
        width = D_FF // MOE_COL_TILES
        x = _load_token_tiles(xbuf.at[slot], rows).astype(BF16)
        acts = []
        for t in range(MOE_COL_TILES):
            cg = slice(t * width, (t + 1) * width)
            cu = slice(D_FF + t * width, D_FF + (t + 1) * width)
            g = jnp.dot(x, wgu_bf[:, cg], preferred_element_type=F32) + bgu_ref[:, cg]
            u = jnp.dot(x, wgu_bf[:, cu], preferred_element_type=F32) + bgu_ref[:, cu]
            for r in range(t * burst, (t + 1) * burst):
                gather_copy(gbase, r, nxt).start(priority=r % DMA_THREADS)
            acts.append(swiglu(g, u))
        act = jnp.concatenate(acts, axis=1)
        out_width = wd_bf.shape[1] // MOE_COL_TILES
        for t in range(MOE_COL_TILES):
            co = slice(t * out_width, (t + 1) * out_width)
            y_t = jnp.dot(act, wd_bf[:, co], preferred_element_type=F32) + bd_ref[:, co]
            for r in range(t * burst, (t + 1) * burst):
                scatter_copy(sbase, r, nxt).start(priority=r % DMA_THREADS)
            for j in range(out_width // LANES):
                ybuf[slot, pl.ds(t * (out_width // LANES) + j, rows, stride=SUBLANES), :] = (
                    y_t[:, j * LANES:(j + 1) * LANES])

    @pl.when(jnp.logical_and(i < n_used, jnp.logical_not(steady)))
    def _():
        @pl.when(i + 1 < n_used)
        def _():
            _start_tile_gather(src_smem, src_base(i + 1), rows, h2_ref, xbuf.at[nxt], gsem.at[nxt])

        x = _load_token_tiles(xbuf.at[slot], rows).astype(BF16)
        gu = jnp.dot(x, wgu_bf[...], preferred_element_type=F32) + bgu_ref[...]
        act = swiglu(gu[:, :D_FF], gu[:, D_FF:])
        y = jnp.dot(act, wd_bf[...], preferred_element_type=F32) + bd_ref[...]
        _store_token_tiles(ybuf.at[slot], y)

        @pl.when(i >= 1)
        def _():
            _start_tile_scatter(dst_smem, dst_base(i - 1), rows, ybuf.at[nxt], y_ref, ssem.at[nxt])

    @pl.when(i == n_used)
    def _():
        _start_tile_scatter(dst_smem, dst_base(i - 1), rows, ybuf.at[nxt], y_ref, ssem.at[nxt])
        _wait_tiles(rows, ybuf.at[nxt], y_ref, ssem.at[nxt])

    @pl.when(i + 2 < n_used)
    def _():
        for c in idx_copies(i + 2):
            c.wait()


def _moe(block_exp, n_used, src_rows, dst_rows, h2_tiles, w_gate_up, b_gate_up, w_down, b_down, rows, n_assign):
    n_blocks = src_rows.shape[0]
    n_exp, d_model, ff2 = w_gate_up.shape
    grid_spec = pltpu.PrefetchScalarGridSpec(
        num_scalar_prefetch=2,
        grid=(n_blocks,),
        in_specs=[
            pl.BlockSpec(memory_space=pl.ANY),
            pl.BlockSpec(memory_space=pl.ANY),
            pl.BlockSpec(memory_space=pl.ANY),
            pl.BlockSpec((None, d_model, ff2), lambda i, be, nu: (be[i], 0, 0)),
            pl.BlockSpec((None, 1, ff2), lambda i, be, nu: (be[i], 0, 0)),
            pl.BlockSpec((None, ff2 // 2, d_model), lambda i, be, nu: (be[i], 0, 0)),
            pl.BlockSpec((None, 1, d_model), lambda i, be, nu: (be[i], 0, 0)),
        ],
        out_specs=pl.BlockSpec(memory_space=pl.ANY),
        scratch_shapes=[
            pltpu.SMEM((SRC_WINDOWS * rows,), jnp.int32),
            pltpu.SMEM((DST_WINDOWS * rows,), jnp.int32),
            pltpu.VMEM((2, rows * SUBLANES, LANES), F32),
            pltpu.VMEM((2, rows * SUBLANES, LANES), F32),
            pltpu.VMEM((d_model, ff2), BF16),
            pltpu.VMEM((ff2 // 2, d_model), BF16),
            pltpu.SemaphoreType.DMA((2, DST_WINDOWS)),
            pltpu.SemaphoreType.DMA((2,)),
            pltpu.SemaphoreType.DMA((2,)),
        ],
    )
    return pl.pallas_call(
        functools.partial(_moe_kernel, rows=rows, n_assign=n_assign),
        grid_spec=grid_spec,
        out_shape=jax.ShapeDtypeStruct(((n_assign + 2 * rows) * SUBLANES, LANES), F32),
        compiler_params=_cparams(("arbitrary",)),
        name="moe",
    )(block_exp, n_used, src_rows, dst_rows, h2_tiles, w_gate_up, b_gate_up[:, None, :], w_down,
      b_down[:, None, :])


def _combine_kernel(y_ref, gate_ref, x1_ref, g_ref, o_ref, *, tm):
    gate = gate_ref[...]
    out = x1_ref[...]
    stride = TOP_K * SUBLANES
    for k in range(TOP_K):
        y_k = jnp.concatenate(
            [y_ref[pl.ds(k * SUBLANES + j, tm, stride=stride), :] for j in range(SUBLANES)], axis=1)
        out = out + gate[:, k:k + 1] * y_k
    ms = jnp.mean(out * out, axis=-1, keepdims=True)
    o_ref[...] = out * lax.rsqrt(ms + EPS) * g_ref[...]


def _combine(y_tiles, gate, x1, g_final, tm):
    tokens, d_model = x1.shape
    row = lambda i: (i, 0)
    return pl.pallas_call(
        functools.partial(_combine_kernel, tm=tm),
        grid=(tokens // tm,),
        in_specs=[
            pl.BlockSpec((tm * TOP_K * SUBLANES, LANES), row),
            pl.BlockSpec((tm, LANES), row),
            pl.BlockSpec((tm, d_model), row),
            pl.BlockSpec((1, d_model), lambda i: (0, 0)),
        ],
        out_specs=pl.BlockSpec((tm, d_model), row),
        out_shape=jax.ShapeDtypeStruct((tokens, d_model), F32),
        compiler_params=_cparams(("arbitrary",)),
        name="combine",
    )(y_tiles, gate, x1, g_final.astype(F32)[None, :])


PROJ_ROWS = 256
MID_ROWS = 512
ATTN_Q_ROWS = 512
ATTN_K_ROWS = 512
WINDOW_BLOCKS_PER_STEP = 4
MOE_ROWS = 256
COMBINE_ROWS = 256


def _layer(x2d, batch, seq, g_mix, w_in, qa_norm, ka_norm, sink_b, w_branch_a, w_branch_b, w_out,
           g_ffn, w_router, b_router, w_gate_up, b_gate_up, w_down, b_down, g_final):
    tokens = x2d.shape[0]
    qa, ka, vta, qb, kb, vtb, sga, sgb = _project(x2d, g_mix, w_in, qa_norm, ka_norm, seq,
                                                  min(PROJ_ROWS, seq), min(ATTN_K_ROWS, seq // 2))
    oa = _attention_global(qa, ka, vta, batch, seq, ATTN_Q_ROWS)
    ob = _attention_window(qb, kb, vtb, sink_b, batch, seq)
    x1, h2_tiles, route, gate, cnt = _mid(
        oa, ob, sga, sgb, x2d, w_branch_a, w_branch_b, w_out, g_ffn, w_router, b_router, min(MID_ROWS, tokens))

    rows = MOE_ROWS
    n_assign = tokens * TOP_K
    n_blocks = n_assign // rows + N_EXPERTS
    counts = cnt[:, 0].astype(jnp.int32)
    padded = (counts + rows - 1) // rows * rows
    ends = jnp.cumsum(padded)
    starts = ends - padded
    eidx, rank = route[:TOP_K], route[SUBLANES:SUBLANES + TOP_K]
    seg_start = jnp.sum(jnp.where(eidx[None] == jnp.arange(N_EXPERTS)[:, None, None], starts[:, None, None], 0),
                        axis=0)
    dest = seg_start + rank
    assign = jnp.arange(tokens, dtype=jnp.int32)[None, :] * TOP_K + jnp.arange(TOP_K, dtype=jnp.int32)[:, None]
    row_assign = jnp.full((n_blocks * rows,), -1, jnp.int32).at[dest.reshape(-1)].set(
        assign.reshape(-1), unique_indices=True).reshape(n_blocks, rows)
    real = row_assign >= 0
    dump_slot = (n_assign + (jnp.arange(n_blocks, dtype=jnp.int32) % 2)[:, None] * rows
                 + jnp.arange(rows, dtype=jnp.int32)[None, :])
    src_rows = jnp.where(real, row_assign // TOP_K, 0) * SUBLANES
    dst_rows = jnp.where(real, row_assign, dump_slot) * SUBLANES
    block_start = jnp.arange(n_blocks, dtype=jnp.int32) * rows
    block_exp = jnp.minimum(
        jnp.sum((ends[None, :] <= block_start[:, None]).astype(jnp.int32), axis=1), N_EXPERTS - 1)
    n_used = (ends[-1:] // rows).astype(jnp.int32)

    y_tiles = _moe(block_exp, n_used, src_rows, dst_rows, h2_tiles,
                   w_gate_up, b_gate_up, w_down, b_down, rows, n_assign)
    return _combine(y_tiles, gate, x1, g_final, min(COMBINE_ROWS, tokens))


def kernel(x, g_mix, w_in, qa_norm, ka_norm, sink_b, w_branch_a, w_branch_b, w_out, g_ffn, w_router, b_router,
           w_gate_up, b_gate_up, w_down, b_down, g_final):
    batch, seq, d_model = x.shape
    assert g_mix.shape[0] == 1, "single-layer block"
    out = _layer(x.reshape(batch * seq, d_model), batch, seq, g_mix[0], w_in[0], qa_norm[0], ka_norm[0],
                 sink_b[0], w_branch_a[0], w_branch_b[0], w_out[0], g_ffn[0], w_router[0], b_router[0],
                 w_gate_up[0], b_gate_up[0], w_down[0], b_down[0], g_final)
    return out.reshape(batch, seq, d_model)
```

```python
import functools

import jax
import jax.numpy as jnp
from jax import lax
from jax.experimental import pallas as pl
from jax.experimental.pallas import tpu as pltpu

HEAD_DIM = 64
Q_HEADS = 8
KV_HEADS = 2
GROUPS = Q_HEADS // KV_HEADS
Q_COLS = Q_HEADS * HEAD_DIM
KV_COLS = KV_HEADS * HEAD_DIM
GROUP_COLS = GROUPS * HEAD_DIM
WINDOW = 128
GRID_W = 64
ROPE_THETA = 10000.0
N_EXPERTS = 32
TOP_K = 4
D_FF = 1024
SWIGLU_LIMIT = 7.0
SWIGLU_ALPHA = 1.702
EPS = 1e-6
NEG_INF = -1e30
LOG2_E = 1.4426950408889634

LANES = 128
SUBLANES = 8
V_ROWS = 80
VMEM_LIMIT = 56 * 1024 * 1024

F32 = jnp.float32
BF16 = jnp.bfloat16


def _cparams(sem, vmem=VMEM_LIMIT):
    return pltpu.CompilerParams(dimension_semantics=sem, vmem_limit_bytes=vmem)


def _rope(y, cos, sin_signed):
    lane = lax.broadcasted_iota(jnp.int32, y.shape, 1)
    first_half = (lane & 32) == 0
    rot = jnp.where(first_half, pltpu.roll(y, 96, 1), pltpu.roll(y, 32, 1))
    return y * cos + rot * sin_signed


def _head_rmsnorm(q, ones_blockdiag, gain):
    q2 = q * q
    hi = q2.astype(BF16)
    lo = (q2 - hi.astype(F32)).astype(BF16)
    ss = (jnp.dot(hi, ones_blockdiag, preferred_element_type=F32)
          + jnp.dot(lo, ones_blockdiag, preferred_element_type=F32))
    return q * lax.rsqrt(ss * (1.0 / HEAD_DIM) + EPS) * gain


def _proj_kernel(x_ref, g_ref, w_ref, cosa_ref, sina_ref, cos1_ref, sin1_ref, qg_ref, kg_ref, bd_ref,
                 qa_ref, ka_ref, vta_ref, qb_ref, kb_ref, vtb_ref, sga_ref, sgb_ref, *, d_model):
    x = x_ref[...]
    ms = jnp.mean(x * x, axis=-1, keepdims=True)
    h = (x * lax.rsqrt(ms + EPS) * g_ref[...]).astype(BF16)
    scale = HEAD_DIM ** -0.5 * LOG2_E

    def proj(c0, width):
        return jnp.dot(h, w_ref[:, c0:c0 + width], preferred_element_type=F32)

    cosa, sina = cosa_ref[...], sina_ref[...]
    cos1, sin1 = cos1_ref[...], sin1_ref[...]
    bd = bd_ref[...]
    slab = lambda v, s: v[:, s * LANES:(s + 1) * LANES]
    c = 0
    qa = proj(c, Q_COLS)
    for s in range(Q_COLS // LANES):
        q = _head_rmsnorm(slab(qa, s), bd, qg_ref[...])
        qa_ref[:, s * LANES:(s + 1) * LANES] = (_rope(q, cosa, sina) * scale).astype(BF16)
    c += Q_COLS
    kva = proj(c, 2 * KV_COLS)
    ka = _rope(_head_rmsnorm(slab(kva, 0), bd, kg_ref[...]), cosa, sina).astype(BF16)
    va_t = slab(kva, 1).T
    ones = jnp.ones((V_ROWS - HEAD_DIM, va_t.shape[1]), BF16)
    for hd in range(KV_HEADS):
        ka_ref[hd] = ka[:, hd * HEAD_DIM:(hd + 1) * HEAD_DIM]
        vta_ref[hd, :HEAD_DIM, :] = va_t[hd * HEAD_DIM:(hd + 1) * HEAD_DIM, :].astype(BF16)
        vta_ref[hd, HEAD_DIM:, :] = ones
    c += 2 * KV_COLS
    qb = proj(c, Q_COLS)
    for s in range(Q_COLS // LANES):
        qb_ref[:, s * LANES:(s + 1) * LANES] = (_rope(slab(qb, s), cos1, sin1) * scale).astype(BF16)
    c += Q_COLS
    kvb = proj(c, 2 * KV_COLS)
    kb = _rope(slab(kvb, 0), cos1, sin1).astype(BF16)
    vb_t = slab(kvb, 1).T
    ones = jnp.ones((V_ROWS - HEAD_DIM, WINDOW), BF16)
    for hd in range(KV_HEADS):
        kb_ref[hd] = kb[:, hd * HEAD_DIM:(hd + 1) * HEAD_DIM]
        for ch in range(vtb_ref.shape[1]):
            vtb_ref[hd, ch, :HEAD_DIM, :] = vb_t[hd * HEAD_DIM:(hd + 1) * HEAD_DIM,
                                                 ch * WINDOW:(ch + 1) * WINDOW].astype(BF16)
            vtb_ref[hd, ch, HEAD_DIM:, :] = ones
    c += 2 * KV_COLS
    sga_ref[...] = jax.nn.sigmoid(proj(c, d_model)).astype(BF16)
    c += d_model
    sgb_ref[...] = jax.nn.sigmoid(proj(c, d_model)).astype(BF16)


def _rope_tables(seq):
    def slab(ang):
        cos, sin = jnp.cos(ang), jnp.sin(ang)
        cos64 = jnp.concatenate([cos, cos], axis=-1)
        sin64 = jnp.concatenate([-sin, sin], axis=-1)
        return jnp.tile(cos64, (1, 2)), jnp.tile(sin64, (1, 2))

    pos = jnp.arange(seq, dtype=F32)
    inv1 = ROPE_THETA ** (-jnp.arange(0, HEAD_DIM, 2, dtype=F32) / HEAD_DIM)
    ang1 = pos[:, None] * inv1[None, :]
    rows = seq // GRID_W
    row = jnp.broadcast_to(jnp.arange(rows, dtype=F32)[:, None], (rows, GRID_W)).reshape(-1)
    col = jnp.broadcast_to(jnp.arange(GRID_W, dtype=F32)[None, :], (rows, GRID_W)).reshape(-1)
    axis_dim = HEAD_DIM // 2
    inva = ROPE_THETA ** (-jnp.arange(0, axis_dim, 2, dtype=F32) / axis_dim)
    anga = jnp.concatenate([row[:, None] * inva[None, :], col[:, None] * inva[None, :]], axis=-1)
    return slab(anga), slab(ang1)


def _project(x2d, g_mix, w_in, qa_norm, ka_norm, seq, tm, tk):
    tokens, d_model = x2d.shape
    in_cols = w_in.shape[1]
    (cosa, sina), (cos1, sin1) = _rope_tables(seq)
    head_id = jnp.arange(LANES) // HEAD_DIM
    ones_blockdiag = (head_id[:, None] == head_id[None, :]).astype(BF16)
    qg = jnp.tile(qa_norm.astype(F32), 2)[None, :]
    kg = jnp.tile(ka_norm.astype(F32), 2)[None, :]
    seq_tiles = seq // tm
    row = lambda i: (i, 0)
    pos = lambda i: (i % seq_tiles, 0)
    fixed = lambda i: (0, 0)
    batch = tokens // seq
    assert tk % tm == 0 and tm % WINDOW == 0
    sub = tk // tm
    bf = lambda *shape: jax.ShapeDtypeStruct(shape, BF16)
    tok_spec = lambda cols: pl.BlockSpec((tm, cols), row)
    k_spec = pl.BlockSpec((None, KV_HEADS, tm, HEAD_DIM), lambda i: (i // seq_tiles, 0, i % seq_tiles, 0))
    vta_spec = pl.BlockSpec((None, KV_HEADS, None, V_ROWS, tm),
                            lambda i: (i // seq_tiles, 0, (i % seq_tiles) // sub, 0, (i % seq_tiles) % sub))
    vtb_spec = pl.BlockSpec((None, KV_HEADS, tm // WINDOW, V_ROWS, WINDOW),
                            lambda i: (i // seq_tiles, 0, i % seq_tiles, 0, 0))
    k_shape = bf(batch, KV_HEADS, seq, HEAD_DIM)
    return pl.pallas_call(
        functools.partial(_proj_kernel, d_model=d_model),
        grid=(tokens // tm,),
        in_specs=[
            pl.BlockSpec((tm, d_model), row),
            pl.BlockSpec((1, d_model), fixed),
            pl.BlockSpec((d_model, in_cols), fixed),
            pl.BlockSpec((tm, LANES), pos), pl.BlockSpec((tm, LANES), pos),
            pl.BlockSpec((tm, LANES), pos), pl.BlockSpec((tm, LANES), pos),
            pl.BlockSpec((1, LANES), fixed), pl.BlockSpec((1, LANES), fixed),
            pl.BlockSpec((LANES, LANES), fixed),
        ],
        out_specs=[tok_spec(Q_COLS), k_spec, vta_spec, tok_spec(Q_COLS), k_spec, vtb_spec,
                   tok_spec(d_model), tok_spec(d_model)],
        out_shape=[bf(tokens, Q_COLS), k_shape, bf(batch, KV_HEADS, seq // tk, V_ROWS, tk),
                   bf(tokens, Q_COLS), k_shape, bf(batch, KV_HEADS, seq // WINDOW, V_ROWS, WINDOW),
                   bf(tokens, d_model), bf(tokens, d_model)],
        compiler_params=_cparams(("arbitrary",)),
        name="proj",
    )(x2d, g_mix.astype(F32)[None, :], w_in.astype(BF16), cosa, sina, cos1, sin1, qg, kg, ones_blockdiag)


def _stack_group_queries(q):
    return jnp.concatenate([q[:, g * HEAD_DIM:(g + 1) * HEAD_DIM] for g in range(GROUPS)], axis=0)


def _unstack_group_outputs(o_t, tq):
    pad = jnp.zeros((LANES - HEAD_DIM, o_t.shape[1]), o_t.dtype)
    o_pad = jnp.concatenate([o_t, pad], axis=0)
    heads = [o_pad[:, g * tq:(g + 1) * tq].T[:, :HEAD_DIM] for g in range(GROUPS)]
    return jnp.concatenate(heads, axis=1)


def _attn_global_kernel(q_ref, k_ref, vt_ref, o_ref, s0, s1, p0, p1, acc_ref, *, tq, tk, n_chunks):
    q_st = _stack_group_queries(q_ref[...])
    m_rows = q_st.shape[0]

    def scores(c, s_ref):
        k_c = k_ref[pl.ds(pl.multiple_of(c * tk, tk), tk), :]
        s_ref[...] = lax.dot_general(k_c, q_st, (((1,), (1,)), ((), ())), preferred_element_type=F32)

    def softmax(s_ref, p_ref, m_prev):
        m_new = jnp.maximum(m_prev, jnp.max(s_ref[...], axis=0, keepdims=True))
        p_ref[...] = jnp.exp2(s_ref[...] - m_new).astype(BF16)
        return m_new, jnp.exp2(m_prev - m_new)

    def values(c, p_ref, alpha):
        acc_ref[...] = acc_ref[...] * alpha + jnp.dot(vt_ref[c], p_ref[...], preferred_element_type=F32)

    scores(0, s0)
    p1[...] = jnp.zeros(p1.shape, BF16)
    acc_ref[...] = jnp.zeros(acc_ref.shape, F32)

    def pair(j, carry):
        m, alpha = carry
        c = 2 * j
        scores(c + 1, s1)
        m_a, alpha_a = softmax(s0, p0, m)
        values(jnp.maximum(c - 1, 0), p1, alpha)
        scores(jnp.minimum(c + 2, n_chunks - 1), s0)
        m_b, alpha_b = softmax(s1, p1, m_a)
        values(c, p0, alpha_a)
        return m_b, alpha_b

    init = (jnp.full((1, m_rows), NEG_INF, F32), jnp.ones((1, m_rows), F32))
    _, alpha = lax.fori_loop(0, n_chunks // 2, pair, init)
    values(n_chunks - 1, p1, alpha)
    acc = acc_ref[...]
    o_t = acc[:HEAD_DIM, :] / acc[HEAD_DIM:HEAD_DIM + 1, :]
    o_ref[...] = _unstack_group_outputs(o_t, tq).astype(o_ref.dtype)


def _attn_window_kernel(q_ref, k_ref, vt_ref, sink_ref, o_ref, *, tq, n_blocks, blocks_per_step):
    span = 3 * tq
    sink = sink_ref[...]
    for i in range(blocks_per_step):
        n = pl.program_id(2) * blocks_per_step + i
        c0 = jnp.clip(n - 1, 0, n_blocks - 3)
        start = pl.multiple_of(c0 * tq, tq)
        q_st = _stack_group_queries(q_ref[i * tq:(i + 1) * tq, :])
        k_w = k_ref[pl.ds(start, span), :]
        s = lax.dot_general(k_w, q_st, (((1,), (1,)), ((), ())), preferred_element_type=F32)
        kpos = start + lax.broadcasted_iota(jnp.int32, s.shape, 0)
        qpos = n * tq + (lax.broadcasted_iota(jnp.int32, s.shape, 1) & (tq - 1))
        s = jnp.where(jnp.abs(kpos - qpos) <= WINDOW, s, NEG_INF)
        m = jnp.maximum(jnp.max(s, axis=0, keepdims=True), sink)
        p = jnp.exp2(s - m).astype(BF16)
        acc = jnp.dot(vt_ref[c0], p[:tq], preferred_element_type=F32)
        for j in range(1, 3):
            acc += jnp.dot(vt_ref[c0 + j], p[j * tq:(j + 1) * tq], preferred_element_type=F32)
        denom = acc[HEAD_DIM:HEAD_DIM + 1, :] + jnp.exp2(sink - m)
        o_ref[i * tq:(i + 1) * tq, :] = _unstack_group_outputs(acc[:HEAD_DIM, :] / denom, tq).astype(o_ref.dtype)


def _attention_global(q2d, k_hm, v_t, batch, seq, tq):
    tokens = q2d.shape[0]
    n_chunks, tk = v_t.shape[2], v_t.shape[4]
    assert n_chunks % 2 == 0, "the chunk pipeline advances two chunks per step"
    q_tiles = seq // tq
    m_rows = GROUPS * tq
    qmap = lambda b, h, i: (b * q_tiles + i, h)
    return pl.pallas_call(
        functools.partial(_attn_global_kernel, tq=tq, tk=tk, n_chunks=n_chunks),
        grid=(batch, KV_HEADS, q_tiles),
        in_specs=[
            pl.BlockSpec((tq, GROUP_COLS), qmap),
            pl.BlockSpec((None, None, seq, HEAD_DIM), lambda b, h, i: (b, h, 0, 0)),
            pl.BlockSpec((None, None, n_chunks, V_ROWS, tk), lambda b, h, i: (b, h, 0, 0, 0)),
        ],
        out_specs=pl.BlockSpec((tq, GROUP_COLS), qmap),
        out_shape=jax.ShapeDtypeStruct((tokens, Q_COLS), BF16),
        scratch_shapes=[pltpu.VMEM((tk, m_rows), F32), pltpu.VMEM((tk, m_rows), F32),
                        pltpu.VMEM((tk, m_rows), BF16), pltpu.VMEM((tk, m_rows), BF16),
                        pltpu.VMEM((V_ROWS, m_rows), F32)],
        compiler_params=_cparams(("arbitrary", "arbitrary", "arbitrary")),
        name="attn_global",
    )(q2d, k_hm, v_t)


def _attention_window(q2d, k_hm, v_t, sink, batch, seq):
    tokens = q2d.shape[0]
    tq = WINDOW
    n_blocks = seq // tq
    assert n_blocks >= 3
    m_rows = GROUPS * tq
    sink_rows = jnp.repeat(sink.astype(F32).reshape(KV_HEADS, GROUPS) * LOG2_E, tq, axis=1)[:, None, :]
    per_step = WINDOW_BLOCKS_PER_STEP if n_blocks % WINDOW_BLOCKS_PER_STEP == 0 else 1
    steps = n_blocks // per_step
    qmap = lambda b, h, i: (b * steps + i, h)
    return pl.pallas_call(
        functools.partial(_attn_window_kernel, tq=tq, n_blocks=n_blocks, blocks_per_step=per_step),
        grid=(batch, KV_HEADS, steps),
        in_specs=[
            pl.BlockSpec((per_step * tq, GROUP_COLS), qmap),
            pl.BlockSpec((None, None, seq, HEAD_DIM), lambda b, h, i: (b, h, 0, 0)),
            pl.BlockSpec((None, None, n_blocks, V_ROWS, tq), lambda b, h, i: (b, h, 0, 0, 0)),
            pl.BlockSpec((None, 1, m_rows), lambda b, h, i: (h, 0, 0)),
        ],
        out_specs=pl.BlockSpec((per_step * tq, GROUP_COLS), qmap),
        out_shape=jax.ShapeDtypeStruct((tokens, Q_COLS), BF16),
        compiler_params=_cparams(("arbitrary", "arbitrary", "arbitrary")),
        name="attn_window",
    )(q2d, k_hm, v_t, sink_rows)


def _store_token_tiles(ref, val):
    rows = val.shape[0]
    for j in range(val.shape[1] // LANES):
        ref[pl.ds(j, rows, stride=SUBLANES), :] = val[:, j * LANES:(j + 1) * LANES]


def _load_token_tiles(ref, rows):
    return jnp.concatenate([ref[pl.ds(j, rows, stride=SUBLANES), :] for j in range(SUBLANES)], axis=1)


ROUTE_ROWS = 2 * SUBLANES


def _top_k_route(logits_t):
    eid = lax.broadcasted_iota(jnp.int32, logits_t.shape, 0)
    work = logits_t
    vals, idxs, sels = [], [], []
    for _ in range(TOP_K):
        mx = jnp.max(work, axis=0, keepdims=True)
        ix = jnp.min(jnp.where(work == mx, eid, N_EXPERTS), axis=0, keepdims=True)
        sel = eid == ix
        work = jnp.where(sel, -jnp.inf, work)
        vals.append(mx)
        idxs.append(ix)
        sels.append(sel)
    return vals, idxs, sels


def _mid_kernel(oa_ref, ob_ref, sga_ref, sgb_ref, x_ref, wa_ref, wb_ref, wo_ref, g_ref, wr_ref, br_ref, tri_ref,
                x1_ref, h2_ref, route_ref, gate_ref, cnt_ref, *, halves):
    @pl.when(pl.program_id(0) == 0)
    def _():
        cnt_ref[...] = jnp.zeros(cnt_ref.shape, F32)

    tm = x_ref.shape[0]
    hm = tm // halves
    routed = []
    for hf in range(halves):
        rs = slice(hf * hm, (hf + 1) * hm)
        a = jnp.dot(oa_ref[rs, :], wa_ref[...], preferred_element_type=F32)
        b = jnp.dot(ob_ref[rs, :], wb_ref[...], preferred_element_type=F32)
        merged = sga_ref[rs, :].astype(F32) * a + sgb_ref[rs, :].astype(F32) * b
        x1 = x_ref[rs, :] + jnp.dot(merged.astype(BF16), wo_ref[...], preferred_element_type=F32)
        x1_ref[rs, :] = x1
        ms = jnp.mean(x1 * x1, axis=-1, keepdims=True)
        h2 = x1 * lax.rsqrt(ms + EPS) * g_ref[...]
        _store_token_tiles(h2_ref.at[pl.ds(hf * hm * SUBLANES, hm * SUBLANES)], h2)
        hi = h2.astype(BF16)
        lo = (h2 - hi.astype(F32)).astype(BF16)
        nt = (((1,), (1,)), ((), ()))
        r2 = (lax.dot_general(wr_ref[...], hi, nt, preferred_element_type=F32)
              + lax.dot_general(wr_ref[...], lo, nt, preferred_element_type=F32))
        bias = jnp.concatenate([br_ref[...]] * (hm // LANES), axis=1)
        routed.append(_top_k_route(r2[:N_EXPERTS] + r2[N_EXPERTS:] + bias))

    onehots = []
    for vals, idxs, sels in routed:
        chosen = sels[0] | sels[1] | sels[2] | sels[3]
        onehots.append(jnp.where(chosen, 1.0, 0.0))
    onehot = jnp.concatenate(onehots, axis=1)
    counted = jnp.concatenate([cnt_ref[...]] * (tm // LANES), axis=1)
    before = jnp.dot(onehot.astype(BF16), tri_ref[...], preferred_element_type=F32) + counted
    zeros4 = jnp.zeros((SUBLANES - TOP_K, hm), jnp.int32)
    for hf, (vals, idxs, sels) in enumerate(routed):
        rs = slice(hf * hm, (hf + 1) * hm)
        exps = [jnp.exp(v - vals[0]) for v in vals]
        denom = exps[0] + exps[1] + exps[2] + exps[3]
        ranks = [jnp.sum(jnp.where(sels[k], before[:, rs], 0.0), axis=0, keepdims=True).astype(jnp.int32)
                 for k in range(TOP_K)]
        route_ref[:, rs] = jnp.concatenate(idxs + [zeros4] + ranks + [zeros4], axis=0)
        gates_t = jnp.concatenate([e / denom for e in exps] + [jnp.zeros((LANES - TOP_K, hm), F32)], axis=0)
        gate_ref[rs, :] = gates_t.T
    cnt_ref[...] = cnt_ref[...] + jnp.sum(onehot, axis=1, keepdims=True)


def _mid(oa, ob, sga, sgb, x2d, w_a, w_b, w_o, g_ffn, w_router, b_router, tm):
    tokens, d_model = x2d.shape
    wr_t = w_router.astype(F32).T
    wr_hi = wr_t.astype(BF16)
    wr_lo = (wr_t - wr_hi.astype(F32)).astype(BF16)
    wr_split = jnp.concatenate([wr_hi, wr_lo], axis=0)
    br = jnp.broadcast_to(b_router.astype(F32)[:, None], (N_EXPERTS, LANES))
    tri = (jnp.arange(tm)[:, None] < jnp.arange(tm)[None, :]).astype(BF16)
    halves = 2 if tm % (2 * LANES) == 0 else 1
    row = lambda i: (i, 0)
    fixed = lambda i: (0, 0)
    return pl.pallas_call(
        functools.partial(_mid_kernel, halves=halves),
        grid=(tokens // tm,),
        in_specs=[
            pl.BlockSpec((tm, Q_COLS), row), pl.BlockSpec((tm, Q_COLS), row),
            pl.BlockSpec((tm, d_model), row), pl.BlockSpec((tm, d_model), row),
            pl.BlockSpec((tm, d_model), row),
            pl.BlockSpec((Q_COLS, d_model), fixed), pl.BlockSpec((Q_COLS, d_model), fixed),
            pl.BlockSpec((d_model, d_model), fixed),
            pl.BlockSpec((1, d_model), fixed),
            pl.BlockSpec((2 * N_EXPERTS, d_model), fixed),
            pl.BlockSpec((N_EXPERTS, LANES), fixed),
            pl.BlockSpec((tm, tm), fixed),
        ],
        out_specs=[
            pl.BlockSpec((tm, d_model), row),
            pl.BlockSpec((tm * SUBLANES, LANES), row),
            pl.BlockSpec((ROUTE_ROWS, tm), lambda i: (0, i)),
            pl.BlockSpec((tm, LANES), row),
            pl.BlockSpec((N_EXPERTS, LANES), fixed),
        ],
        out_shape=[
            jax.ShapeDtypeStruct((tokens, d_model), F32),
            jax.ShapeDtypeStruct((tokens * SUBLANES, LANES), F32),
            jax.ShapeDtypeStruct((ROUTE_ROWS, tokens), jnp.int32),
            jax.ShapeDtypeStruct((tokens, LANES), F32),
            jax.ShapeDtypeStruct((N_EXPERTS, LANES), F32),
        ],
        compiler_params=_cparams(("arbitrary",)),
        name="mid",
    )(oa, ob, sga, sgb, x2d, w_a.astype(BF16), w_b.astype(BF16), w_o.astype(BF16),
      g_ffn.astype(F32)[None, :], wr_split, br, tri)


def _tile_copy(src_ref, src_row, dst_ref, dst_row, sem):
    return pltpu.make_async_copy(
        src_ref.at[pl.ds(pl.multiple_of(src_row, SUBLANES), SUBLANES)],
        dst_ref.at[pl.ds(pl.multiple_of(dst_row, SUBLANES), SUBLANES)], sem)


def _start_tile_gather(off_smem, base, n_rows, src_ref, dst_ref, sem):
    def start(r, carry):
        _tile_copy(src_ref, off_smem[base + r], dst_ref, r * SUBLANES, sem).start()
        return carry
    lax.fori_loop(0, n_rows, start, 0, unroll=8)


def _start_tile_scatter(off_smem, base, n_rows, src_ref, dst_ref, sem):
    def start(r, carry):
        _tile_copy(src_ref, r * SUBLANES, dst_ref, off_smem[base + r], sem).start()
        return carry
    lax.fori_loop(0, n_rows, start, 0, unroll=8)


def _wait_tiles(n_rows, src_ref, dst_ref, sem):
    window = pl.ds(0, n_rows * SUBLANES)
    pltpu.make_async_copy(src_ref.at[window], dst_ref.at[window], sem).wait()


SRC_WINDOWS = 3
DST_WINDOWS = 5
MOE_BUFFERS = 3
MOE_COL_TILES = 4
DMA_THREADS = 2


def _moe_kernel(bexp_ref, nused_ref, src_ref, dst_ref, h2_ref, wgu_ref, bgu_ref, wd_ref, bd_ref,
                y_ref, src_smem, dst_smem, xbuf, ybuf, wgu_bf, wd_bf, isem, gsem, ssem, *, rows, n_assign):
    i = pl.program_id(0)
    n_used = nused_ref[0]
    cur = lax.rem(i, MOE_BUFFERS)
    far = lax.rem(i + 2, MOE_BUFFERS)
    src_base = lambda blk: lax.rem(blk, SRC_WINDOWS) * rows
    dst_base = lambda blk: lax.rem(blk, DST_WINDOWS) * rows

    def idx_copies(blk):
        sw = pl.ds(pl.multiple_of(src_base(blk), rows), rows)
        dw = pl.ds(pl.multiple_of(dst_base(blk), rows), rows)
        return (pltpu.make_async_copy(src_ref.at[blk], src_smem.at[sw], isem.at[0, lax.rem(blk, SRC_WINDOWS)]),
                pltpu.make_async_copy(dst_ref.at[blk], dst_smem.at[dw], isem.at[1, lax.rem(blk, DST_WINDOWS)]))

    def fetch_idx_now(blk):
        for c in idx_copies(blk):
            c.start()
        for c in idx_copies(blk):
            c.wait()

    def gather_copy(blk_base, r, s):
        return _tile_copy(h2_ref, src_smem[blk_base + r], xbuf.at[s], r * SUBLANES, gsem.at[s])

    def scatter_copy(blk_base, r, s):
        return _tile_copy(ybuf.at[s], r * SUBLANES, y_ref, dst_smem[blk_base + r], ssem.at[s])

    def wait_scatter(s):
        _wait_tiles(rows, ybuf.at[s], y_ref, ssem.at[s])

    def swiglu(g, u):
        gate = jnp.minimum(g, SWIGLU_LIMIT)
        up = jnp.clip(u, -SWIGLU_LIMIT, SWIGLU_LIMIT)
        return ((up + 1.0) * (gate * jax.nn.sigmoid(SWIGLU_ALPHA * gate))).astype(BF16)

    @pl.when(i == 0)
    def _():
        ybuf[...] = jnp.zeros(ybuf.shape, F32)
        for s in range(2):
            dump = pl.ds((n_assign + s * rows) * SUBLANES, rows * SUBLANES)
            zero_fill = pltpu.make_async_copy(ybuf.at[s], y_ref.at[dump], ssem.at[s])
            zero_fill.start()
            zero_fill.wait()
        fetch_idx_now(0)
        _start_tile_gather(src_smem, 0, rows, h2_ref, xbuf.at[0], gsem.at[0])

        @pl.when(1 < n_used)
        def _():
            fetch_idx_now(1)
            _start_tile_gather(src_smem, src_base(1), rows, h2_ref, xbuf.at[1], gsem.at[1])

        @pl.when(2 < n_used)
        def _():
            fetch_idx_now(2)

    @pl.when(i + 3 < n_used)
    def _():
        for c in idx_copies(i + 3):
            c.start()

    @pl.when(i < n_used)
    def _():
        _wait_tiles(rows, h2_ref, xbuf.at[cur], gsem.at[cur])

    @pl.when(jnp.logical_and(i >= 3, i <= n_used))
    def _():
        wait_scatter(cur)

    new_expert = jnp.logical_or(i == 0, bexp_ref[i] != bexp_ref[jnp.maximum(i - 1, 0)])

    @pl.when(jnp.logical_and(new_expert, i < n_used))
    def _():
        wgu_bf[...] = wgu_ref[...].astype(BF16)
        wd_bf[...] = wd_ref[...].astype(BF16)

    steady = jnp.logical_and(i >= 1, i + 2 < n_used)

    @pl.when(steady)
    def _():
        gbase, sbase = src_base(i + 2), dst_base(i - 1)
        burst = rows // MOE_COL_TILES
        width = D_FF // MOE_COL_TILES
        x = _load_token_tiles(xbuf.at[cur], rows).astype(BF16)
        acts = []
        for t in range(MOE_COL_TILES):
            cg = slice(t * width, (t + 1) * width)
            cu = slice(D_FF + t * width, D_FF + (t + 1) * width)
            g = jnp.dot(x, wgu_bf[:, cg], preferred_element_type=F32) + bgu_ref[:, cg]
            u = jnp.dot(x, wgu_bf[:, cu], preferred_element_type=F32) + bgu_ref[:, cu]
            for r in range(t * burst, (t + 1) * burst):
                gather_copy(gbase, r, far).start(priority=r % DMA_THREADS)
            acts.append(swiglu(g, u))
        act = jnp.concatenate(acts, axis=1)
        out_width = wd_bf.shape[1] // MOE_COL_TILES
        for t in range(MOE_COL_TILES):
            co = slice(t * out_width, (t + 1) * out_width)
            y_t = jnp.dot(act, wd_bf[:, co], preferred_element_type=F32) + bd_ref[:, co]
            for r in range(t * burst, (t + 1) * burst):
                scatter_copy(sbase, r, far).start(priority=r % DMA_THREADS)
            for j in range(out_width // LANES):
                ybuf[cur, pl.ds(t * (out_width // LANES) + j, rows, stride=SUBLANES), :] = (
                    y_t[:, j * LANES:(j + 1) * LANES])

    @pl.when(jnp.logical_and(i < n_used, jnp.logical_not(steady)))
    def _():
        @pl.when(i + 2 < n_used)
        def _():
            _start_tile_gather(src_smem, src_base(i + 2), rows, h2_ref, xbuf.at[far], gsem.at[far])

        x = _load_token_tiles(xbuf.at[cur], rows).astype(BF16)
        gu = jnp.dot(x, wgu_bf[...], preferred_element_type=F32) + bgu_ref[...]
        act = swiglu(gu[:, :D_FF], gu[:, D_FF:])
        y = jnp.dot(act, wd_bf[...], preferred_element_type=F32) + bd_ref[...]
        _store_token_tiles(ybuf.at[cur], y)

        @pl.when(i >= 1)
        def _():
            _start_tile_scatter(dst_smem, dst_base(i - 1), rows, ybuf.at[far], y_ref, ssem.at[far])

    @pl.when(i == n_used)
    def _():
        _start_tile_scatter(dst_smem, dst_base(i - 1), rows, ybuf.at[far], y_ref, ssem.at[far])

        @pl.when(i >= 2)
        def _():
            wait_scatter(lax.rem(i + 1, MOE_BUFFERS))

        wait_scatter(far)

    @pl.when(i + 3 < n_used)
    def _():
        for c in idx_copies(i + 3):
            c.wait()


def _moe(block_exp, n_used, src_rows, dst_rows, h2_tiles, w_gate_up, b_gate_up, w_down, b_down, rows, n_assign):
    n_blocks = src_rows.shape[0]
    n_exp, d_model, ff2 = w_gate_up.shape
    grid_spec = pltpu.PrefetchScalarGridSpec(
        num_scalar_prefetch=2,
        grid=(n_blocks,),
        in_specs=[
            pl.BlockSpec(memory_space=pl.ANY),
            pl.BlockSpec(memory_space=pl.ANY),
            pl.BlockSpec(memory_space=pl.ANY),
            pl.BlockSpec((None, d_model, ff2), lambda i, be, nu: (be[i], 0, 0)),
            pl.BlockSpec((None, 1, ff2), lambda i, be, nu: (be[i], 0, 0)),
            pl.BlockSpec((None, ff2 // 2, d_model), lambda i, be, nu: (be[i], 0, 0)),
            pl.BlockSpec((None, 1, d_model), lambda i, be, nu: (be[i], 0, 0)),
        ],
        out_specs=pl.BlockSpec(memory_space=pl.ANY),
        scratch_shapes=[
            pltpu.SMEM((SRC_WINDOWS * rows,), jnp.int32),
            pltpu.SMEM((DST_WINDOWS * rows,), jnp.int32),
            pltpu.VMEM((MOE_BUFFERS, rows * SUBLANES, LANES), F32),
            pltpu.VMEM((MOE_BUFFERS, rows * SUBLANES, LANES), F32),
            pltpu.VMEM((d_model, ff2), BF16),
            pltpu.VMEM((ff2 // 2, d_model), BF16),
            pltpu.SemaphoreType.DMA((2, DST_WINDOWS)),
            pltpu.SemaphoreType.DMA((MOE_BUFFERS,)),
            pltpu.SemaphoreType.DMA((MOE_BUFFERS,)),
        ],
    )
    return pl.pallas_call(
        functools.partial(_moe_kernel, rows=rows, n_assign=n_assign),
        grid_spec=grid_spec,
        out_shape=jax.ShapeDtypeStruct(((n_assign + 2 * rows) * SUBLANES, LANES), F32),
        compiler_params=_cparams(("arbitrary",)),
        name="moe",
    )(block_exp, n_used, src_rows, dst_rows, h2_tiles, w_gate_up, b_gate_up[:, None, :], w_down,
      b_down[:, None, :])


def _combine_kernel(y_ref, gate_ref, x1_ref, g_ref, o_ref, *, tm):
    gate = gate_ref[...]
    out = x1_ref[...]
    stride = TOP_K * SUBLANES
    for k in range(TOP_K):
        y_k = jnp.concatenate(
            [y_ref[pl.ds(k * SUBLANES + j, tm, stride=stride), :] for j in range(SUBLANES)], axis=1)
        out = out + gate[:, k:k + 1] * y_k
    ms = jnp.mean(out * out, axis=-1, keepdims=True)
    o_ref[...] = out * lax.rsqrt(ms + EPS) * g_ref[...]


def _combine(y_tiles, gate, x1, g_final, tm):
    tokens, d_model = x1.shape
    row = lambda i: (i, 0)
    return pl.pallas_call(
        functools.partial(_combine_kernel, tm=tm),
        grid=(tokens // tm,),
        in_specs=[
            pl.BlockSpec((tm * TOP_K * SUBLANES, LANES), row),
            pl.BlockSpec((tm, LANES), row),
            pl.BlockSpec((tm, d_model), row),
            pl.BlockSpec((1, d_model), lambda i: (0, 0)),
        ],
        out_specs=pl.BlockSpec((tm, d_model), row),
        out_shape=jax.ShapeDtypeStruct((tokens, d_model), F32),
        compiler_params=_cparams(("arbitrary",)),
        name="combine",
    )(y_tiles, gate, x1, g_final.astype(F32)[None, :])


PROJ_ROWS = 256
MID_ROWS = 512
ATTN_Q_ROWS = 512
ATTN_K_ROWS = 512
WINDOW_BLOCKS_PER_STEP = 4
MOE_ROWS = 256
COMBINE_ROWS = 256


def _layer(x2d, batch, seq, g_mix, w_in, qa_norm, ka_norm, sink_b, w_branch_a, w_branch_b, w_out,
           g_ffn, w_router, b_router, w_gate_up, b_gate_up, w_down, b_down, g_final):
    tokens = x2d.shape[0]
    qa, ka, vta, qb, kb, vtb, sga, sgb = _project(x2d, g_mix, w_in, qa_norm, ka_norm, seq,
                                                  min(PROJ_ROWS, seq), min(ATTN_K_ROWS, seq // 2))
    oa = _attention_global(qa, ka, vta, batch, seq, ATTN_Q_ROWS)
    ob = _attention_window(qb, kb, vtb, sink_b, batch, seq)
    x1, h2_tiles, route, gate, cnt = _mid(
        oa, ob, sga, sgb, x2d, w_branch_a, w_branch_b, w_out, g_ffn, w_router, b_router, min(MID_ROWS, tokens))

    rows = MOE_ROWS
    n_assign = tokens * TOP_K
    n_blocks = n_assign // rows + N_EXPERTS
    counts = cnt[:, 0].astype(jnp.int32)
    padded = (counts + rows - 1) // rows * rows
    ends = jnp.cumsum(padded)
    starts = ends - padded
    eidx, rank = route[:TOP_K], route[SUBLANES:SUBLANES + TOP_K]
    seg_start = jnp.sum(jnp.where(eidx[None] == jnp.arange(N_EXPERTS)[:, None, None], starts[:, None, None], 0),
                        axis=0)
    dest = seg_start + rank
    assign = jnp.arange(tokens, dtype=jnp.int32)[None, :] * TOP_K + jnp.arange(TOP_K, dtype=jnp.int32)[:, None]
    row_assign = jnp.full((n_blocks * rows,), -1, jnp.int32).at[dest.reshape(-1)].set(
        assign.reshape(-1), unique_indices=True).reshape(n_blocks, rows)
    real = row_assign >= 0
    dump_slot = (n_assign + (jnp.arange(n_blocks, dtype=jnp.int32) % 2)[:, None] * rows
                 + jnp.arange(rows, dtype=jnp.int32)[None, :])
    src_rows = jnp.where(real, row_assign // TOP_K, 0) * SUBLANES
    dst_rows = jnp.where(real, row_assign, dump_slot) * SUBLANES
    block_start = jnp.arange(n_blocks, dtype=jnp.int32) * rows
    block_exp = jnp.minimum(
        jnp.sum((ends[None, :] <= block_start[:, None]).astype(jnp.int32), axis=1), N_EXPERTS - 1)
    n_used = (ends[-1:] // rows).astype(jnp.int32)

    y_tiles = _moe(block_exp, n_used, src_rows, dst_rows, h2_tiles,
                   w_gate_up, b_gate_up, w_down, b_down, rows, n_assign)
    return _combine(y_tiles, gate, x1, g_final, min(COMBINE_ROWS, tokens))


def kernel(x, g_mix, w_in, qa_norm, ka_norm, sink_b, w_branch_a, w_branch_b, w_out, g_ffn, w_router, b_router,
           w_gate_up, b_gate_up, w_down, b_down, g_final):
    batch, seq, d_model = x.shape
    assert g_mix.shape[0] == 1, "single-layer block"
    out = _layer(x.reshape(batch * seq, d_model), batch, seq, g_mix[0], w_in[0], qa_norm[0], ka_norm[0],
                 sink_b[0], w_branch_a[0], w_branch_b[0], w_out[0], g_ffn[0], w_router[0], b_router[0],
                 w_gate_up[0], b_gate_up[0], w_down[0], b_down[0], g_final)
    return out.reshape(batch, seq, d_model)
```

```python
import functools

import jax
import jax.numpy as jnp
from jax import lax
from jax.experimental import pallas as pl
from jax.experimental.pallas import tpu as pltpu

HEAD_DIM = 64
Q_HEADS = 8
KV_HEADS = 2
GROUPS = Q_HEADS // KV_HEADS
Q_COLS = Q_HEADS * HEAD_DIM
KV_COLS = KV_HEADS * HEAD_DIM
GROUP_COLS = GROUPS * HEAD_DIM
WINDOW = 128
GRID_W = 64
ROPE_THETA = 10000.0
N_EXPERTS = 32
TOP_K = 4
D_FF = 1024
SWIGLU_LIMIT = 7.0
SWIGLU_ALPHA = 1.702
EPS = 1e-6
NEG_INF = -1e30
LOG2_E = 1.4426950408889634

LANES = 128
SUBLANES = 8
V_ROWS = 80
VMEM_LIMIT = 56 * 1024 * 1024

F32 = jnp.float32
BF16 = jnp.bfloat16


def _cparams(sem, vmem=VMEM_LIMIT):
    return pltpu.CompilerParams(dimension_semantics=sem, vmem_limit_bytes=vmem)


def _rope(y, cos, sin_signed):
    lane = lax.broadcasted_iota(jnp.int32, y.shape, 1)
    first_half = (lane & 32) == 0
    rot = jnp.where(first_half, pltpu.roll(y, 96, 1), pltpu.roll(y, 32, 1))
    return y * cos + rot * sin_signed


def _head_rmsnorm(q, ones_blockdiag, gain):
    q2 = q * q
    hi = q2.astype(BF16)
    lo = (q2 - hi.astype(F32)).astype(BF16)
    ss = (jnp.dot(hi, ones_blockdiag, preferred_element_type=F32)
          + jnp.dot(lo, ones_blockdiag, preferred_element_type=F32))
    return q * lax.rsqrt(ss * (1.0 / HEAD_DIM) + EPS) * gain


def _proj_kernel(x_ref, g_ref, w_ref, cosa_ref, sina_ref, cos1_ref, sin1_ref, qg_ref, kg_ref, bd_ref,
                 qa_ref, ka_ref, vta_ref, qb_ref, kb_ref, vtb_ref, sga_ref, sgb_ref, *, d_model):
    x = x_ref[...]
    ms = jnp.mean(x * x, axis=-1, keepdims=True)
    h = (x * lax.rsqrt(ms + EPS) * g_ref[...]).astype(BF16)
    scale = HEAD_DIM ** -0.5 * LOG2_E

    def proj(c0, width):
        return jnp.dot(h, w_ref[:, c0:c0 + width], preferred_element_type=F32)

    cosa, sina = cosa_ref[...], sina_ref[...]
    cos1, sin1 = cos1_ref[...], sin1_ref[...]
    bd = bd_ref[...]
    slab = lambda v, s: v[:, s * LANES:(s + 1) * LANES]
    c = 0
    qa = proj(c, Q_COLS)
    for s in range(Q_COLS // LANES):
        q = _head_rmsnorm(slab(qa, s), bd, qg_ref[...])
        qa_ref[:, s * LANES:(s + 1) * LANES] = (_rope(q, cosa, sina) * scale).astype(BF16)
    c += Q_COLS
    kva = proj(c, 2 * KV_COLS)
    ka = _rope(_head_rmsnorm(slab(kva, 0), bd, kg_ref[...]), cosa, sina).astype(BF16)
    va_t = slab(kva, 1).T
    ones = jnp.ones((V_ROWS - HEAD_DIM, va_t.shape[1]), BF16)
    for hd in range(KV_HEADS):
        ka_ref[hd] = ka[:, hd * HEAD_DIM:(hd + 1) * HEAD_DIM]
        vta_ref[hd, :HEAD_DIM, :] = va_t[hd * HEAD_DIM:(hd + 1) * HEAD_DIM, :].astype(BF16)
        vta_ref[hd, HEAD_DIM:, :] = ones
    c += 2 * KV_COLS
    qb = proj(c, Q_COLS)
    for s in range(Q_COLS // LANES):
        qb_ref[:, s * LANES:(s + 1) * LANES] = (_rope(slab(qb, s), cos1, sin1) * scale).astype(BF16)
    c += Q_COLS
    kvb = proj(c, 2 * KV_COLS)
    kb = _rope(slab(kvb, 0), cos1, sin1).astype(BF16)
    vb_t = slab(kvb, 1).T
    ones = jnp.ones((V_ROWS - HEAD_DIM, WINDOW), BF16)
    for hd in range(KV_HEADS):
        kb_ref[hd] = kb[:, hd * HEAD_DIM:(hd + 1) * HEAD_DIM]
        for ch in range(vtb_ref.shape[1]):
            vtb_ref[hd, ch, :HEAD_DIM, :] = vb_t[hd * HEAD_DIM:(hd + 1) * HEAD_DIM,
                                                 ch * WINDOW:(ch + 1) * WINDOW].astype(BF16)
            vtb_ref[hd, ch, HEAD_DIM:, :] = ones
    c += 2 * KV_COLS
    sga_ref[...] = jax.nn.sigmoid(proj(c, d_model)).astype(BF16)
    c += d_model
    sgb_ref[...] = jax.nn.sigmoid(proj(c, d_model)).astype(BF16)


def _rope_tables(seq):
    def slab(ang):
        cos, sin = jnp.cos(ang), jnp.sin(ang)
        cos64 = jnp.concatenate([cos, cos], axis=-1)
        sin64 = jnp.concatenate([-sin, sin], axis=-1)
        return jnp.tile(cos64, (1, 2)), jnp.tile(sin64, (1, 2))

    pos = jnp.arange(seq, dtype=F32)
    inv1 = ROPE_THETA ** (-jnp.arange(0, HEAD_DIM, 2, dtype=F32) / HEAD_DIM)
    ang1 = pos[:, None] * inv1[None, :]
    rows = seq // GRID_W
    row = jnp.broadcast_to(jnp.arange(rows, dtype=F32)[:, None], (rows, GRID_W)).reshape(-1)
    col = jnp.broadcast_to(jnp.arange(GRID_W, dtype=F32)[None, :], (rows, GRID_W)).reshape(-1)
    axis_dim = HEAD_DIM // 2
    inva = ROPE_THETA ** (-jnp.arange(0, axis_dim, 2, dtype=F32) / axis_dim)
    anga = jnp.concatenate([row[:, None] * inva[None, :], col[:, None] * inva[None, :]], axis=-1)
    return slab(anga), slab(ang1)


def _project(x2d, g_mix, w_in, qa_norm, ka_norm, seq, tm, tk):
    tokens, d_model = x2d.shape
    in_cols = w_in.shape[1]
    (cosa, sina), (cos1, sin1) = _rope_tables(seq)
    head_id = jnp.arange(LANES) // HEAD_DIM
    ones_blockdiag = (head_id[:, None] == head_id[None, :]).astype(BF16)
    qg = jnp.tile(qa_norm.astype(F32), 2)[None, :]
    kg = jnp.tile(ka_norm.astype(F32), 2)[None, :]
    seq_tiles = seq // tm
    row = lambda i: (i, 0)
    pos = lambda i: (i % seq_tiles, 0)
    fixed = lambda i: (0, 0)
    batch = tokens // seq
    assert tk % tm == 0 and tm % WINDOW == 0
    sub = tk // tm
    bf = lambda *shape: jax.ShapeDtypeStruct(shape, BF16)
    tok_spec = lambda cols: pl.BlockSpec((tm, cols), row)
    k_spec = pl.BlockSpec((None, KV_HEADS, tm, HEAD_DIM), lambda i: (i // seq_tiles, 0, i % seq_tiles, 0))
    vta_spec = pl.BlockSpec((None, KV_HEADS, None, V_ROWS, tm),
                            lambda i: (i // seq_tiles, 0, (i % seq_tiles) // sub, 0, (i % seq_tiles) % sub))
    vtb_spec = pl.BlockSpec((None, KV_HEADS, tm // WINDOW, V_ROWS, WINDOW),
                            lambda i: (i // seq_tiles, 0, i % seq_tiles, 0, 0))
    k_shape = bf(batch, KV_HEADS, seq, HEAD_DIM)
    return pl.pallas_call(
        functools.partial(_proj_kernel, d_model=d_model),
        grid=(tokens // tm,),
        in_specs=[
            pl.BlockSpec((tm, d_model), row),
            pl.BlockSpec((1, d_model), fixed),
            pl.BlockSpec((d_model, in_cols), fixed),
            pl.BlockSpec((tm, LANES), pos), pl.BlockSpec((tm, LANES), pos),
            pl.BlockSpec((tm, LANES), pos), pl.BlockSpec((tm, LANES), pos),
            pl.BlockSpec((1, LANES), fixed), pl.BlockSpec((1, LANES), fixed),
            pl.BlockSpec((LANES, LANES), fixed),
        ],
        out_specs=[tok_spec(Q_COLS), k_spec, vta_spec, tok_spec(Q_COLS), k_spec, vtb_spec,
                   tok_spec(d_model), tok_spec(d_model)],
        out_shape=[bf(tokens, Q_COLS), k_shape, bf(batch, KV_HEADS, seq // tk, V_ROWS, tk),
                   bf(tokens, Q_COLS), k_shape, bf(batch, KV_HEADS, seq // WINDOW, V_ROWS, WINDOW),
                   bf(tokens, d_model), bf(tokens, d_model)],
        compiler_params=_cparams(("arbitrary",)),
        name="proj",
    )(x2d, g_mix.astype(F32)[None, :], w_in.astype(BF16), cosa, sina, cos1, sin1, qg, kg, ones_blockdiag)


def _stack_group_queries(q):
    return jnp.concatenate([q[:, g * HEAD_DIM:(g + 1) * HEAD_DIM] for g in range(GROUPS)], axis=0)


def _unstack_group_outputs(o_t, tq):
    pad = jnp.zeros((LANES - HEAD_DIM, o_t.shape[1]), o_t.dtype)
    o_pad = jnp.concatenate([o_t, pad], axis=0)
    heads = [o_pad[:, g * tq:(g + 1) * tq].T[:, :HEAD_DIM] for g in range(GROUPS)]
    return jnp.concatenate(heads, axis=1)


def _attn_global_kernel(q_ref, k_ref, vt_ref, o_ref, s0, s1, p0, p1, acc_ref, *, tq, tk, n_chunks):
    q_st = _stack_group_queries(q_ref[...])
    m_rows = q_st.shape[0]

    def scores(c, s_ref):
        k_c = k_ref[pl.ds(pl.multiple_of(c * tk, tk), tk), :]
        s_ref[...] = lax.dot_general(k_c, q_st, (((1,), (1,)), ((), ())), preferred_element_type=F32)

    def softmax(s_ref, p_ref, m_prev):
        m_new = jnp.maximum(m_prev, jnp.max(s_ref[...], axis=0, keepdims=True))
        p_ref[...] = jnp.exp2(s_ref[...] - m_new).astype(BF16)
        return m_new, jnp.exp2(m_prev - m_new)

    def values(c, p_ref, alpha):
        acc_ref[...] = acc_ref[...] * alpha + jnp.dot(vt_ref[c], p_ref[...], preferred_element_type=F32)

    scores(0, s0)
    p1[...] = jnp.zeros(p1.shape, BF16)
    acc_ref[...] = jnp.zeros(acc_ref.shape, F32)

    def pair(j, carry):
        m, alpha = carry
        c = 2 * j
        scores(c + 1, s1)
        m_a, alpha_a = softmax(s0, p0, m)
        values(jnp.maximum(c - 1, 0), p1, alpha)
        scores(jnp.minimum(c + 2, n_chunks - 1), s0)
        m_b, alpha_b = softmax(s1, p1, m_a)
        values(c, p0, alpha_a)
        return m_b, alpha_b

    init = (jnp.full((1, m_rows), NEG_INF, F32), jnp.ones((1, m_rows), F32))
    _, alpha = lax.fori_loop(0, n_chunks // 2, pair, init)
    values(n_chunks - 1, p1, alpha)
    acc = acc_ref[...]
    o_t = acc[:HEAD_DIM, :] / acc[HEAD_DIM:HEAD_DIM + 1, :]
    o_ref[...] = _unstack_group_outputs(o_t, tq).astype(o_ref.dtype)


def _attn_window_kernel(q_ref, k_ref, vt_ref, sink_ref, o_ref, *, tq, n_blocks, blocks_per_step):
    span = 3 * tq
    sink = sink_ref[...]
    for i in range(blocks_per_step):
        n = pl.program_id(2) * blocks_per_step + i
        c0 = jnp.clip(n - 1, 0, n_blocks - 3)
        start = pl.multiple_of(c0 * tq, tq)
        q_st = _stack_group_queries(q_ref[i * tq:(i + 1) * tq, :])
        k_w = k_ref[pl.ds(start, span), :]
        s = lax.dot_general(k_w, q_st, (((1,), (1,)), ((), ())), preferred_element_type=F32)
        kpos = start + lax.broadcasted_iota(jnp.int32, s.shape, 0)
        qpos = n * tq + (lax.broadcasted_iota(jnp.int32, s.shape, 1) & (tq - 1))
        s = jnp.where(jnp.abs(kpos - qpos) <= WINDOW, s, NEG_INF)
        m = jnp.maximum(jnp.max(s, axis=0, keepdims=True), sink)
        p = jnp.exp2(s - m).astype(BF16)
        acc = jnp.dot(vt_ref[c0], p[:tq], preferred_element_type=F32)
        for j in range(1, 3):
            acc += jnp.dot(vt_ref[c0 + j], p[j * tq:(j + 1) * tq], preferred_element_type=F32)
        denom = acc[HEAD_DIM:HEAD_DIM + 1, :] + jnp.exp2(sink - m)
        o_ref[i * tq:(i + 1) * tq, :] = _unstack_group_outputs(acc[:HEAD_DIM, :] / denom, tq).astype(o_ref.dtype)


def _attention_global(q2d, k_hm, v_t, batch, seq, tq):
    tokens = q2d.shape[0]
    n_chunks, tk = v_t.shape[2], v_t.shape[4]
    assert n_chunks % 2 == 0, "the chunk pipeline advances two chunks per step"
    q_tiles = seq // tq
    m_rows = GROUPS * tq
    qmap = lambda b, h, i: (b * q_tiles + i, h)
    return pl.pallas_call(
        functools.partial(_attn_global_kernel, tq=tq, tk=tk, n_chunks=n_chunks),
        grid=(batch, KV_HEADS, q_tiles),
        in_specs=[
            pl.BlockSpec((tq, GROUP_COLS), qmap),
            pl.BlockSpec((None, None, seq, HEAD_DIM), lambda b, h, i: (b, h, 0, 0)),
            pl.BlockSpec((None, None, n_chunks, V_ROWS, tk), lambda b, h, i: (b, h, 0, 0, 0)),
        ],
        out_specs=pl.BlockSpec((tq, GROUP_COLS), qmap),
        out_shape=jax.ShapeDtypeStruct((tokens, Q_COLS), BF16),
        scratch_shapes=[pltpu.VMEM((tk, m_rows), F32), pltpu.VMEM((tk, m_rows), F32),
                        pltpu.VMEM((tk, m_rows), BF16), pltpu.VMEM((tk, m_rows), BF16),
                        pltpu.VMEM((V_ROWS, m_rows), F32)],
        compiler_params=_cparams(("arbitrary", "arbitrary", "arbitrary")),
        name="attn_global",
    )(q2d, k_hm, v_t)


def _attention_window(q2d, k_hm, v_t, sink, batch, seq):
    tokens = q2d.shape[0]
    tq = WINDOW
    n_blocks = seq // tq
    assert n_blocks >= 3
    m_rows = GROUPS * tq
    sink_rows = jnp.repeat(sink.astype(F32).reshape(KV_HEADS, GROUPS) * LOG2_E, tq, axis=1)[:, None, :]
    per_step = WINDOW_BLOCKS_PER_STEP if n_blocks % WINDOW_BLOCKS_PER_STEP == 0 else 1
    steps = n_blocks // per_step
    qmap = lambda b, h, i: (b * steps + i, h)
    return pl.pallas_call(
        functools.partial(_attn_window_kernel, tq=tq, n_blocks=n_blocks, blocks_per_step=per_step),
        grid=(batch, KV_HEADS, steps),
        in_specs=[
            pl.BlockSpec((per_step * tq, GROUP_COLS), qmap),
            pl.BlockSpec((None, None, seq, HEAD_DIM), lambda b, h, i: (b, h, 0, 0)),
            pl.BlockSpec((None, None, n_blocks, V_ROWS, tq), lambda b, h, i: (b, h, 0, 0, 0)),
            pl.BlockSpec((None, 1, m_rows), lambda b, h, i: (h, 0, 0)),
        ],
        out_specs=pl.BlockSpec((per_step * tq, GROUP_COLS), qmap),
        out_shape=jax.ShapeDtypeStruct((tokens, Q_COLS), BF16),
        compiler_params=_cparams(("arbitrary", "arbitrary", "arbitrary")),
        name="attn_window",
    )(q2d, k_hm, v_t, sink_rows)


def _store_token_tiles(ref, val):
    rows = val.shape[0]
    for j in range(val.shape[1] // LANES):
        ref[pl.ds(j, rows, stride=SUBLANES), :] = val[:, j * LANES:(j + 1) * LANES]


def _load_token_tiles(ref, rows):
    return jnp.concatenate([ref[pl.ds(j, rows, stride=SUBLANES), :] for j in range(SUBLANES)], axis=1)


ROUTE_ROWS = 2 * SUBLANES


def _top_k_route(logits_t):
    eid = lax.broadcasted_iota(jnp.int32, logits_t.shape, 0)
    work = logits_t
    vals, idxs, sels = [], [], []
    for _ in range(TOP_K):
        mx = jnp.max(work, axis=0, keepdims=True)
        ix = jnp.min(jnp.where(work == mx, eid, N_EXPERTS), axis=0, keepdims=True)
        sel = eid == ix
        work = jnp.where(sel, -jnp.inf, work)
        vals.append(mx)
        idxs.append(ix)
        sels.append(sel)
    return vals, idxs, sels


def _mid_kernel(oa_ref, ob_ref, sga_ref, sgb_ref, x_ref, wa_ref, wb_ref, wo_ref, g_ref, wr_ref, br_ref, tri_ref,
                x1_ref, h2_ref, route_ref, gate_ref, cnt_ref, *, halves):
    @pl.when(pl.program_id(0) == 0)
    def _():
        cnt_ref[...] = jnp.zeros(cnt_ref.shape, F32)

    tm = x_ref.shape[0]
    hm = tm // halves
    routed = []
    for hf in range(halves):
        rs = slice(hf * hm, (hf + 1) * hm)
        a = jnp.dot(oa_ref[rs, :], wa_ref[...], preferred_element_type=F32)
        b = jnp.dot(ob_ref[rs, :], wb_ref[...], preferred_element_type=F32)
        merged = sga_ref[rs, :].astype(F32) * a + sgb_ref[rs, :].astype(F32) * b
        x1 = x_ref[rs, :] + jnp.dot(merged.astype(BF16), wo_ref[...], preferred_element_type=F32)
        x1_ref[rs, :] = x1
        ms = jnp.mean(x1 * x1, axis=-1, keepdims=True)
        h2 = x1 * lax.rsqrt(ms + EPS) * g_ref[...]
        _store_token_tiles(h2_ref.at[pl.ds(hf * hm * SUBLANES, hm * SUBLANES)], h2)
        hi = h2.astype(BF16)
        lo = (h2 - hi.astype(F32)).astype(BF16)
        nt = (((1,), (1,)), ((), ()))
        r2 = (lax.dot_general(wr_ref[...], hi, nt, preferred_element_type=F32)
              + lax.dot_general(wr_ref[...], lo, nt, preferred_element_type=F32))
        bias = jnp.concatenate([br_ref[...]] * (hm // LANES), axis=1)
        routed.append(_top_k_route(r2[:N_EXPERTS] + r2[N_EXPERTS:] + bias))

    onehots = []
    for vals, idxs, sels in routed:
        chosen = sels[0] | sels[1] | sels[2] | sels[3]
        onehots.append(jnp.where(chosen, 1.0, 0.0))
    onehot = jnp.concatenate(onehots, axis=1)
    counted = jnp.concatenate([cnt_ref[...]] * (tm // LANES), axis=1)
    before = jnp.dot(onehot.astype(BF16), tri_ref[...], preferred_element_type=F32) + counted
    zeros4 = jnp.zeros((SUBLANES - TOP_K, hm), jnp.int32)
    for hf, (vals, idxs, sels) in enumerate(routed):
        rs = slice(hf * hm, (hf + 1) * hm)
        exps = [jnp.exp(v - vals[0]) for v in vals]
        denom = exps[0] + exps[1] + exps[2] + exps[3]
        ranks = [jnp.sum(jnp.where(sels[k], before[:, rs], 0.0), axis=0, keepdims=True).astype(jnp.int32)
                 for k in range(TOP_K)]
        route_ref[:, rs] = jnp.concatenate(idxs + [zeros4] + ranks + [zeros4], axis=0)
        gates_t = jnp.concatenate([e / denom for e in exps] + [jnp.zeros((LANES - TOP_K, hm), F32)], axis=0)
        gate_ref[rs, :] = gates_t.T
    cnt_ref[...] = cnt_ref[...] + jnp.sum(onehot, axis=1, keepdims=True)


def _mid(oa, ob, sga, sgb, x2d, w_a, w_b, w_o, g_ffn, w_router, b_router, tm):
    tokens, d_model = x2d.shape
    wr_t = w_router.astype(F32).T
    wr_hi = wr_t.astype(BF16)
    wr_lo = (wr_t - wr_hi.astype(F32)).astype(BF16)
    wr_split = jnp.concatenate([wr_hi, wr_lo], axis=0)
    br = jnp.broadcast_to(b_router.astype(F32)[:, None], (N_EXPERTS, LANES))
    tri = (jnp.arange(tm)[:, None] < jnp.arange(tm)[None, :]).astype(BF16)
    halves = 2 if tm % (2 * LANES) == 0 else 1
    row = lambda i: (i, 0)
    fixed = lambda i: (0, 0)
    return pl.pallas_call(
        functools.partial(_mid_kernel, halves=halves),
        grid=(tokens // tm,),
        in_specs=[
            pl.BlockSpec((tm, Q_COLS), row), pl.BlockSpec((tm, Q_COLS), row),
            pl.BlockSpec((tm, d_model), row), pl.BlockSpec((tm, d_model), row),
            pl.BlockSpec((tm, d_model), row),
            pl.BlockSpec((Q_COLS, d_model), fixed), pl.BlockSpec((Q_COLS, d_model), fixed),
            pl.BlockSpec((d_model, d_model), fixed),
            pl.BlockSpec((1, d_model), fixed),
            pl.BlockSpec((2 * N_EXPERTS, d_model), fixed),
            pl.BlockSpec((N_EXPERTS, LANES), fixed),
            pl.BlockSpec((tm, tm), fixed),
        ],
        out_specs=[
            pl.BlockSpec((tm, d_model), row),
            pl.BlockSpec((tm * SUBLANES, LANES), row),
            pl.BlockSpec((ROUTE_ROWS, tm), lambda i: (0, i)),
            pl.BlockSpec((tm, LANES), row),
            pl.BlockSpec((N_EXPERTS, LANES), fixed),
        ],
        out_shape=[
            jax.ShapeDtypeStruct((tokens, d_model), F32),
            jax.ShapeDtypeStruct((tokens * SUBLANES, LANES), F32),
            jax.ShapeDtypeStruct((ROUTE_ROWS, tokens), jnp.int32),
            jax.ShapeDtypeStruct((tokens, LANES), F32),
            jax.ShapeDtypeStruct((N_EXPERTS, LANES), F32),
        ],
        compiler_params=_cparams(("arbitrary",)),
        name="mid",
    )(oa, ob, sga, sgb, x2d, w_a.astype(BF16), w_b.astype(BF16), w_o.astype(BF16),
      g_ffn.astype(F32)[None, :], wr_split, br, tri)


def _invert_kernel(gap_lo_ref, gap_hi_ref, dest_ref, out_ref, *, tokens):
    def clear(r, carry):
        out_ref[r] = -1
        return carry
    for g in range(gap_lo_ref.shape[0]):
        lax.fori_loop(gap_lo_ref[g], gap_hi_ref[g], clear, 0)
    for k in range(TOP_K):
        def put(t, carry):
            out_ref[dest_ref[k * tokens + t]] = t * TOP_K + k
            return carry
        lax.fori_loop(0, tokens, put, 0, unroll=16)


def _invert_routes(dest_flat, gap_lo, gap_hi, n_rows, tokens):
    smem = pl.BlockSpec(memory_space=pltpu.SMEM)
    return pl.pallas_call(
        functools.partial(_invert_kernel, tokens=tokens),
        in_specs=[smem, smem, smem],
        out_specs=smem,
        out_shape=jax.ShapeDtypeStruct((n_rows,), jnp.int32),
        name="invert_routes",
    )(gap_lo, gap_hi, dest_flat)


def _tile_copy(src_ref, src_row, dst_ref, dst_row, sem):
    return pltpu.make_async_copy(
        src_ref.at[pl.ds(pl.multiple_of(src_row, SUBLANES), SUBLANES)],
        dst_ref.at[pl.ds(pl.multiple_of(dst_row, SUBLANES), SUBLANES)], sem)


def _start_tile_gather(off_smem, base, n_rows, src_ref, dst_ref, sem):
    def start(r, carry):
        _tile_copy(src_ref, off_smem[base + r], dst_ref, r * SUBLANES, sem).start()
        return carry
    lax.fori_loop(0, n_rows, start, 0, unroll=8)


def _start_tile_scatter(off_smem, base, n_rows, src_ref, dst_ref, sem):
    def start(r, carry):
        _tile_copy(src_ref, r * SUBLANES, dst_ref, off_smem[base + r], sem).start()
        return carry
    lax.fori_loop(0, n_rows, start, 0, unroll=8)


def _wait_tiles(n_rows, src_ref, dst_ref, sem):
    window = pl.ds(0, n_rows * SUBLANES)
    pltpu.make_async_copy(src_ref.at[window], dst_ref.at[window], sem).wait()


SRC_WINDOWS = 3
DST_WINDOWS = 5
MOE_BUFFERS = 3
MOE_COL_TILES = 4
DMA_THREADS = 2


def _moe_kernel(bexp_ref, nused_ref, src_ref, dst_ref, h2_ref, wgu_ref, bgu_ref, wd_ref, bd_ref,
                y_ref, src_smem, dst_smem, xbuf, ybuf, wgu_bf, wd_bf, isem, gsem, ssem, *, rows, n_assign):
    i = pl.program_id(0)
    n_used = nused_ref[0]
    cur = lax.rem(i, MOE_BUFFERS)
    far = lax.rem(i + 2, MOE_BUFFERS)
    src_base = lambda blk: lax.rem(blk, SRC_WINDOWS) * rows
    dst_base = lambda blk: lax.rem(blk, DST_WINDOWS) * rows

    def idx_copies(blk):
        sw = pl.ds(pl.multiple_of(src_base(blk), rows), rows)
        dw = pl.ds(pl.multiple_of(dst_base(blk), rows), rows)
        return (pltpu.make_async_copy(src_ref.at[blk], src_smem.at[sw], isem.at[0, lax.rem(blk, SRC_WINDOWS)]),
                pltpu.make_async_copy(dst_ref.at[blk], dst_smem.at[dw], isem.at[1, lax.rem(blk, DST_WINDOWS)]))

    def fetch_idx_now(blk):
        for c in idx_copies(blk):
            c.start()
        for c in idx_copies(blk):
            c.wait()

    def gather_copy(blk_base, r, s):
        return _tile_copy(h2_ref, src_smem[blk_base + r], xbuf.at[s], r * SUBLANES, gsem.at[s])

    def scatter_copy(blk_base, r, s):
        return _tile_copy(ybuf.at[s], r * SUBLANES, y_ref, dst_smem[blk_base + r], ssem.at[s])

    def wait_scatter(s):
        _wait_tiles(rows, ybuf.at[s], y_ref, ssem.at[s])

    def swiglu(g, u):
        gate = jnp.minimum(g, SWIGLU_LIMIT)
        up = jnp.clip(u, -SWIGLU_LIMIT, SWIGLU_LIMIT)
        return ((up + 1.0) * (gate * jax.nn.sigmoid(SWIGLU_ALPHA * gate))).astype(BF16)

    @pl.when(i == 0)
    def _():
        ybuf[...] = jnp.zeros(ybuf.shape, F32)
        for s in range(2):
            dump = pl.ds((n_assign + s * rows) * SUBLANES, rows * SUBLANES)
            zero_fill = pltpu.make_async_copy(ybuf.at[s], y_ref.at[dump], ssem.at[s])
            zero_fill.start()
            zero_fill.wait()
        fetch_idx_now(0)
        _start_tile_gather(src_smem, 0, rows, h2_ref, xbuf.at[0], gsem.at[0])

        @pl.when(1 < n_used)
        def _():
            fetch_idx_now(1)
            _start_tile_gather(src_smem, src_base(1), rows, h2_ref, xbuf.at[1], gsem.at[1])

        @pl.when(2 < n_used)
        def _():
            fetch_idx_now(2)

    @pl.when(i + 3 < n_used)
    def _():
        for c in idx_copies(i + 3):
            c.start()

    @pl.when(i < n_used)
    def _():
        _wait_tiles(rows, h2_ref, xbuf.at[cur], gsem.at[cur])

    @pl.when(jnp.logical_and(i >= 3, i <= n_used))
    def _():
        wait_scatter(cur)

    new_expert = jnp.logical_or(i == 0, bexp_ref[i] != bexp_ref[jnp.maximum(i - 1, 0)])

    @pl.when(jnp.logical_and(new_expert, i < n_used))
    def _():
        wgu_bf[...] = wgu_ref[...].astype(BF16)
        wd_bf[...] = wd_ref[...].astype(BF16)

    steady = jnp.logical_and(i >= 1, i + 2 < n_used)

    @pl.when(steady)
    def _():
        gbase, sbase = src_base(i + 2), dst_base(i - 1)
        burst = rows // MOE_COL_TILES
        width = D_FF // MOE_COL_TILES
        x = _load_token_tiles(xbuf.at[cur], rows).astype(BF16)
        acts = []
        for t in range(MOE_COL_TILES):
            cg = slice(t * width, (t + 1) * width)
            cu = slice(D_FF + t * width, D_FF + (t + 1) * width)
            g = jnp.dot(x, wgu_bf[:, cg], preferred_element_type=F32) + bgu_ref[:, cg]
            u = jnp.dot(x, wgu_bf[:, cu], preferred_element_type=F32) + bgu_ref[:, cu]
            for r in range(t * burst, (t + 1) * burst):
                gather_copy(gbase, r, far).start(priority=r % DMA_THREADS)
            acts.append(swiglu(g, u))
        act = jnp.concatenate(acts, axis=1)
        out_width = wd_bf.shape[1] // MOE_COL_TILES
        for t in range(MOE_COL_TILES):
            co = slice(t * out_width, (t + 1) * out_width)
            y_t = jnp.dot(act, wd_bf[:, co], preferred_element_type=F32) + bd_ref[:, co]
            for r in range(t * burst, (t + 1) * burst):
                scatter_copy(sbase, r, far).start(priority=r % DMA_THREADS)
            for j in range(out_width // LANES):
                ybuf[cur, pl.ds(t * (out_width // LANES) + j, rows, stride=SUBLANES), :] = (
                    y_t[:, j * LANES:(j + 1) * LANES])

    @pl.when(jnp.logical_and(i < n_used, jnp.logical_not(steady)))
    def _():
        @pl.when(i + 2 < n_used)
        def _():
            _start_tile_gather(src_smem, src_base(i + 2), rows, h2_ref, xbuf.at[far], gsem.at[far])

        x = _load_token_tiles(xbuf.at[cur], rows).astype(BF16)
        gu = jnp.dot(x, wgu_bf[...], preferred_element_type=F32) + bgu_ref[...]
        act = swiglu(gu[:, :D_FF], gu[:, D_FF:])
        y = jnp.dot(act, wd_bf[...], preferred_element_type=F32) + bd_ref[...]
        _store_token_tiles(ybuf.at[cur], y)

        @pl.when(i >= 1)
        def _():
            _start_tile_scatter(dst_smem, dst_base(i - 1), rows, ybuf.at[far], y_ref, ssem.at[far])

    @pl.when(i == n_used)
    def _():
        _start_tile_scatter(dst_smem, dst_base(i - 1), rows, ybuf.at[far], y_ref, ssem.at[far])

        @pl.when(i >= 2)
        def _():
            wait_scatter(lax.rem(i + 1, MOE_BUFFERS))

        wait_scatter(far)

    @pl.when(i + 3 < n_used)
    def _():
        for c in idx_copies(i + 3):
            c.wait()


def _moe(block_exp, n_used, src_rows, dst_rows, h2_tiles, w_gate_up, b_gate_up, w_down, b_down, rows, n_assign):
    n_blocks = src_rows.shape[0]
    n_exp, d_model, ff2 = w_gate_up.shape
    grid_spec = pltpu.PrefetchScalarGridSpec(
        num_scalar_prefetch=2,
        grid=(n_blocks,),
        in_specs=[
            pl.BlockSpec(memory_space=pl.ANY),
            pl.BlockSpec(memory_space=pl.ANY),
            pl.BlockSpec(memory_space=pl.ANY),
            pl.BlockSpec((None, d_model, ff2), lambda i, be, nu: (be[i], 0, 0)),
            pl.BlockSpec((None, 1, ff2), lambda i, be, nu: (be[i], 0, 0)),
            pl.BlockSpec((None, ff2 // 2, d_model), lambda i, be, nu: (be[i], 0, 0)),
            pl.BlockSpec((None, 1, d_model), lambda i, be, nu: (be[i], 0, 0)),
        ],
        out_specs=pl.BlockSpec(memory_space=pl.ANY),
        scratch_shapes=[
            pltpu.SMEM((SRC_WINDOWS * rows,), jnp.int32),
            pltpu.SMEM((DST_WINDOWS * rows,), jnp.int32),
            pltpu.VMEM((MOE_BUFFERS, rows * SUBLANES, LANES), F32),
            pltpu.VMEM((MOE_BUFFERS, rows * SUBLANES, LANES), F32),
            pltpu.VMEM((d_model, ff2), BF16),
            pltpu.VMEM((ff2 // 2, d_model), BF16),
            pltpu.SemaphoreType.DMA((2, DST_WINDOWS)),
            pltpu.SemaphoreType.DMA((MOE_BUFFERS,)),
            pltpu.SemaphoreType.DMA((MOE_BUFFERS,)),
        ],
    )
    return pl.pallas_call(
        functools.partial(_moe_kernel, rows=rows, n_assign=n_assign),
        grid_spec=grid_spec,
        out_shape=jax.ShapeDtypeStruct(((n_assign + 2 * rows) * SUBLANES, LANES), F32),
        compiler_params=_cparams(("arbitrary",)),
        name="moe",
    )(block_exp, n_used, src_rows, dst_rows, h2_tiles, w_gate_up, b_gate_up[:, None, :], w_down,
      b_down[:, None, :])


def _combine_kernel(y_ref, gate_ref, x1_ref, g_ref, o_ref, *, tm):
    gate = gate_ref[...]
    out = x1_ref[...]
    stride = TOP_K * SUBLANES
    for k in range(TOP_K):
        y_k = jnp.concatenate(
            [y_ref[pl.ds(k * SUBLANES + j, tm, stride=stride), :] for j in range(SUBLANES)], axis=1)
        out = out + gate[:, k:k + 1] * y_k
    ms = jnp.mean(out * out, axis=-1, keepdims=True)
    o_ref[...] = out * lax.rsqrt(ms + EPS) * g_ref[...]


def _combine(y_tiles, gate, x1, g_final, tm):
    tokens, d_model = x1.shape
    row = lambda i: (i, 0)
    return pl.pallas_call(
        functools.partial(_combine_kernel, tm=tm),
        grid=(tokens // tm,),
        in_specs=[
            pl.BlockSpec((tm * TOP_K * SUBLANES, LANES), row),
            pl.BlockSpec((tm, LANES), row),
            pl.BlockSpec((tm, d_model), row),
            pl.BlockSpec((1, d_model), lambda i: (0, 0)),
        ],
        out_specs=pl.BlockSpec((tm, d_model), row),
        out_shape=jax.ShapeDtypeStruct((tokens, d_model), F32),
        compiler_params=_cparams(("arbitrary",)),
        name="combine",
    )(y_tiles, gate, x1, g_final.astype(F32)[None, :])


PROJ_ROWS = 256
MID_ROWS = 512
ATTN_Q_ROWS = 1024
ATTN_K_ROWS = 512
WINDOW_BLOCKS_PER_STEP = 4
MOE_ROWS = 256
COMBINE_ROWS = 256


def _layer(x2d, batch, seq, g_mix, w_in, qa_norm, ka_norm, sink_b, w_branch_a, w_branch_b, w_out,
           g_ffn, w_router, b_router, w_gate_up, b_gate_up, w_down, b_down, g_final):
    tokens = x2d.shape[0]
    qa, ka, vta, qb, kb, vtb, sga, sgb = _project(x2d, g_mix, w_in, qa_norm, ka_norm, seq,
                                                  min(PROJ_ROWS, seq), min(ATTN_K_ROWS, seq // 2))
    oa = _attention_global(qa, ka, vta, batch, seq, ATTN_Q_ROWS)
    ob = _attention_window(qb, kb, vtb, sink_b, batch, seq)
    x1, h2_tiles, route, gate, cnt = _mid(
        oa, ob, sga, sgb, x2d, w_branch_a, w_branch_b, w_out, g_ffn, w_router, b_router, min(MID_ROWS, tokens))

    rows = MOE_ROWS
    n_assign = tokens * TOP_K
    n_blocks = n_assign // rows + N_EXPERTS
    counts = cnt[:, 0].astype(jnp.int32)
    padded = (counts + rows - 1) // rows * rows
    ends = jnp.cumsum(padded)
    starts = ends - padded
    eidx, rank = route[:TOP_K], route[SUBLANES:SUBLANES + TOP_K]
    seg_start = jnp.sum(jnp.where(eidx[None] == jnp.arange(N_EXPERTS)[:, None, None], starts[:, None, None], 0),
                        axis=0)
    dest = seg_start + rank
    gap_lo = jnp.concatenate([starts + counts, ends[-1:]]).astype(jnp.int32)
    gap_hi = jnp.concatenate([ends, jnp.full((1,), n_blocks * rows)]).astype(jnp.int32)
    row_assign = _invert_routes(dest.reshape(-1), gap_lo, gap_hi, n_blocks * rows, tokens).reshape(n_blocks, rows)
    real = row_assign >= 0
    dump_slot = (n_assign + (jnp.arange(n_blocks, dtype=jnp.int32) % 2)[:, None] * rows
                 + jnp.arange(rows, dtype=jnp.int32)[None, :])
    src_rows = jnp.where(real, row_assign // TOP_K, 0) * SUBLANES
    dst_rows = jnp.where(real, row_assign, dump_slot) * SUBLANES
    block_start = jnp.arange(n_blocks, dtype=jnp.int32) * rows
    block_exp = jnp.minimum(
        jnp.sum((ends[None, :] <= block_start[:, None]).astype(jnp.int32), axis=1), N_EXPERTS - 1)
    n_used = (ends[-1:] // rows).astype(jnp.int32)

    y_tiles = _moe(block_exp, n_used, src_rows, dst_rows, h2_tiles,
                   w_gate_up, b_gate_up, w_down, b_down, rows, n_assign)
    return _combine(y_tiles, gate, x1, g_final, min(COMBINE_ROWS, tokens))


def kernel(x, g_mix, w_in, qa_norm, ka_norm, sink_b, w_branch_a, w_branch_b, w_out, g_ffn, w_router, b_router,
           w_gate_up, b_gate_up, w_down, b_down, g_final):
    batch, seq, d_model = x.shape
    assert g_mix.shape[0] == 1, "single-layer block"
    out = _layer(x.reshape(batch * seq, d_model), batch, seq, g_mix[0], w_in[0], qa_norm[0], ka_norm[0],
                 sink_b[0], w_branch_a[0], w_branch_b[0], w_out[0], g_ffn[0], w_router[0], b_router[0],
                 w_gate_up[0], b_gate_up[0], w_down[0], b_down[0], g_final)
    return out.reshape(batch, seq, d_model)
```

```python
import functools

import jax
import jax.numpy as jnp
from jax import lax
from jax.experimental import pallas as pl
from jax.experimental.pallas import tpu as pltpu

HEAD_DIM = 64
Q_HEADS = 8
KV_HEADS = 2
GROUPS = Q_HEADS // KV_HEADS
Q_COLS = Q_HEADS * HEAD_DIM
KV_COLS = KV_HEADS * HEAD_DIM
GROUP_COLS = GROUPS * HEAD_DIM
WINDOW = 128
GRID_W = 64
ROPE_THETA = 10000.0
N_EXPERTS = 32
TOP_K = 4
D_FF = 1024
SWIGLU_LIMIT = 7.0
SWIGLU_ALPHA = 1.702
EPS = 1e-6
NEG_INF = -1e30
LOG2_E = 1.4426950408889634

LANES = 128
SUBLANES = 8
V_ROWS = 80
VMEM_LIMIT = 56 * 1024 * 1024

F32 = jnp.float32
BF16 = jnp.bfloat16


def _cparams(sem, vmem=VMEM_LIMIT):
    return pltpu.CompilerParams(dimension_semantics=sem, vmem_limit_bytes=vmem)


def _rope(y, cos, sin_signed):
    lane = lax.broadcasted_iota(jnp.int32, y.shape, 1)
    first_half = (lane & 32) == 0
    rot = jnp.where(first_half, pltpu.roll(y, 96, 1), pltpu.roll(y, 32, 1))
    return y * cos + rot * sin_signed


def _head_rmsnorm(q, ones_blockdiag, gain):
    q2 = q * q
    hi = q2.astype(BF16)
    lo = (q2 - hi.astype(F32)).astype(BF16)
    ss = (jnp.dot(hi, ones_blockdiag, preferred_element_type=F32)
          + jnp.dot(lo, ones_blockdiag, preferred_element_type=F32))
    return q * lax.rsqrt(ss * (1.0 / HEAD_DIM) + EPS) * gain


def _proj_kernel(x_ref, g_ref, w_ref, cosa_ref, sina_ref, cos1_ref, sin1_ref, qg_ref, kg_ref, bd_ref,
                 qa_ref, ka_ref, vta_ref, qb_ref, kb_ref, vtb_ref, sga_ref, sgb_ref, *, d_model):
    x = x_ref[...]
    ms = jnp.mean(x * x, axis=-1, keepdims=True)
    h = (x * lax.rsqrt(ms + EPS) * g_ref[...]).astype(BF16)
    scale = HEAD_DIM ** -0.5 * LOG2_E

    def proj(c0, width):
        return jnp.dot(h, w_ref[:, c0:c0 + width], preferred_element_type=F32)

    cosa, sina = cosa_ref[...], sina_ref[...]
    cos1, sin1 = cos1_ref[...], sin1_ref[...]
    bd = bd_ref[...]
    slab = lambda v, s: v[:, s * LANES:(s + 1) * LANES]
    c = 0
    qa = proj(c, Q_COLS)
    for s in range(Q_COLS // LANES):
        q = _head_rmsnorm(slab(qa, s), bd, qg_ref[...])
        qa_ref[:, s * LANES:(s + 1) * LANES] = (_rope(q, cosa, sina) * scale).astype(BF16)
    c += Q_COLS
    kva = proj(c, 2 * KV_COLS)
    ka = _rope(_head_rmsnorm(slab(kva, 0), bd, kg_ref[...]), cosa, sina).astype(BF16)
    va_t = slab(kva, 1).T
    ones = jnp.ones((V_ROWS - HEAD_DIM, va_t.shape[1]), BF16)
    for hd in range(KV_HEADS):
        ka_ref[hd] = ka[:, hd * HEAD_DIM:(hd + 1) * HEAD_DIM]
        vta_ref[hd, :HEAD_DIM, :] = va_t[hd * HEAD_DIM:(hd + 1) * HEAD_DIM, :].astype(BF16)
        vta_ref[hd, HEAD_DIM:, :] = ones
    c += 2 * KV_COLS
    qb = proj(c, Q_COLS)
    for s in range(Q_COLS // LANES):
        qb_ref[:, s * LANES:(s + 1) * LANES] = (_rope(slab(qb, s), cos1, sin1) * scale).astype(BF16)
    c += Q_COLS
    kvb = proj(c, 2 * KV_COLS)
    kb = _rope(slab(kvb, 0), cos1, sin1).astype(BF16)
    vb_t = slab(kvb, 1).T
    ones = jnp.ones((V_ROWS - HEAD_DIM, WINDOW), BF16)
    for hd in range(KV_HEADS):
        kb_ref[hd] = kb[:, hd * HEAD_DIM:(hd + 1) * HEAD_DIM]
        for ch in range(vtb_ref.shape[1]):
            vtb_ref[hd, ch, :HEAD_DIM, :] = vb_t[hd * HEAD_DIM:(hd + 1) * HEAD_DIM,
                                                 ch * WINDOW:(ch + 1) * WINDOW].astype(BF16)
            vtb_ref[hd, ch, HEAD_DIM:, :] = ones
    c += 2 * KV_COLS
    sga_ref[...] = jax.nn.sigmoid(proj(c, d_model)).astype(BF16)
    c += d_model
    sgb_ref[...] = jax.nn.sigmoid(proj(c, d_model)).astype(BF16)


def _rope_tables(seq):
    def slab(ang):
        cos, sin = jnp.cos(ang), jnp.sin(ang)
        cos64 = jnp.concatenate([cos, cos], axis=-1)
        sin64 = jnp.concatenate([-sin, sin], axis=-1)
        return jnp.tile(cos64, (1, 2)), jnp.tile(sin64, (1, 2))

    pos = jnp.arange(seq, dtype=F32)
    inv1 = ROPE_THETA ** (-jnp.arange(0, HEAD_DIM, 2, dtype=F32) / HEAD_DIM)
    ang1 = pos[:, None] * inv1[None, :]
    rows = seq // GRID_W
    row = jnp.broadcast_to(jnp.arange(rows, dtype=F32)[:, None], (rows, GRID_W)).reshape(-1)
    col = jnp.broadcast_to(jnp.arange(GRID_W, dtype=F32)[None, :], (rows, GRID_W)).reshape(-1)
    axis_dim = HEAD_DIM // 2
    inva = ROPE_THETA ** (-jnp.arange(0, axis_dim, 2, dtype=F32) / axis_dim)
    anga = jnp.concatenate([row[:, None] * inva[None, :], col[:, None] * inva[None, :]], axis=-1)
    return slab(anga), slab(ang1)


def _project(x2d, g_mix, w_in, qa_norm, ka_norm, seq, tm, tk):
    tokens, d_model = x2d.shape
    in_cols = w_in.shape[1]
    (cosa, sina), (cos1, sin1) = _rope_tables(seq)
    head_id = jnp.arange(LANES) // HEAD_DIM
    ones_blockdiag = (head_id[:, None] == head_id[None, :]).astype(BF16)
    qg = jnp.tile(qa_norm.astype(F32), 2)[None, :]
    kg = jnp.tile(ka_norm.astype(F32), 2)[None, :]
    seq_tiles = seq // tm
    row = lambda i: (i, 0)
    pos = lambda i: (i % seq_tiles, 0)
    fixed = lambda i: (0, 0)
    batch = tokens // seq
    assert tk % tm == 0 and tm % WINDOW == 0
    sub = tk // tm
    bf = lambda *shape: jax.ShapeDtypeStruct(shape, BF16)
    tok_spec = lambda cols: pl.BlockSpec((tm, cols), row)
    k_spec = pl.BlockSpec((None, KV_HEADS, tm, HEAD_DIM), lambda i: (i // seq_tiles, 0, i % seq_tiles, 0))
    vta_spec = pl.BlockSpec((None, KV_HEADS, None, V_ROWS, tm),
                            lambda i: (i // seq_tiles, 0, (i % seq_tiles) // sub, 0, (i % seq_tiles) % sub))
    vtb_spec = pl.BlockSpec((None, KV_HEADS, tm // WINDOW, V_ROWS, WINDOW),
                            lambda i: (i // seq_tiles, 0, i % seq_tiles, 0, 0))
    k_shape = bf(batch, KV_HEADS, seq, HEAD_DIM)
    return pl.pallas_call(
        functools.partial(_proj_kernel, d_model=d_model),
        grid=(tokens // tm,),
        in_specs=[
            pl.BlockSpec((tm, d_model), row),
            pl.BlockSpec((1, d_model), fixed),
            pl.BlockSpec((d_model, in_cols), fixed),
            pl.BlockSpec((tm, LANES), pos), pl.BlockSpec((tm, LANES), pos),
            pl.BlockSpec((tm, LANES), pos), pl.BlockSpec((tm, LANES), pos),
            pl.BlockSpec((1, LANES), fixed), pl.BlockSpec((1, LANES), fixed),
            pl.BlockSpec((LANES, LANES), fixed),
        ],
        out_specs=[tok_spec(Q_COLS), k_spec, vta_spec, tok_spec(Q_COLS), k_spec, vtb_spec,
                   tok_spec(d_model), tok_spec(d_model)],
        out_shape=[bf(tokens, Q_COLS), k_shape, bf(batch, KV_HEADS, seq // tk, V_ROWS, tk),
                   bf(tokens, Q_COLS), k_shape, bf(batch, KV_HEADS, seq // WINDOW, V_ROWS, WINDOW),
                   bf(tokens, d_model), bf(tokens, d_model)],
        compiler_params=_cparams(("arbitrary",)),
        name="proj",
    )(x2d, g_mix.astype(F32)[None, :], w_in.astype(BF16), cosa, sina, cos1, sin1, qg, kg, ones_blockdiag)


def _stack_group_queries(q):
    return jnp.concatenate([q[:, g * HEAD_DIM:(g + 1) * HEAD_DIM] for g in range(GROUPS)], axis=0)


def _unstack_group_outputs(o_t, tq):
    pad = jnp.zeros((LANES - HEAD_DIM, o_t.shape[1]), o_t.dtype)
    o_pad = jnp.concatenate([o_t, pad], axis=0)
    heads = [o_pad[:, g * tq:(g + 1) * tq].T[:, :HEAD_DIM] for g in range(GROUPS)]
    return jnp.concatenate(heads, axis=1)


def _attn_global_kernel(q_ref, k_ref, vt_ref, o_ref, s0, s1, p0, p1, acc_ref, *, tq, tk, n_chunks):
    q_st = _stack_group_queries(q_ref[...])
    m_rows = q_st.shape[0]

    def scores(c, s_ref):
        k_c = k_ref[pl.ds(pl.multiple_of(c * tk, tk), tk), :]
        s_ref[...] = lax.dot_general(k_c, q_st, (((1,), (1,)), ((), ())), preferred_element_type=F32)

    def softmax(s_ref, p_ref, m_prev):
        m_new = jnp.maximum(m_prev, jnp.max(s_ref[...], axis=0, keepdims=True))
        p_ref[...] = jnp.exp2(s_ref[...] - m_new).astype(BF16)
        return m_new, jnp.exp2(m_prev - m_new)

    def values(c, p_ref, alpha):
        acc_ref[...] = acc_ref[...] * alpha + jnp.dot(vt_ref[c], p_ref[...], preferred_element_type=F32)

    scores(0, s0)
    p1[...] = jnp.zeros(p1.shape, BF16)
    acc_ref[...] = jnp.zeros(acc_ref.shape, F32)

    def pair(j, carry):
        m, alpha = carry
        c = 2 * j
        scores(c + 1, s1)
        m_a, alpha_a = softmax(s0, p0, m)
        values(jnp.maximum(c - 1, 0), p1, alpha)
        scores(jnp.minimum(c + 2, n_chunks - 1), s0)
        m_b, alpha_b = softmax(s1, p1, m_a)
        values(c, p0, alpha_a)
        return m_b, alpha_b

    init = (jnp.full((1, m_rows), NEG_INF, F32), jnp.ones((1, m_rows), F32))
    _, alpha = lax.fori_loop(0, n_chunks // 2, pair, init)
    values(n_chunks - 1, p1, alpha)
    acc = acc_ref[...]
    o_t = acc[:HEAD_DIM, :] / acc[HEAD_DIM:HEAD_DIM + 1, :]
    o_ref[...] = _unstack_group_outputs(o_t, tq).astype(o_ref.dtype)


def _attn_window_kernel(q_ref, k_ref, vt_ref, sink_ref, o_ref, *, tq, n_blocks, blocks_per_step):
    span = 3 * tq
    sink = sink_ref[...]
    for i in range(blocks_per_step):
        n = pl.program_id(2) * blocks_per_step + i
        c0 = jnp.clip(n - 1, 0, n_blocks - 3)
        start = pl.multiple_of(c0 * tq, tq)
        q_st = _stack_group_queries(q_ref[i * tq:(i + 1) * tq, :])
        k_w = k_ref[pl.ds(start, span), :]
        s = lax.dot_general(k_w, q_st, (((1,), (1,)), ((), ())), preferred_element_type=F32)
        kpos = start + lax.broadcasted_iota(jnp.int32, s.shape, 0)
        qpos = n * tq + (lax.broadcasted_iota(jnp.int32, s.shape, 1) & (tq - 1))
        s = jnp.where(jnp.abs(kpos - qpos) <= WINDOW, s, NEG_INF)
        m = jnp.maximum(jnp.max(s, axis=0, keepdims=True), sink)
        p = jnp.exp2(s - m).astype(BF16)
        acc = jnp.dot(vt_ref[c0], p[:tq], preferred_element_type=F32)
        for j in range(1, 3):
            acc += jnp.dot(vt_ref[c0 + j], p[j * tq:(j + 1) * tq], preferred_element_type=F32)
        denom = acc[HEAD_DIM:HEAD_DIM + 1, :] + jnp.exp2(sink - m)
        o_ref[i * tq:(i + 1) * tq, :] = _unstack_group_outputs(acc[:HEAD_DIM, :] / denom, tq).astype(o_ref.dtype)


def _attention_global(q2d, k_hm, v_t, batch, seq, tq):
    tokens = q2d.shape[0]
    n_chunks, tk = v_t.shape[2], v_t.shape[4]
    assert n_chunks % 2 == 0, "the chunk pipeline advances two chunks per step"
    q_tiles = seq // tq
    m_rows = GROUPS * tq
    qmap = lambda b, h, i: (b * q_tiles + i, h)
    return pl.pallas_call(
        functools.partial(_attn_global_kernel, tq=tq, tk=tk, n_chunks=n_chunks),
        grid=(batch, KV_HEADS, q_tiles),
        in_specs=[
            pl.BlockSpec((tq, GROUP_COLS), qmap),
            pl.BlockSpec((None, None, seq, HEAD_DIM), lambda b, h, i: (b, h, 0, 0)),
            pl.BlockSpec((None, None, n_chunks, V_ROWS, tk), lambda b, h, i: (b, h, 0, 0, 0)),
        ],
        out_specs=pl.BlockSpec((tq, GROUP_COLS), qmap),
        out_shape=jax.ShapeDtypeStruct((tokens, Q_COLS), BF16),
        scratch_shapes=[pltpu.VMEM((tk, m_rows), F32), pltpu.VMEM((tk, m_rows), F32),
                        pltpu.VMEM((tk, m_rows), BF16), pltpu.VMEM((tk, m_rows), BF16),
                        pltpu.VMEM((V_ROWS, m_rows), F32)],
        compiler_params=_cparams(("arbitrary", "arbitrary", "arbitrary")),
        name="attn_global",
    )(q2d, k_hm, v_t)


def _attention_window(q2d, k_hm, v_t, sink, batch, seq):
    tokens = q2d.shape[0]
    tq = WINDOW
    n_blocks = seq // tq
    assert n_blocks >= 3
    m_rows = GROUPS * tq
    sink_rows = jnp.repeat(sink.astype(F32).reshape(KV_HEADS, GROUPS) * LOG2_E, tq, axis=1)[:, None, :]
    per_step = WINDOW_BLOCKS_PER_STEP if n_blocks % WINDOW_BLOCKS_PER_STEP == 0 else 1
    steps = n_blocks // per_step
    qmap = lambda b, h, i: (b * steps + i, h)
    return pl.pallas_call(
        functools.partial(_attn_window_kernel, tq=tq, n_blocks=n_blocks, blocks_per_step=per_step),
        grid=(batch, KV_HEADS, steps),
        in_specs=[
            pl.BlockSpec((per_step * tq, GROUP_COLS), qmap),
            pl.BlockSpec((None, None, seq, HEAD_DIM), lambda b, h, i: (b, h, 0, 0)),
            pl.BlockSpec((None, None, n_blocks, V_ROWS, tq), lambda b, h, i: (b, h, 0, 0, 0)),
            pl.BlockSpec((None, 1, m_rows), lambda b, h, i: (h, 0, 0)),
        ],
        out_specs=pl.BlockSpec((per_step * tq, GROUP_COLS), qmap),
        out_shape=jax.ShapeDtypeStruct((tokens, Q_COLS), BF16),
        compiler_params=_cparams(("arbitrary", "arbitrary", "arbitrary")),
        name="attn_window",
    )(q2d, k_hm, v_t, sink_rows)


def _store_token_tiles(ref, val):
    rows = val.shape[0]
    for j in range(val.shape[1] // LANES):
        ref[pl.ds(j, rows, stride=SUBLANES), :] = val[:, j * LANES:(j + 1) * LANES]


def _load_token_tiles(ref, rows):
    return jnp.concatenate([ref[pl.ds(j, rows, stride=SUBLANES), :] for j in range(SUBLANES)], axis=1)


ROUTE_ROWS = 2 * SUBLANES


def _top_k_route(logits_t):
    eid = lax.broadcasted_iota(jnp.int32, logits_t.shape, 0)
    work = logits_t
    vals, idxs, sels = [], [], []
    for _ in range(TOP_K):
        mx = jnp.max(work, axis=0, keepdims=True)
        ix = jnp.min(jnp.where(work == mx, eid, N_EXPERTS), axis=0, keepdims=True)
        sel = eid == ix
        work = jnp.where(sel, -jnp.inf, work)
        vals.append(mx)
        idxs.append(ix)
        sels.append(sel)
    return vals, idxs, sels


def _mid_kernel(oa_ref, ob_ref, sga_ref, sgb_ref, x_ref, wa_ref, wb_ref, wo_ref, g_ref, wr_ref, br_ref, tri_ref,
                x1_ref, h2_ref, route_ref, gate_ref, cnt_ref, *, halves):
    @pl.when(pl.program_id(0) == 0)
    def _():
        cnt_ref[...] = jnp.zeros(cnt_ref.shape, F32)

    tm = x_ref.shape[0]
    hm = tm // halves
    routed = []
    for hf in range(halves):
        rs = slice(hf * hm, (hf + 1) * hm)
        a = jnp.dot(oa_ref[rs, :], wa_ref[...], preferred_element_type=F32)
        b = jnp.dot(ob_ref[rs, :], wb_ref[...], preferred_element_type=F32)
        merged = sga_ref[rs, :].astype(F32) * a + sgb_ref[rs, :].astype(F32) * b
        x1 = x_ref[rs, :] + jnp.dot(merged.astype(BF16), wo_ref[...], preferred_element_type=F32)
        x1_ref[rs, :] = x1
        ms = jnp.mean(x1 * x1, axis=-1, keepdims=True)
        h2 = x1 * lax.rsqrt(ms + EPS) * g_ref[...]
        _store_token_tiles(h2_ref.at[pl.ds(hf * hm * SUBLANES, hm * SUBLANES)], h2)
        hi = h2.astype(BF16)
        lo = (h2 - hi.astype(F32)).astype(BF16)
        nt = (((1,), (1,)), ((), ()))
        r2 = (lax.dot_general(wr_ref[...], hi, nt, preferred_element_type=F32)
              + lax.dot_general(wr_ref[...], lo, nt, preferred_element_type=F32))
        bias = jnp.concatenate([br_ref[...]] * (hm // LANES), axis=1)
        routed.append(_top_k_route(r2[:N_EXPERTS] + r2[N_EXPERTS:] + bias))

    onehots = []
    for vals, idxs, sels in routed:
        chosen = sels[0] | sels[1] | sels[2] | sels[3]
        onehots.append(jnp.where(chosen, 1.0, 0.0))
    onehot = jnp.concatenate(onehots, axis=1)
    counted = jnp.concatenate([cnt_ref[...]] * (tm // LANES), axis=1)
    before = jnp.dot(onehot.astype(BF16), tri_ref[...], preferred_element_type=F32) + counted
    zeros4 = jnp.zeros((SUBLANES - TOP_K, hm), jnp.int32)
    for hf, (vals, idxs, sels) in enumerate(routed):
        rs = slice(hf * hm, (hf + 1) * hm)
        exps = [jnp.exp(v - vals[0]) for v in vals]
        denom = exps[0] + exps[1] + exps[2] + exps[3]
        ranks = [jnp.sum(jnp.where(sels[k], before[:, rs], 0.0), axis=0, keepdims=True).astype(jnp.int32)
                 for k in range(TOP_K)]
        route_ref[:, rs] = jnp.concatenate(idxs + [zeros4] + ranks + [zeros4], axis=0)
        gates_t = jnp.concatenate([e / denom for e in exps] + [jnp.zeros((LANES - TOP_K, hm), F32)], axis=0)
        gate_ref[rs, :] = gates_t.T
    cnt_ref[...] = cnt_ref[...] + jnp.sum(onehot, axis=1, keepdims=True)


def _mid(oa, ob, sga, sgb, x2d, w_a, w_b, w_o, g_ffn, w_router, b_router, tm):
    tokens, d_model = x2d.shape
    wr_t = w_router.astype(F32).T
    wr_hi = wr_t.astype(BF16)
    wr_lo = (wr_t - wr_hi.astype(F32)).astype(BF16)
    wr_split = jnp.concatenate([wr_hi, wr_lo], axis=0)
    br = jnp.broadcast_to(b_router.astype(F32)[:, None], (N_EXPERTS, LANES))
    tri = (jnp.arange(tm)[:, None] < jnp.arange(tm)[None, :]).astype(BF16)
    halves = 2 if tm % (2 * LANES) == 0 else 1
    row = lambda i: (i, 0)
    fixed = lambda i: (0, 0)
    return pl.pallas_call(
        functools.partial(_mid_kernel, halves=halves),
        grid=(tokens // tm,),
        in_specs=[
            pl.BlockSpec((tm, Q_COLS), row), pl.BlockSpec((tm, Q_COLS), row),
            pl.BlockSpec((tm, d_model), row), pl.BlockSpec((tm, d_model), row),
            pl.BlockSpec((tm, d_model), row),
            pl.BlockSpec((Q_COLS, d_model), fixed), pl.BlockSpec((Q_COLS, d_model), fixed),
            pl.BlockSpec((d_model, d_model), fixed),
            pl.BlockSpec((1, d_model), fixed),
            pl.BlockSpec((2 * N_EXPERTS, d_model), fixed),
            pl.BlockSpec((N_EXPERTS, LANES), fixed),
            pl.BlockSpec((tm, tm), fixed),
        ],
        out_specs=[
            pl.BlockSpec((tm, d_model), row),
            pl.BlockSpec((tm * SUBLANES, LANES), row),
            pl.BlockSpec((ROUTE_ROWS, tm), lambda i: (0, i)),
            pl.BlockSpec((tm, LANES), row),
            pl.BlockSpec((N_EXPERTS, LANES), fixed),
        ],
        out_shape=[
            jax.ShapeDtypeStruct((tokens, d_model), F32),
            jax.ShapeDtypeStruct((tokens * SUBLANES, LANES), F32),
            jax.ShapeDtypeStruct((ROUTE_ROWS, tokens), jnp.int32),
            jax.ShapeDtypeStruct((tokens, LANES), F32),
            jax.ShapeDtypeStruct((N_EXPERTS, LANES), F32),
        ],
        compiler_params=_cparams(("arbitrary",)),
        name="mid",
    )(oa, ob, sga, sgb, x2d, w_a.astype(BF16), w_b.astype(BF16), w_o.astype(BF16),
      g_ffn.astype(F32)[None, :], wr_split, br, tri)


def _invert_kernel(gap_lo_ref, gap_hi_ref, dest_ref, out_ref, *, tokens):
    def clear(r, carry):
        out_ref[r] = -1
        return carry
    for g in range(gap_lo_ref.shape[0]):
        lax.fori_loop(gap_lo_ref[g], gap_hi_ref[g], clear, 0)
    for k in range(TOP_K):
        def put(t, carry):
            out_ref[dest_ref[k * tokens + t]] = t * TOP_K + k
            return carry
        lax.fori_loop(0, tokens, put, 0, unroll=16)


def _invert_routes(dest_flat, gap_lo, gap_hi, n_rows, tokens):
    smem = pl.BlockSpec(memory_space=pltpu.SMEM)
    return pl.pallas_call(
        functools.partial(_invert_kernel, tokens=tokens),
        in_specs=[smem, smem, smem],
        out_specs=smem,
        out_shape=jax.ShapeDtypeStruct((n_rows,), jnp.int32),
        name="invert_routes",
    )(gap_lo, gap_hi, dest_flat)


def _tile_copy(src_ref, src_row, dst_ref, dst_row, sem):
    return pltpu.make_async_copy(
        src_ref.at[pl.ds(pl.multiple_of(src_row, SUBLANES), SUBLANES)],
        dst_ref.at[pl.ds(pl.multiple_of(dst_row, SUBLANES), SUBLANES)], sem)


def _start_tile_gather(off_smem, base, n_rows, src_ref, dst_ref, sem):
    def start(r, carry):
        _tile_copy(src_ref, off_smem[base + r], dst_ref, r * SUBLANES, sem).start()
        return carry
    lax.fori_loop(0, n_rows, start, 0, unroll=8)


def _start_tile_scatter(off_smem, base, n_rows, src_ref, dst_ref, sem):
    def start(r, carry):
        _tile_copy(src_ref, r * SUBLANES, dst_ref, off_smem[base + r], sem).start()
        return carry
    lax.fori_loop(0, n_rows, start, 0, unroll=8)


def _wait_tiles(n_rows, src_ref, dst_ref, sem):
    window = pl.ds(0, n_rows * SUBLANES)
    pltpu.make_async_copy(src_ref.at[window], dst_ref.at[window], sem).wait()


SRC_WINDOWS = 3
DST_WINDOWS = 5
MOE_BUFFERS = 3
MOE_COL_TILES = 4
DMA_THREADS = 2


def _moe_kernel(bexp_ref, nused_ref, wslot_ref, wnext_ref, src_ref, dst_ref, h2_ref, wgu_ref, bgu_ref, wd_ref, bd_ref,
                y_ref, src_smem, dst_smem, xbuf, ybuf, wgu_f32, wd_f32, wgu_bf, wd_bf, isem, gsem, ssem, wsem,
                *, rows, n_assign):
    i = pl.program_id(0)
    n_used = nused_ref[0]
    cur = lax.rem(i, MOE_BUFFERS)
    far = lax.rem(i + 2, MOE_BUFFERS)
    src_base = lambda blk: lax.rem(blk, SRC_WINDOWS) * rows
    dst_base = lambda blk: lax.rem(blk, DST_WINDOWS) * rows

    def idx_copies(blk):
        sw = pl.ds(pl.multiple_of(src_base(blk), rows), rows)
        dw = pl.ds(pl.multiple_of(dst_base(blk), rows), rows)
        return (pltpu.make_async_copy(src_ref.at[blk], src_smem.at[sw], isem.at[0, lax.rem(blk, SRC_WINDOWS)]),
                pltpu.make_async_copy(dst_ref.at[blk], dst_smem.at[dw], isem.at[1, lax.rem(blk, DST_WINDOWS)]))

    def fetch_idx_now(blk):
        for c in idx_copies(blk):
            c.start()
        for c in idx_copies(blk):
            c.wait()

    def gather_copy(blk_base, r, s):
        return _tile_copy(h2_ref, src_smem[blk_base + r], xbuf.at[s], r * SUBLANES, gsem.at[s])

    def scatter_copy(blk_base, r, s):
        return _tile_copy(ybuf.at[s], r * SUBLANES, y_ref, dst_smem[blk_base + r], ssem.at[s])

    def wait_scatter(s):
        _wait_tiles(rows, ybuf.at[s], y_ref, ssem.at[s])

    def swiglu(g, u):
        gate = jnp.minimum(g, SWIGLU_LIMIT)
        up = jnp.clip(u, -SWIGLU_LIMIT, SWIGLU_LIMIT)
        return ((up + 1.0) * (gate * jax.nn.sigmoid(SWIGLU_ALPHA * gate))).astype(BF16)

    @pl.when(i == 0)
    def _():
        ybuf[...] = jnp.zeros(ybuf.shape, F32)
        for s in range(2):
            dump = pl.ds((n_assign + s * rows) * SUBLANES, rows * SUBLANES)
            zero_fill = pltpu.make_async_copy(ybuf.at[s], y_ref.at[dump], ssem.at[s])
            zero_fill.start()
            zero_fill.wait()
        fetch_idx_now(0)
        _start_tile_gather(src_smem, 0, rows, h2_ref, xbuf.at[0], gsem.at[0])

        @pl.when(1 < n_used)
        def _():
            fetch_idx_now(1)
            _start_tile_gather(src_smem, src_base(1), rows, h2_ref, xbuf.at[1], gsem.at[1])

        @pl.when(2 < n_used)
        def _():
            fetch_idx_now(2)

    @pl.when(i + 3 < n_used)
    def _():
        for c in idx_copies(i + 3):
            c.start()

    @pl.when(i < n_used)
    def _():
        _wait_tiles(rows, h2_ref, xbuf.at[cur], gsem.at[cur])

    @pl.when(jnp.logical_and(i >= 3, i <= n_used))
    def _():
        wait_scatter(cur)

    new_expert = jnp.logical_or(i == 0, bexp_ref[i] != bexp_ref[jnp.maximum(i - 1, 0)])

    def weight_copies(expert, s):
        return (pltpu.make_async_copy(wgu_ref.at[expert], wgu_f32.at[s], wsem.at[0, s]),
                pltpu.make_async_copy(wd_ref.at[expert], wd_f32.at[s], wsem.at[1, s]))

    @pl.when(jnp.logical_and(new_expert, i < n_used))
    def _():
        s = wslot_ref[i]

        @pl.when(i == 0)
        def _():
            for c in weight_copies(bexp_ref[0], 0):
                c.start()

        for c in weight_copies(bexp_ref[i], s):
            c.wait()
        wgu_bf[...] = wgu_f32[s].astype(BF16)
        wd_bf[...] = wd_f32[s].astype(BF16)

        @pl.when(wnext_ref[i] >= 0)
        def _():
            for c in weight_copies(wnext_ref[i], 1 - s):
                c.start()

    steady = jnp.logical_and(i >= 1, i + 2 < n_used)

    @pl.when(steady)
    def _():
        gbase, sbase = src_base(i + 2), dst_base(i - 1)
        burst = rows // MOE_COL_TILES
        width = D_FF // MOE_COL_TILES
        x = _load_token_tiles(xbuf.at[cur], rows).astype(BF16)
        acts = []
        for t in range(MOE_COL_TILES):
            cg = slice(t * width, (t + 1) * width)
            cu = slice(D_FF + t * width, D_FF + (t + 1) * width)
            g = jnp.dot(x, wgu_bf[:, cg], preferred_element_type=F32) + bgu_ref[:, cg]
            u = jnp.dot(x, wgu_bf[:, cu], preferred_element_type=F32) + bgu_ref[:, cu]
            for r in range(t * burst, (t + 1) * burst):
                gather_copy(gbase, r, far).start(priority=r % DMA_THREADS)
            acts.append(swiglu(g, u))
        act = jnp.concatenate(acts, axis=1)
        out_width = wd_bf.shape[1] // MOE_COL_TILES
        for t in range(MOE_COL_TILES):
            co = slice(t * out_width, (t + 1) * out_width)
            y_t = jnp.dot(act, wd_bf[:, co], preferred_element_type=F32) + bd_ref[:, co]
            for r in range(t * burst, (t + 1) * burst):
                scatter_copy(sbase, r, far).start(priority=r % DMA_THREADS)
            for j in range(out_width // LANES):
                ybuf[cur, pl.ds(t * (out_width // LANES) + j, rows, stride=SUBLANES), :] = (
                    y_t[:, j * LANES:(j + 1) * LANES])

    @pl.when(jnp.logical_and(i < n_used, jnp.logical_not(steady)))
    def _():
        @pl.when(i + 2 < n_used)
        def _():
            _start_tile_gather(src_smem, src_base(i + 2), rows, h2_ref, xbuf.at[far], gsem.at[far])

        x = _load_token_tiles(xbuf.at[cur], rows).astype(BF16)
        gu = jnp.dot(x, wgu_bf[...], preferred_element_type=F32) + bgu_ref[...]
        act = swiglu(gu[:, :D_FF], gu[:, D_FF:])
        y = jnp.dot(act, wd_bf[...], preferred_element_type=F32) + bd_ref[...]
        _store_token_tiles(ybuf.at[cur], y)

        @pl.when(i >= 1)
        def _():
            _start_tile_scatter(dst_smem, dst_base(i - 1), rows, ybuf.at[far], y_ref, ssem.at[far])

    @pl.when(i == n_used)
    def _():
        _start_tile_scatter(dst_smem, dst_base(i - 1), rows, ybuf.at[far], y_ref, ssem.at[far])

        @pl.when(i >= 2)
        def _():
            wait_scatter(lax.rem(i + 1, MOE_BUFFERS))

        wait_scatter(far)

    @pl.when(i + 3 < n_used)
    def _():
        for c in idx_copies(i + 3):
            c.wait()


def _moe(block_exp, n_used, weight_slot, next_exp, src_rows, dst_rows, h2_tiles, w_gate_up, b_gate_up, w_down,
         b_down, rows, n_assign):
    n_blocks = src_rows.shape[0]
    n_exp, d_model, ff2 = w_gate_up.shape
    by_expert = lambda i, be, nu, ws, ne: (be[i], 0, 0)
    hbm = pl.BlockSpec(memory_space=pl.ANY)
    grid_spec = pltpu.PrefetchScalarGridSpec(
        num_scalar_prefetch=4,
        grid=(n_blocks,),
        in_specs=[
            hbm, hbm, hbm,
            hbm,
            pl.BlockSpec((None, 1, ff2), by_expert),
            hbm,
            pl.BlockSpec((None, 1, d_model), by_expert),
        ],
        out_specs=hbm,
        scratch_shapes=[
            pltpu.SMEM((SRC_WINDOWS * rows,), jnp.int32),
            pltpu.SMEM((DST_WINDOWS * rows,), jnp.int32),
            pltpu.VMEM((MOE_BUFFERS, rows * SUBLANES, LANES), F32),
            pltpu.VMEM((MOE_BUFFERS, rows * SUBLANES, LANES), F32),
            pltpu.VMEM((2, d_model, ff2), F32),
            pltpu.VMEM((2, ff2 // 2, d_model), F32),
            pltpu.VMEM((d_model, ff2), BF16),
            pltpu.VMEM((ff2 // 2, d_model), BF16),
            pltpu.SemaphoreType.DMA((2, DST_WINDOWS)),
            pltpu.SemaphoreType.DMA((MOE_BUFFERS,)),
            pltpu.SemaphoreType.DMA((MOE_BUFFERS,)),
            pltpu.SemaphoreType.DMA((2, 2)),
        ],
    )
    return pl.pallas_call(
        functools.partial(_moe_kernel, rows=rows, n_assign=n_assign),
        grid_spec=grid_spec,
        out_shape=jax.ShapeDtypeStruct(((n_assign + 2 * rows) * SUBLANES, LANES), F32),
        compiler_params=_cparams(("arbitrary",)),
        name="moe",
    )(block_exp, n_used, weight_slot, next_exp, src_rows, dst_rows, h2_tiles, w_gate_up, b_gate_up[:, None, :],
      w_down, b_down[:, None, :])


def _combine_kernel(y_ref, gate_ref, x1_ref, g_ref, o_ref, *, tm):
    gate = gate_ref[...]
    out = x1_ref[...]
    stride = TOP_K * SUBLANES
    for k in range(TOP_K):
        y_k = jnp.concatenate(
            [y_ref[pl.ds(k * SUBLANES + j, tm, stride=stride), :] for j in range(SUBLANES)], axis=1)
        out = out + gate[:, k:k + 1] * y_k
    ms = jnp.mean(out * out, axis=-1, keepdims=True)
    o_ref[...] = out * lax.rsqrt(ms + EPS) * g_ref[...]


def _combine(y_tiles, gate, x1, g_final, tm):
    tokens, d_model = x1.shape
    row = lambda i: (i, 0)
    return pl.pallas_call(
        functools.partial(_combine_kernel, tm=tm),
        grid=(tokens // tm,),
        in_specs=[
            pl.BlockSpec((tm * TOP_K * SUBLANES, LANES), row),
            pl.BlockSpec((tm, LANES), row),
            pl.BlockSpec((tm, d_model), row),
            pl.BlockSpec((1, d_model), lambda i: (0, 0)),
        ],
        out_specs=pl.BlockSpec((tm, d_model), row),
        out_shape=jax.ShapeDtypeStruct((tokens, d_model), F32),
        compiler_params=_cparams(("arbitrary",)),
        name="combine",
    )(y_tiles, gate, x1, g_final.astype(F32)[None, :])


PROJ_ROWS = 512
MID_ROWS = 512
ATTN_Q_ROWS = 1024
ATTN_K_ROWS = 512
WINDOW_BLOCKS_PER_STEP = 4
MOE_ROWS = 256
COMBINE_ROWS = 256


def _layer(x2d, batch, seq, g_mix, w_in, qa_norm, ka_norm, sink_b, w_branch_a, w_branch_b, w_out,
           g_ffn, w_router, b_router, w_gate_up, b_gate_up, w_down, b_down, g_final):
    tokens = x2d.shape[0]
    qa, ka, vta, qb, kb, vtb, sga, sgb = _project(x2d, g_mix, w_in, qa_norm, ka_norm, seq,
                                                  min(PROJ_ROWS, seq), min(ATTN_K_ROWS, seq // 2))
    oa = _attention_global(qa, ka, vta, batch, seq, ATTN_Q_ROWS)
    ob = _attention_window(qb, kb, vtb, sink_b, batch, seq)
    x1, h2_tiles, route, gate, cnt = _mid(
        oa, ob, sga, sgb, x2d, w_branch_a, w_branch_b, w_out, g_ffn, w_router, b_router, min(MID_ROWS, tokens))

    rows = MOE_ROWS
    n_assign = tokens * TOP_K
    n_blocks = n_assign // rows + N_EXPERTS
    counts = cnt[:, 0].astype(jnp.int32)
    padded = (counts + rows - 1) // rows * rows
    ends = jnp.cumsum(padded)
    starts = ends - padded
    eidx, rank = route[:TOP_K], route[SUBLANES:SUBLANES + TOP_K]
    seg_start = jnp.sum(jnp.where(eidx[None] == jnp.arange(N_EXPERTS)[:, None, None], starts[:, None, None], 0),
                        axis=0)
    dest = seg_start + rank
    gap_lo = jnp.concatenate([starts + counts, ends[-1:]]).astype(jnp.int32)
    gap_hi = jnp.concatenate([ends, jnp.full((1,), n_blocks * rows)]).astype(jnp.int32)
    row_assign = _invert_routes(dest.reshape(-1), gap_lo, gap_hi, n_blocks * rows, tokens).reshape(n_blocks, rows)
    real = row_assign >= 0
    dump_slot = (n_assign + (jnp.arange(n_blocks, dtype=jnp.int32) % 2)[:, None] * rows
                 + jnp.arange(rows, dtype=jnp.int32)[None, :])
    src_rows = jnp.where(real, row_assign // TOP_K, 0) * SUBLANES
    dst_rows = jnp.where(real, row_assign, dump_slot) * SUBLANES
    block_start = jnp.arange(n_blocks, dtype=jnp.int32) * rows
    block_exp = jnp.minimum(
        jnp.sum((ends[None, :] <= block_start[:, None]).astype(jnp.int32), axis=1), N_EXPERTS - 1)
    n_used = (ends[-1:] // rows).astype(jnp.int32)

    block_id = jnp.arange(n_blocks, dtype=jnp.int32)
    first_of_expert = jnp.logical_or(block_id == 0, block_exp != jnp.roll(block_exp, 1))
    weight_slot = ((jnp.cumsum(first_of_expert.astype(jnp.int32)) - 1) % 2).astype(jnp.int32)
    after_segment = ends[block_exp] // rows
    next_exp = jnp.where(after_segment < n_used[0],
                         block_exp[jnp.minimum(after_segment, n_blocks - 1)], -1).astype(jnp.int32)

    y_tiles = _moe(block_exp, n_used, weight_slot, next_exp, src_rows, dst_rows, h2_tiles,
                   w_gate_up, b_gate_up, w_down, b_down, rows, n_assign)
    return _combine(y_tiles, gate, x1, g_final, min(COMBINE_ROWS, tokens))


def kernel(x, g_mix, w_in, qa_norm, ka_norm, sink_b, w_branch_a, w_branch_b, w_out, g_ffn, w_router, b_router,
           w_gate_up, b_gate_up, w_down, b_down, g_final):
    batch, seq, d_model = x.shape
    assert g_mix.shape[0] == 1, "single-layer block"
    out = _layer(x.reshape(batch * seq, d_model), batch, seq, g_mix[0], w_in[0], qa_norm[0], ka_norm[0],
                 sink_b[0], w_branch_a[0], w_branch_b[0], w_out[0], g_ffn[0], w_router[0], b_router[0],
                 w_gate_up[0], b_gate_up[0], w_down[0], b_down[0], g_final)
    return out.reshape(batch, seq, d_model)
```

```python
import functools

import jax
import jax.numpy as jnp
from jax import lax
from jax.experimental import pallas as pl
from jax.experimental.pallas import tpu as pltpu

HEAD_DIM = 64
Q_HEADS = 8
KV_HEADS = 2
GROUPS = Q_HEADS // KV_HEADS
Q_COLS = Q_HEADS * HEAD_DIM
KV_COLS = KV_HEADS * HEAD_DIM
GROUP_COLS = GROUPS * HEAD_DIM
WINDOW = 128
GRID_W = 64
ROPE_THETA = 10000.0
N_EXPERTS = 32
TOP_K = 4
D_FF = 1024
SWIGLU_LIMIT = 7.0
SWIGLU_ALPHA = 1.702
EPS = 1e-6
NEG_INF = -1e30
LOG2_E = 1.4426950408889634

LANES = 128
SUBLANES = 8
V_ROWS = 80
VMEM_LIMIT = 56 * 1024 * 1024

F32 = jnp.float32
BF16 = jnp.bfloat16


def _cparams(sem, vmem=VMEM_LIMIT):
    return pltpu.CompilerParams(dimension_semantics=sem, vmem_limit_bytes=vmem)


def _rope(y, cos, sin_signed):
    lane = lax.broadcasted_iota(jnp.int32, y.shape, 1)
    first_half = (lane & 32) == 0
    rot = jnp.where(first_half, pltpu.roll(y, 96, 1), pltpu.roll(y, 32, 1))
    return y * cos + rot * sin_signed


def _head_rmsnorm(q, ones_blockdiag, gain):
    q2 = q * q
    hi = q2.astype(BF16)
    lo = (q2 - hi.astype(F32)).astype(BF16)
    ss = (jnp.dot(hi, ones_blockdiag, preferred_element_type=F32)
          + jnp.dot(lo, ones_blockdiag, preferred_element_type=F32))
    return q * lax.rsqrt(ss * (1.0 / HEAD_DIM) + EPS) * gain


def _proj_kernel(x_ref, g_ref, w_ref, cosa_ref, sina_ref, cos1_ref, sin1_ref, qg_ref, kg_ref, bd_ref,
                 qa_ref, ka_ref, vta_ref, qb_ref, kb_ref, vtb_ref, sga_ref, sgb_ref, *, d_model):
    x = x_ref[...]
    ms = jnp.mean(x * x, axis=-1, keepdims=True)
    h = (x * lax.rsqrt(ms + EPS) * g_ref[...]).astype(BF16)
    scale = HEAD_DIM ** -0.5 * LOG2_E

    def proj(c0, width):
        return jnp.dot(h, w_ref[:, c0:c0 + width], preferred_element_type=F32)

    cosa, sina = cosa_ref[...], sina_ref[...]
    cos1, sin1 = cos1_ref[...], sin1_ref[...]
    bd = bd_ref[...]
    slab = lambda v, s: v[:, s * LANES:(s + 1) * LANES]
    c = 0
    qa = proj(c, Q_COLS)
    for s in range(Q_COLS // LANES):
        q = _head_rmsnorm(slab(qa, s), bd, qg_ref[...])
        qa_ref[:, s * LANES:(s + 1) * LANES] = (_rope(q, cosa, sina) * scale).astype(BF16)
    c += Q_COLS
    kva = proj(c, 2 * KV_COLS)
    ka = _rope(_head_rmsnorm(slab(kva, 0), bd, kg_ref[...]), cosa, sina).astype(BF16)
    va_t = slab(kva, 1).T
    ones = jnp.ones((V_ROWS - HEAD_DIM, va_t.shape[1]), BF16)
    for hd in range(KV_HEADS):
        ka_ref[hd] = ka[:, hd * HEAD_DIM:(hd + 1) * HEAD_DIM]
        vta_ref[hd, :HEAD_DIM, :] = va_t[hd * HEAD_DIM:(hd + 1) * HEAD_DIM, :].astype(BF16)
        vta_ref[hd, HEAD_DIM:, :] = ones
    c += 2 * KV_COLS
    qb = proj(c, Q_COLS)
    for s in range(Q_COLS // LANES):
        qb_ref[:, s * LANES:(s + 1) * LANES] = (_rope(slab(qb, s), cos1, sin1) * scale).astype(BF16)
    c += Q_COLS
    kvb = proj(c, 2 * KV_COLS)
    kb = _rope(slab(kvb, 0), cos1, sin1).astype(BF16)
    vb_t = slab(kvb, 1).T
    ones = jnp.ones((V_ROWS - HEAD_DIM, WINDOW), BF16)
    for hd in range(KV_HEADS):
        kb_ref[hd] = kb[:, hd * HEAD_DIM:(hd + 1) * HEAD_DIM]
        for ch in range(vtb_ref.shape[1]):
            vtb_ref[hd, ch, :HEAD_DIM, :] = vb_t[hd * HEAD_DIM:(hd + 1) * HEAD_DIM,
                                                 ch * WINDOW:(ch + 1) * WINDOW].astype(BF16)
            vtb_ref[hd, ch, HEAD_DIM:, :] = ones
    c += 2 * KV_COLS
    sga_ref[...] = jax.nn.sigmoid(proj(c, d_model)).astype(BF16)
    c += d_model
    sgb_ref[...] = jax.nn.sigmoid(proj(c, d_model)).astype(BF16)


def _rope_tables(seq):
    def slab(ang):
        cos, sin = jnp.cos(ang), jnp.sin(ang)
        cos64 = jnp.concatenate([cos, cos], axis=-1)
        sin64 = jnp.concatenate([-sin, sin], axis=-1)
        return jnp.tile(cos64, (1, 2)), jnp.tile(sin64, (1, 2))

    pos = jnp.arange(seq, dtype=F32)
    inv1 = ROPE_THETA ** (-jnp.arange(0, HEAD_DIM, 2, dtype=F32) / HEAD_DIM)
    ang1 = pos[:, None] * inv1[None, :]
    rows = seq // GRID_W
    row = jnp.broadcast_to(jnp.arange(rows, dtype=F32)[:, None], (rows, GRID_W)).reshape(-1)
    col = jnp.broadcast_to(jnp.arange(GRID_W, dtype=F32)[None, :], (rows, GRID_W)).reshape(-1)
    axis_dim = HEAD_DIM // 2
    inva = ROPE_THETA ** (-jnp.arange(0, axis_dim, 2, dtype=F32) / axis_dim)
    anga = jnp.concatenate([row[:, None] * inva[None, :], col[:, None] * inva[None, :]], axis=-1)
    return slab(anga), slab(ang1)


def _project(x2d, g_mix, w_in, qa_norm, ka_norm, seq, tm, tk):
    tokens, d_model = x2d.shape
    in_cols = w_in.shape[1]
    (cosa, sina), (cos1, sin1) = _rope_tables(seq)
    head_id = jnp.arange(LANES) // HEAD_DIM
    ones_blockdiag = (head_id[:, None] == head_id[None, :]).astype(BF16)
    qg = jnp.tile(qa_norm.astype(F32), 2)[None, :]
    kg = jnp.tile(ka_norm.astype(F32), 2)[None, :]
    seq_tiles = seq // tm
    row = lambda i: (i, 0)
    pos = lambda i: (i % seq_tiles, 0)
    fixed = lambda i: (0, 0)
    batch = tokens // seq
    assert tk % tm == 0 and tm % WINDOW == 0
    sub = tk // tm
    bf = lambda *shape: jax.ShapeDtypeStruct(shape, BF16)
    tok_spec = lambda cols: pl.BlockSpec((tm, cols), row)
    k_spec = pl.BlockSpec((None, KV_HEADS, tm, HEAD_DIM), lambda i: (i // seq_tiles, 0, i % seq_tiles, 0))
    vta_spec = pl.BlockSpec((None, KV_HEADS, None, V_ROWS, tm),
                            lambda i: (i // seq_tiles, 0, (i % seq_tiles) // sub, 0, (i % seq_tiles) % sub))
    vtb_spec = pl.BlockSpec((None, KV_HEADS, tm // WINDOW, V_ROWS, WINDOW),
                            lambda i: (i // seq_tiles, 0, i % seq_tiles, 0, 0))
    k_shape = bf(batch, KV_HEADS, seq, HEAD_DIM)
    return pl.pallas_call(
        functools.partial(_proj_kernel, d_model=d_model),
        grid=(tokens // tm,),
        in_specs=[
            pl.BlockSpec((tm, d_model), row),
            pl.BlockSpec((1, d_model), fixed),
            pl.BlockSpec((d_model, in_cols), fixed),
            pl.BlockSpec((tm, LANES), pos), pl.BlockSpec((tm, LANES), pos),
            pl.BlockSpec((tm, LANES), pos), pl.BlockSpec((tm, LANES), pos),
            pl.BlockSpec((1, LANES), fixed), pl.BlockSpec((1, LANES), fixed),
            pl.BlockSpec((LANES, LANES), fixed),
        ],
        out_specs=[tok_spec(Q_COLS), k_spec, vta_spec, tok_spec(Q_COLS), k_spec, vtb_spec,
                   tok_spec(d_model), tok_spec(d_model)],
        out_shape=[bf(tokens, Q_COLS), k_shape, bf(batch, KV_HEADS, seq // tk, V_ROWS, tk),
                   bf(tokens, Q_COLS), k_shape, bf(batch, KV_HEADS, seq // WINDOW, V_ROWS, WINDOW),
                   bf(tokens, d_model), bf(tokens, d_model)],
        compiler_params=_cparams(("arbitrary",)),
        name="proj",
    )(x2d, g_mix.astype(F32)[None, :], w_in.astype(BF16), cosa, sina, cos1, sin1, qg, kg, ones_blockdiag)


def _stack_group_queries(q):
    return jnp.concatenate([q[:, g * HEAD_DIM:(g + 1) * HEAD_DIM] for g in range(GROUPS)], axis=0)


def _unstack_group_outputs(o_t, tq):
    pad = jnp.zeros((LANES - HEAD_DIM, o_t.shape[1]), o_t.dtype)
    o_pad = jnp.concatenate([o_t, pad], axis=0)
    heads = [o_pad[:, g * tq:(g + 1) * tq].T[:, :HEAD_DIM] for g in range(GROUPS)]
    return jnp.concatenate(heads, axis=1)


def _attn_global_kernel(q_ref, k_ref, vt_ref, o_ref, s0, s1, p0, p1, acc_ref, *, tq, tk, n_chunks):
    q_st = _stack_group_queries(q_ref[...])
    m_rows = q_st.shape[0]

    def scores(c, s_ref):
        k_c = k_ref[pl.ds(pl.multiple_of(c * tk, tk), tk), :]
        s_ref[...] = lax.dot_general(k_c, q_st, (((1,), (1,)), ((), ())), preferred_element_type=F32)

    def softmax(s_ref, p_ref, m_prev):
        m_new = jnp.maximum(m_prev, jnp.max(s_ref[...], axis=0, keepdims=True))
        p_ref[...] = jnp.exp2(s_ref[...] - m_new).astype(BF16)
        return m_new, jnp.exp2(m_prev - m_new)

    def values(c, p_ref, alpha):
        acc_ref[...] = acc_ref[...] * alpha + jnp.dot(vt_ref[c], p_ref[...], preferred_element_type=F32)

    scores(0, s0)
    p1[...] = jnp.zeros(p1.shape, BF16)
    acc_ref[...] = jnp.zeros(acc_ref.shape, F32)

    def pair(j, carry):
        m, alpha = carry
        c = 2 * j
        scores(c + 1, s1)
        m_a, alpha_a = softmax(s0, p0, m)
        values(jnp.maximum(c - 1, 0), p1, alpha)
        scores(jnp.minimum(c + 2, n_chunks - 1), s0)
        m_b, alpha_b = softmax(s1, p1, m_a)
        values(c, p0, alpha_a)
        return m_b, alpha_b

    init = (jnp.full((1, m_rows), NEG_INF, F32), jnp.ones((1, m_rows), F32))
    _, alpha = lax.fori_loop(0, n_chunks // 2, pair, init)
    values(n_chunks - 1, p1, alpha)
    acc = acc_ref[...]
    o_t = acc[:HEAD_DIM, :] / acc[HEAD_DIM:HEAD_DIM + 1, :]
    o_ref[...] = _unstack_group_outputs(o_t, tq).astype(o_ref.dtype)


def _attn_window_kernel(q_ref, k_ref, vt_ref, sink_ref, o_ref, *, tq, n_blocks, blocks_per_step):
    span = 3 * tq
    sink = sink_ref[...]
    for i in range(blocks_per_step):
        n = pl.program_id(2) * blocks_per_step + i
        c0 = jnp.clip(n - 1, 0, n_blocks - 3)
        start = pl.multiple_of(c0 * tq, tq)
        q_st = _stack_group_queries(q_ref[i * tq:(i + 1) * tq, :])
        k_w = k_ref[pl.ds(start, span), :]
        s = lax.dot_general(k_w, q_st, (((1,), (1,)), ((), ())), preferred_element_type=F32)
        kpos = start + lax.broadcasted_iota(jnp.int32, s.shape, 0)
        qpos = n * tq + (lax.broadcasted_iota(jnp.int32, s.shape, 1) & (tq - 1))
        s = jnp.where(jnp.abs(kpos - qpos) <= WINDOW, s, NEG_INF)
        m = jnp.maximum(jnp.max(s, axis=0, keepdims=True), sink)
        p = jnp.exp2(s - m).astype(BF16)
        acc = jnp.dot(vt_ref[c0], p[:tq], preferred_element_type=F32)
        for j in range(1, 3):
            acc += jnp.dot(vt_ref[c0 + j], p[j * tq:(j + 1) * tq], preferred_element_type=F32)
        denom = acc[HEAD_DIM:HEAD_DIM + 1, :] + jnp.exp2(sink - m)
        o_ref[i * tq:(i + 1) * tq, :] = _unstack_group_outputs(acc[:HEAD_DIM, :] / denom, tq).astype(o_ref.dtype)


def _attention_global(q2d, k_hm, v_t, batch, seq, tq):
    tokens = q2d.shape[0]
    n_chunks, tk = v_t.shape[2], v_t.shape[4]
    assert n_chunks % 2 == 0, "the chunk pipeline advances two chunks per step"
    q_tiles = seq // tq
    m_rows = GROUPS * tq
    qmap = lambda b, h, i: (b * q_tiles + i, h)
    return pl.pallas_call(
        functools.partial(_attn_global_kernel, tq=tq, tk=tk, n_chunks=n_chunks),
        grid=(batch, KV_HEADS, q_tiles),
        in_specs=[
            pl.BlockSpec((tq, GROUP_COLS), qmap),
            pl.BlockSpec((None, None, seq, HEAD_DIM), lambda b, h, i: (b, h, 0, 0)),
            pl.BlockSpec((None, None, n_chunks, V_ROWS, tk), lambda b, h, i: (b, h, 0, 0, 0)),
        ],
        out_specs=pl.BlockSpec((tq, GROUP_COLS), qmap),
        out_shape=jax.ShapeDtypeStruct((tokens, Q_COLS), BF16),
        scratch_shapes=[pltpu.VMEM((tk, m_rows), F32), pltpu.VMEM((tk, m_rows), F32),
                        pltpu.VMEM((tk, m_rows), BF16), pltpu.VMEM((tk, m_rows), BF16),
                        pltpu.VMEM((V_ROWS, m_rows), F32)],
        compiler_params=_cparams(("arbitrary", "arbitrary", "arbitrary")),
        name="attn_global",
    )(q2d, k_hm, v_t)


def _attention_window(q2d, k_hm, v_t, sink, batch, seq):
    tokens = q2d.shape[0]
    tq = WINDOW
    n_blocks = seq // tq
    assert n_blocks >= 3
    m_rows = GROUPS * tq
    sink_rows = jnp.repeat(sink.astype(F32).reshape(KV_HEADS, GROUPS) * LOG2_E, tq, axis=1)[:, None, :]
    per_step = WINDOW_BLOCKS_PER_STEP if n_blocks % WINDOW_BLOCKS_PER_STEP == 0 else 1
    steps = n_blocks // per_step
    qmap = lambda b, h, i: (b * steps + i, h)
    return pl.pallas_call(
        functools.partial(_attn_window_kernel, tq=tq, n_blocks=n_blocks, blocks_per_step=per_step),
        grid=(batch, KV_HEADS, steps),
        in_specs=[
            pl.BlockSpec((per_step * tq, GROUP_COLS), qmap),
            pl.BlockSpec((None, None, seq, HEAD_DIM), lambda b, h, i: (b, h, 0, 0)),
            pl.BlockSpec((None, None, n_blocks, V_ROWS, tq), lambda b, h, i: (b, h, 0, 0, 0)),
            pl.BlockSpec((None, 1, m_rows), lambda b, h, i: (h, 0, 0)),
        ],
        out_specs=pl.BlockSpec((per_step * tq, GROUP_COLS), qmap),
        out_shape=jax.ShapeDtypeStruct((tokens, Q_COLS), BF16),
        compiler_params=_cparams(("arbitrary", "arbitrary", "arbitrary")),
        name="attn_window",
    )(q2d, k_hm, v_t, sink_rows)


def _store_token_tiles(ref, val):
    rows = val.shape[0]
    for j in range(val.shape[1] // LANES):
        ref[pl.ds(j, rows, stride=SUBLANES), :] = val[:, j * LANES:(j + 1) * LANES]


def _load_token_tiles(ref, rows):
    return jnp.concatenate([ref[pl.ds(j, rows, stride=SUBLANES), :] for j in range(SUBLANES)], axis=1)


ROUTE_ROWS = 2 * SUBLANES


def _top_k_route(logits_t):
    eid = lax.broadcasted_iota(jnp.int32, logits_t.shape, 0)
    work = logits_t
    vals, idxs, sels = [], [], []
    for _ in range(TOP_K):
        mx = jnp.max(work, axis=0, keepdims=True)
        ix = jnp.min(jnp.where(work == mx, eid, N_EXPERTS), axis=0, keepdims=True)
        sel = eid == ix
        work = jnp.where(sel, -jnp.inf, work)
        vals.append(mx)
        idxs.append(ix)
        sels.append(sel)
    return vals, idxs, sels


def _mid_kernel(oa_ref, ob_ref, sga_ref, sgb_ref, x_ref, wa_ref, wb_ref, wo_ref, g_ref, wr_ref, br_ref, tri_ref,
                x1_ref, h2_ref, route_ref, gate_ref, cnt_ref, *, halves):
    @pl.when(pl.program_id(0) == 0)
    def _():
        cnt_ref[...] = jnp.zeros(cnt_ref.shape, F32)

    tm = x_ref.shape[0]
    hm = tm // halves
    routed = []
    for hf in range(halves):
        rs = slice(hf * hm, (hf + 1) * hm)
        a = jnp.dot(oa_ref[rs, :], wa_ref[...], preferred_element_type=F32)
        b = jnp.dot(ob_ref[rs, :], wb_ref[...], preferred_element_type=F32)
        merged = sga_ref[rs, :].astype(F32) * a + sgb_ref[rs, :].astype(F32) * b
        x1 = x_ref[rs, :] + jnp.dot(merged.astype(BF16), wo_ref[...], preferred_element_type=F32)
        x1_ref[rs, :] = x1
        ms = jnp.mean(x1 * x1, axis=-1, keepdims=True)
        h2 = x1 * lax.rsqrt(ms + EPS) * g_ref[...]
        h2_ref[rs, :] = h2
        hi = h2.astype(BF16)
        lo = (h2 - hi.astype(F32)).astype(BF16)
        nt = (((1,), (1,)), ((), ()))
        r2 = (lax.dot_general(wr_ref[...], hi, nt, preferred_element_type=F32)
              + lax.dot_general(wr_ref[...], lo, nt, preferred_element_type=F32))
        bias = jnp.concatenate([br_ref[...]] * (hm // LANES), axis=1)
        routed.append(_top_k_route(r2[:N_EXPERTS] + r2[N_EXPERTS:] + bias))

    onehots = []
    for vals, idxs, sels in routed:
        chosen = sels[0] | sels[1] | sels[2] | sels[3]
        onehots.append(jnp.where(chosen, 1.0, 0.0))
    onehot = jnp.concatenate(onehots, axis=1)
    counted = jnp.concatenate([cnt_ref[...]] * (tm // LANES), axis=1)
    before = jnp.dot(onehot.astype(BF16), tri_ref[...], preferred_element_type=F32) + counted
    zeros4 = jnp.zeros((SUBLANES - TOP_K, hm), jnp.int32)
    for hf, (vals, idxs, sels) in enumerate(routed):
        rs = slice(hf * hm, (hf + 1) * hm)
        exps = [jnp.exp(v - vals[0]) for v in vals]
        denom = exps[0] + exps[1] + exps[2] + exps[3]
        ranks = [jnp.sum(jnp.where(sels[k], before[:, rs], 0.0), axis=0, keepdims=True).astype(jnp.int32)
                 for k in range(TOP_K)]
        route_ref[:, rs] = jnp.concatenate(idxs + [zeros4] + ranks + [zeros4], axis=0)
        gates_t = jnp.concatenate([e / denom for e in exps] + [jnp.zeros((LANES - TOP_K, hm), F32)], axis=0)
        gate_ref[rs, :] = gates_t.T
    cnt_ref[...] = cnt_ref[...] + jnp.sum(onehot, axis=1, keepdims=True)


def _mid(oa, ob, sga, sgb, x2d, w_a, w_b, w_o, g_ffn, w_router, b_router, tm):
    tokens, d_model = x2d.shape
    wr_t = w_router.astype(F32).T
    wr_hi = wr_t.astype(BF16)
    wr_lo = (wr_t - wr_hi.astype(F32)).astype(BF16)
    wr_split = jnp.concatenate([wr_hi, wr_lo], axis=0)
    br = jnp.broadcast_to(b_router.astype(F32)[:, None], (N_EXPERTS, LANES))
    tri = (jnp.arange(tm)[:, None] < jnp.arange(tm)[None, :]).astype(BF16)
    halves = 2 if tm % (2 * LANES) == 0 else 1
    row = lambda i: (i, 0)
    fixed = lambda i: (0, 0)
    return pl.pallas_call(
        functools.partial(_mid_kernel, halves=halves),
        grid=(tokens // tm,),
        in_specs=[
            pl.BlockSpec((tm, Q_COLS), row), pl.BlockSpec((tm, Q_COLS), row),
            pl.BlockSpec((tm, d_model), row), pl.BlockSpec((tm, d_model), row),
            pl.BlockSpec((tm, d_model), row),
            pl.BlockSpec((Q_COLS, d_model), fixed), pl.BlockSpec((Q_COLS, d_model), fixed),
            pl.BlockSpec((d_model, d_model), fixed),
            pl.BlockSpec((1, d_model), fixed),
            pl.BlockSpec((2 * N_EXPERTS, d_model), fixed),
            pl.BlockSpec((N_EXPERTS, LANES), fixed),
            pl.BlockSpec((tm, tm), fixed),
        ],
        out_specs=[
            pl.BlockSpec((tm, d_model), row),
            pl.BlockSpec((tm, d_model), row),
            pl.BlockSpec((ROUTE_ROWS, tm), lambda i: (0, i)),
            pl.BlockSpec((tm, LANES), row),
            pl.BlockSpec((N_EXPERTS, LANES), fixed),
        ],
        out_shape=[
            jax.ShapeDtypeStruct((tokens, d_model), F32),
            jax.ShapeDtypeStruct((tokens, d_model), F32),
            jax.ShapeDtypeStruct((ROUTE_ROWS, tokens), jnp.int32),
            jax.ShapeDtypeStruct((tokens, LANES), F32),
            jax.ShapeDtypeStruct((N_EXPERTS, LANES), F32),
        ],
        compiler_params=_cparams(("arbitrary",)),
        name="mid",
    )(oa, ob, sga, sgb, x2d, w_a.astype(BF16), w_b.astype(BF16), w_o.astype(BF16),
      g_ffn.astype(F32)[None, :], wr_split, br, tri)


def _invert_kernel(gap_lo_ref, gap_hi_ref, dest_ref, out_ref, *, tokens):
    def clear(r, carry):
        out_ref[r] = -1
        return carry
    for g in range(gap_lo_ref.shape[0]):
        lax.fori_loop(gap_lo_ref[g], gap_hi_ref[g], clear, 0)
    for k in range(TOP_K):
        def put(t, carry):
            out_ref[dest_ref[k * tokens + t]] = t * TOP_K + k
            return carry
        lax.fori_loop(0, tokens, put, 0, unroll=16)


def _invert_routes(dest_flat, gap_lo, gap_hi, n_rows, tokens):
    smem = pl.BlockSpec(memory_space=pltpu.SMEM)
    return pl.pallas_call(
        functools.partial(_invert_kernel, tokens=tokens),
        in_specs=[smem, smem, smem],
        out_specs=smem,
        out_shape=jax.ShapeDtypeStruct((n_rows,), jnp.int32),
        name="invert_routes",
    )(gap_lo, gap_hi, dest_flat)


def _tile_copy(src_ref, src_row, dst_ref, dst_row, sem):
    return pltpu.make_async_copy(
        src_ref.at[pl.ds(pl.multiple_of(src_row, SUBLANES), SUBLANES)],
        dst_ref.at[pl.ds(pl.multiple_of(dst_row, SUBLANES), SUBLANES)], sem)


def _row_copy(src_ref, src_row, dst_ref, dst_row, sem):
    return pltpu.make_async_copy(src_ref.at[pl.ds(src_row, 1)], dst_ref.at[pl.ds(dst_row, 1)], sem)


def _start_row_gather(off_smem, base, n_rows, src_ref, dst_ref, sem):
    def start(r, carry):
        _row_copy(src_ref, off_smem[base + r], dst_ref, r, sem).start()
        return carry
    lax.fori_loop(0, n_rows, start, 0, unroll=8)


def _wait_rows(n_rows, src_ref, dst_ref, sem):
    window = pl.ds(0, n_rows)
    pltpu.make_async_copy(src_ref.at[window], dst_ref.at[window], sem).wait()


def _start_tile_scatter(off_smem, base, n_rows, src_ref, dst_ref, sem):
    def start(r, carry):
        _tile_copy(src_ref, r * SUBLANES, dst_ref, off_smem[base + r], sem).start()
        return carry
    lax.fori_loop(0, n_rows, start, 0, unroll=8)


def _wait_tiles(n_rows, src_ref, dst_ref, sem):
    window = pl.ds(0, n_rows * SUBLANES)
    pltpu.make_async_copy(src_ref.at[window], dst_ref.at[window], sem).wait()


SRC_WINDOWS = 3
DST_WINDOWS = 5
MOE_BUFFERS = 3
MOE_COL_TILES = 4
DMA_THREADS = 2


def _moe_kernel(bexp_ref, nused_ref, wslot_ref, wnext_ref, src_ref, dst_ref, h2_ref, wgu_ref, bgu_ref, wd_ref, bd_ref,
                y_ref, src_smem, dst_smem, xbuf, ybuf, wgu_f32, wd_f32, wgu_bf, wd_bf, isem, gsem, ssem, wsem,
                *, rows, n_assign):
    i = pl.program_id(0)
    n_used = nused_ref[0]
    cur = lax.rem(i, MOE_BUFFERS)
    far = lax.rem(i + 2, MOE_BUFFERS)
    src_base = lambda blk: lax.rem(blk, SRC_WINDOWS) * rows
    dst_base = lambda blk: lax.rem(blk, DST_WINDOWS) * rows

    def idx_copies(blk):
        sw = pl.ds(pl.multiple_of(src_base(blk), rows), rows)
        dw = pl.ds(pl.multiple_of(dst_base(blk), rows), rows)
        return (pltpu.make_async_copy(src_ref.at[blk], src_smem.at[sw], isem.at[0, lax.rem(blk, SRC_WINDOWS)]),
                pltpu.make_async_copy(dst_ref.at[blk], dst_smem.at[dw], isem.at[1, lax.rem(blk, DST_WINDOWS)]))

    def fetch_idx_now(blk):
        for c in idx_copies(blk):
            c.start()
        for c in idx_copies(blk):
            c.wait()

    def gather_copy(blk_base, r, s):
        return _row_copy(h2_ref, src_smem[blk_base + r], xbuf.at[s], r, gsem.at[s])

    def scatter_copy(blk_base, r, s):
        return _tile_copy(ybuf.at[s], r * SUBLANES, y_ref, dst_smem[blk_base + r], ssem.at[s])

    def wait_scatter(s):
        _wait_tiles(rows, ybuf.at[s], y_ref, ssem.at[s])

    def swiglu(g, u):
        gate = jnp.minimum(g, SWIGLU_LIMIT)
        up = jnp.clip(u, -SWIGLU_LIMIT, SWIGLU_LIMIT)
        return ((up + 1.0) * (gate * jax.nn.sigmoid(SWIGLU_ALPHA * gate))).astype(BF16)

    @pl.when(i == 0)
    def _():
        ybuf[...] = jnp.zeros(ybuf.shape, F32)
        for s in range(2):
            dump = pl.ds((n_assign + s * rows) * SUBLANES, rows * SUBLANES)
            zero_fill = pltpu.make_async_copy(ybuf.at[s], y_ref.at[dump], ssem.at[s])
            zero_fill.start()
            zero_fill.wait()
        fetch_idx_now(0)
        _start_row_gather(src_smem, 0, rows, h2_ref, xbuf.at[0], gsem.at[0])

        @pl.when(1 < n_used)
        def _():
            fetch_idx_now(1)
            _start_row_gather(src_smem, src_base(1), rows, h2_ref, xbuf.at[1], gsem.at[1])

        @pl.when(2 < n_used)
        def _():
            fetch_idx_now(2)

    @pl.when(i + 3 < n_used)
    def _():
        for c in idx_copies(i + 3):
            c.start()

    @pl.when(i < n_used)
    def _():
        _wait_rows(rows, h2_ref, xbuf.at[cur], gsem.at[cur])

    @pl.when(jnp.logical_and(i >= 3, i <= n_used))
    def _():
        wait_scatter(cur)

    new_expert = jnp.logical_or(i == 0, bexp_ref[i] != bexp_ref[jnp.maximum(i - 1, 0)])

    def weight_copies(expert, s):
        return (pltpu.make_async_copy(wgu_ref.at[expert], wgu_f32.at[s], wsem.at[0, s]),
                pltpu.make_async_copy(wd_ref.at[expert], wd_f32.at[s], wsem.at[1, s]))

    @pl.when(jnp.logical_and(new_expert, i < n_used))
    def _():
        s = wslot_ref[i]

        @pl.when(i == 0)
        def _():
            for c in weight_copies(bexp_ref[0], 0):
                c.start()

        for c in weight_copies(bexp_ref[i], s):
            c.wait()
        wgu_bf[...] = wgu_f32[s].astype(BF16)
        wd_bf[...] = wd_f32[s].astype(BF16)

        @pl.when(wnext_ref[i] >= 0)
        def _():
            for c in weight_copies(wnext_ref[i], 1 - s):
                c.start()

    steady = jnp.logical_and(i >= 1, i + 2 < n_used)

    @pl.when(steady)
    def _():
        gbase, sbase = src_base(i + 2), dst_base(i - 1)
        burst = rows // MOE_COL_TILES
        width = D_FF // MOE_COL_TILES
        x = xbuf[cur].astype(BF16)
        acts = []
        for t in range(MOE_COL_TILES):
            cg = slice(t * width, (t + 1) * width)
            cu = slice(D_FF + t * width, D_FF + (t + 1) * width)
            g = jnp.dot(x, wgu_bf[:, cg], preferred_element_type=F32) + bgu_ref[:, cg]
            u = jnp.dot(x, wgu_bf[:, cu], preferred_element_type=F32) + bgu_ref[:, cu]
            for r in range(t * burst, (t + 1) * burst):
                gather_copy(gbase, r, far).start(priority=r % DMA_THREADS)
            acts.append(swiglu(g, u))
        act = jnp.concatenate(acts, axis=1)
        out_width = wd_bf.shape[1] // MOE_COL_TILES
        for t in range(MOE_COL_TILES):
            co = slice(t * out_width, (t + 1) * out_width)
            y_t = jnp.dot(act, wd_bf[:, co], preferred_element_type=F32) + bd_ref[:, co]
            for r in range(t * burst, (t + 1) * burst):
                scatter_copy(sbase, r, far).start(priority=r % DMA_THREADS)
            for j in range(out_width // LANES):
                ybuf[cur, pl.ds(t * (out_width // LANES) + j, rows, stride=SUBLANES), :] = (
                    y_t[:, j * LANES:(j + 1) * LANES])

    @pl.when(jnp.logical_and(i < n_used, jnp.logical_not(steady)))
    def _():
        @pl.when(i + 2 < n_used)
        def _():
            _start_row_gather(src_smem, src_base(i + 2), rows, h2_ref, xbuf.at[far], gsem.at[far])

        x = xbuf[cur].astype(BF16)
        gu = jnp.dot(x, wgu_bf[...], preferred_element_type=F32) + bgu_ref[...]
        act = swiglu(gu[:, :D_FF], gu[:, D_FF:])
        y = jnp.dot(act, wd_bf[...], preferred_element_type=F32) + bd_ref[...]
        _store_token_tiles(ybuf.at[cur], y)

        @pl.when(i >= 1)
        def _():
            _start_tile_scatter(dst_smem, dst_base(i - 1), rows, ybuf.at[far], y_ref, ssem.at[far])

    @pl.when(i == n_used)
    def _():
        _start_tile_scatter(dst_smem, dst_base(i - 1), rows, ybuf.at[far], y_ref, ssem.at[far])

        @pl.when(i >= 2)
        def _():
            wait_scatter(lax.rem(i + 1, MOE_BUFFERS))

        wait_scatter(far)

    @pl.when(i + 3 < n_used)
    def _():
        for c in idx_copies(i + 3):
            c.wait()


def _moe(block_exp, n_used, weight_slot, next_exp, src_rows, dst_rows, h2_tiles, w_gate_up, b_gate_up, w_down,
         b_down, rows, n_assign):
    n_blocks = src_rows.shape[0]
    n_exp, d_model, ff2 = w_gate_up.shape
    by_expert = lambda i, be, nu, ws, ne: (be[i], 0, 0)
    hbm = pl.BlockSpec(memory_space=pl.ANY)
    grid_spec = pltpu.PrefetchScalarGridSpec(
        num_scalar_prefetch=4,
        grid=(n_blocks,),
        in_specs=[
            hbm, hbm, hbm,
            hbm,
            pl.BlockSpec((None, 1, ff2), by_expert),
            hbm,
            pl.BlockSpec((None, 1, d_model), by_expert),
        ],
        out_specs=hbm,
        scratch_shapes=[
            pltpu.SMEM((SRC_WINDOWS * rows,), jnp.int32),
            pltpu.SMEM((DST_WINDOWS * rows,), jnp.int32),
            pltpu.VMEM((MOE_BUFFERS, rows, d_model), F32),
            pltpu.VMEM((MOE_BUFFERS, rows * SUBLANES, LANES), F32),
            pltpu.VMEM((2, d_model, ff2), F32),
            pltpu.VMEM((2, ff2 // 2, d_model), F32),
            pltpu.VMEM((d_model, ff2), BF16),
            pltpu.VMEM((ff2 // 2, d_model), BF16),
            pltpu.SemaphoreType.DMA((2, DST_WINDOWS)),
            pltpu.SemaphoreType.DMA((MOE_BUFFERS,)),
            pltpu.SemaphoreType.DMA((MOE_BUFFERS,)),
            pltpu.SemaphoreType.DMA((2, 2)),
        ],
    )
    return pl.pallas_call(
        functools.partial(_moe_kernel, rows=rows, n_assign=n_assign),
        grid_spec=grid_spec,
        out_shape=jax.ShapeDtypeStruct(((n_assign + 2 * rows) * SUBLANES, LANES), F32),
        compiler_params=_cparams(("arbitrary",)),
        name="moe",
    )(block_exp, n_used, weight_slot, next_exp, src_rows, dst_rows, h2_tiles, w_gate_up, b_gate_up[:, None, :],
      w_down, b_down[:, None, :])


def _combine_kernel(y_ref, gate_ref, x1_ref, g_ref, o_ref, *, tm):
    gate = gate_ref[...]
    out = x1_ref[...]
    stride = TOP_K * SUBLANES
    for k in range(TOP_K):
        y_k = jnp.concatenate(
            [y_ref[pl.ds(k * SUBLANES + j, tm, stride=stride), :] for j in range(SUBLANES)], axis=1)
        out = out + gate[:, k:k + 1] * y_k
    ms = jnp.mean(out * out, axis=-1, keepdims=True)
    o_ref[...] = out * lax.rsqrt(ms + EPS) * g_ref[...]


def _combine(y_tiles, gate, x1, g_final, tm):
    tokens, d_model = x1.shape
    row = lambda i: (i, 0)
    return pl.pallas_call(
        functools.partial(_combine_kernel, tm=tm),
        grid=(tokens // tm,),
        in_specs=[
            pl.BlockSpec((tm * TOP_K * SUBLANES, LANES), row),
            pl.BlockSpec((tm, LANES), row),
            pl.BlockSpec((tm, d_model), row),
            pl.BlockSpec((1, d_model), lambda i: (0, 0)),
        ],
        out_specs=pl.BlockSpec((tm, d_model), row),
        out_shape=jax.ShapeDtypeStruct((tokens, d_model), F32),
        compiler_params=_cparams(("arbitrary",)),
        name="combine",
    )(y_tiles, gate, x1, g_final.astype(F32)[None, :])


PROJ_ROWS = 512
MID_ROWS = 512
ATTN_Q_ROWS = 1024
ATTN_K_ROWS = 512
WINDOW_BLOCKS_PER_STEP = 4
MOE_ROWS = 256
COMBINE_ROWS = 256


def _layer(x2d, batch, seq, g_mix, w_in, qa_norm, ka_norm, sink_b, w_branch_a, w_branch_b, w_out,
           g_ffn, w_router, b_router, w_gate_up, b_gate_up, w_down, b_down, g_final):
    tokens = x2d.shape[0]
    qa, ka, vta, qb, kb, vtb, sga, sgb = _project(x2d, g_mix, w_in, qa_norm, ka_norm, seq,
                                                  min(PROJ_ROWS, seq), min(ATTN_K_ROWS, seq // 2))
    oa = _attention_global(qa, ka, vta, batch, seq, ATTN_Q_ROWS)
    ob = _attention_window(qb, kb, vtb, sink_b, batch, seq)
    x1, h2_tiles, route, gate, cnt = _mid(
        oa, ob, sga, sgb, x2d, w_branch_a, w_branch_b, w_out, g_ffn, w_router, b_router, min(MID_ROWS, tokens))

    rows = MOE_ROWS
    n_assign = tokens * TOP_K
    n_blocks = n_assign // rows + N_EXPERTS
    counts = cnt[:, 0].astype(jnp.int32)
    padded = (counts + rows - 1) // rows * rows
    ends = jnp.cumsum(padded)
    starts = ends - padded
    eidx, rank = route[:TOP_K], route[SUBLANES:SUBLANES + TOP_K]
    seg_start = jnp.sum(jnp.where(eidx[None] == jnp.arange(N_EXPERTS)[:, None, None], starts[:, None, None], 0),
                        axis=0)
    dest = seg_start + rank
    gap_lo = jnp.concatenate([starts + counts, ends[-1:]]).astype(jnp.int32)
    gap_hi = jnp.concatenate([ends, jnp.full((1,), n_blocks * rows)]).astype(jnp.int32)
    row_assign = _invert_routes(dest.reshape(-1), gap_lo, gap_hi, n_blocks * rows, tokens).reshape(n_blocks, rows)
    real = row_assign >= 0
    dump_slot = (n_assign + (jnp.arange(n_blocks, dtype=jnp.int32) % 2)[:, None] * rows
                 + jnp.arange(rows, dtype=jnp.int32)[None, :])
    src_rows = jnp.where(real, row_assign // TOP_K, 0)
    dst_rows = jnp.where(real, row_assign, dump_slot) * SUBLANES
    block_start = jnp.arange(n_blocks, dtype=jnp.int32) * rows
    block_exp = jnp.minimum(
        jnp.sum((ends[None, :] <= block_start[:, None]).astype(jnp.int32), axis=1), N_EXPERTS - 1)
    n_used = (ends[-1:] // rows).astype(jnp.int32)

    block_id = jnp.arange(n_blocks, dtype=jnp.int32)
    first_of_expert = jnp.logical_or(block_id == 0, block_exp != jnp.roll(block_exp, 1))
    weight_slot = ((jnp.cumsum(first_of_expert.astype(jnp.int32)) - 1) % 2).astype(jnp.int32)
    after_segment = ends[block_exp] // rows
    next_exp = jnp.where(after_segment < n_used[0],
                         block_exp[jnp.minimum(after_segment, n_blocks - 1)], -1).astype(jnp.int32)

    y_tiles = _moe(block_exp, n_used, weight_slot, next_exp, src_rows, dst_rows, h2_tiles,
                   w_gate_up, b_gate_up, w_down, b_down, rows, n_assign)
    return _combine(y_tiles, gate, x1, g_final, min(COMBINE_ROWS, tokens))


def kernel(x, g_mix, w_in, qa_norm, ka_norm, sink_b, w_branch_a, w_branch_b, w_out, g_ffn, w_router, b_router,
           w_gate_up, b_gate_up, w_down, b_down, g_final):
    batch, seq, d_model = x.shape
    assert g_mix.shape[0] == 1, "single-layer block"
    out = _layer(x.reshape(batch * seq, d_model), batch, seq, g_mix[0], w_in[0], qa_norm[0], ka_norm[0],
                 sink_b[0], w_branch_a[0], w_branch_b[0], w_out[0], g_ffn[0], w_router[0], b_router[0],
                 w_gate_up[0], b_gate_up[0], w_down[0], b_down[0], g_final)
    return out.reshape(batch, seq, d_model)
```

```python
import functools

import jax
import jax.numpy as jnp
from jax import lax
from jax.experimental import pallas as pl
from jax.experimental.pallas import tpu as pltpu

HEAD_DIM = 64
Q_HEADS = 8
KV_HEADS = 2
GROUPS = Q_HEADS // KV_HEADS
Q_COLS = Q_HEADS * HEAD_DIM
KV_COLS = KV_HEADS * HEAD_DIM
GROUP_COLS = GROUPS * HEAD_DIM
WINDOW = 128
GRID_W = 64
ROPE_THETA = 10000.0
N_EXPERTS = 32
TOP_K = 4
D_FF = 1024
SWIGLU_LIMIT = 7.0
SWIGLU_ALPHA = 1.702
EPS = 1e-6
NEG_INF = -1e30
LOG2_E = 1.4426950408889634

LANES = 128
SUBLANES = 8
V_ROWS = 80
VMEM_LIMIT = 56 * 1024 * 1024

F32 = jnp.float32
BF16 = jnp.bfloat16


def _cparams(sem, vmem=VMEM_LIMIT):
    return pltpu.CompilerParams(dimension_semantics=sem, vmem_limit_bytes=vmem)


def _rope(y, cos, sin_signed):
    lane = lax.broadcasted_iota(jnp.int32, y.shape, 1)
    first_half = (lane & 32) == 0
    rot = jnp.where(first_half, pltpu.roll(y, 96, 1), pltpu.roll(y, 32, 1))
    return y * cos + rot * sin_signed


def _head_rmsnorm(q, ones_blockdiag, gain):
    q2 = q * q
    hi = q2.astype(BF16)
    lo = (q2 - hi.astype(F32)).astype(BF16)
    ss = (jnp.dot(hi, ones_blockdiag, preferred_element_type=F32)
          + jnp.dot(lo, ones_blockdiag, preferred_element_type=F32))
    return q * lax.rsqrt(ss * (1.0 / HEAD_DIM) + EPS) * gain


def _proj_kernel(x_ref, g_ref, w_ref, cosa_ref, sina_ref, cos1_ref, sin1_ref, qg_ref, kg_ref, bd_ref,
                 qa_ref, ka_ref, vta_ref, qb_ref, kb_ref, vtb_ref, sga_ref, sgb_ref, *, d_model):
    x = x_ref[...]
    ms = jnp.mean(x * x, axis=-1, keepdims=True)
    h = (x * lax.rsqrt(ms + EPS) * g_ref[...]).astype(BF16)
    scale = HEAD_DIM ** -0.5 * LOG2_E

    def proj(c0, width):
        return jnp.dot(h, w_ref[:, c0:c0 + width], preferred_element_type=F32)

    cosa, sina = cosa_ref[...], sina_ref[...]
    cos1, sin1 = cos1_ref[...], sin1_ref[...]
    bd = bd_ref[...]
    slab = lambda v, s: v[:, s * LANES:(s + 1) * LANES]
    c = 0
    qa = proj(c, Q_COLS)
    for s in range(Q_COLS // LANES):
        q = _head_rmsnorm(slab(qa, s), bd, qg_ref[...])
        qa_ref[:, s * LANES:(s + 1) * LANES] = (_rope(q, cosa, sina) * scale).astype(BF16)
    c += Q_COLS
    kva = proj(c, 2 * KV_COLS)
    ka = _rope(_head_rmsnorm(slab(kva, 0), bd, kg_ref[...]), cosa, sina).astype(BF16)
    va_t = slab(kva, 1).T
    ones = jnp.ones((V_ROWS - HEAD_DIM, va_t.shape[1]), BF16)
    for hd in range(KV_HEADS):
        ka_ref[hd] = ka[:, hd * HEAD_DIM:(hd + 1) * HEAD_DIM]
        vta_ref[hd, :HEAD_DIM, :] = va_t[hd * HEAD_DIM:(hd + 1) * HEAD_DIM, :].astype(BF16)
        vta_ref[hd, HEAD_DIM:, :] = ones
    c += 2 * KV_COLS
    qb = proj(c, Q_COLS)
    for s in range(Q_COLS // LANES):
        qb_ref[:, s * LANES:(s + 1) * LANES] = (_rope(slab(qb, s), cos1, sin1) * scale).astype(BF16)
    c += Q_COLS
    kvb = proj(c, 2 * KV_COLS)
    kb = _rope(slab(kvb, 0), cos1, sin1).astype(BF16)
    vb_t = slab(kvb, 1).T
    ones = jnp.ones((V_ROWS - HEAD_DIM, WINDOW), BF16)
    for hd in range(KV_HEADS):
        kb_ref[hd] = kb[:, hd * HEAD_DIM:(hd + 1) * HEAD_DIM]
        for ch in range(vtb_ref.shape[1]):
            vtb_ref[hd, ch, :HEAD_DIM, :] = vb_t[hd * HEAD_DIM:(hd + 1) * HEAD_DIM,
                                                 ch * WINDOW:(ch + 1) * WINDOW].astype(BF16)
            vtb_ref[hd, ch, HEAD_DIM:, :] = ones
    c += 2 * KV_COLS
    sga_ref[...] = jax.nn.sigmoid(proj(c, d_model)).astype(BF16)
    c += d_model
    sgb_ref[...] = jax.nn.sigmoid(proj(c, d_model)).astype(BF16)


def _rope_tables(seq):
    def slab(ang):
        cos, sin = jnp.cos(ang), jnp.sin(ang)
        cos64 = jnp.concatenate([cos, cos], axis=-1)
        sin64 = jnp.concatenate([-sin, sin], axis=-1)
        return jnp.tile(cos64, (1, 2)), jnp.tile(sin64, (1, 2))

    pos = jnp.arange(seq, dtype=F32)
    inv1 = ROPE_THETA ** (-jnp.arange(0, HEAD_DIM, 2, dtype=F32) / HEAD_DIM)
    ang1 = pos[:, None] * inv1[None, :]
    rows = seq // GRID_W
    row = jnp.broadcast_to(jnp.arange(rows, dtype=F32)[:, None], (rows, GRID_W)).reshape(-1)
    col = jnp.broadcast_to(jnp.arange(GRID_W, dtype=F32)[None, :], (rows, GRID_W)).reshape(-1)
    axis_dim = HEAD_DIM // 2
    inva = ROPE_THETA ** (-jnp.arange(0, axis_dim, 2, dtype=F32) / axis_dim)
    anga = jnp.concatenate([row[:, None] * inva[None, :], col[:, None] * inva[None, :]], axis=-1)
    return slab(anga), slab(ang1)


def _project(x2d, g_mix, w_in, qa_norm, ka_norm, seq, tm, tk):
    tokens, d_model = x2d.shape
    in_cols = w_in.shape[1]
    (cosa, sina), (cos1, sin1) = _rope_tables(seq)
    head_id = jnp.arange(LANES) // HEAD_DIM
    ones_blockdiag = (head_id[:, None] == head_id[None, :]).astype(BF16)
    qg = jnp.tile(qa_norm.astype(F32), 2)[None, :]
    kg = jnp.tile(ka_norm.astype(F32), 2)[None, :]
    seq_tiles = seq // tm
    row = lambda i: (i, 0)
    pos = lambda i: (i % seq_tiles, 0)
    fixed = lambda i: (0, 0)
    batch = tokens // seq
    assert tk % tm == 0 and tm % WINDOW == 0
    sub = tk // tm
    bf = lambda *shape: jax.ShapeDtypeStruct(shape, BF16)
    tok_spec = lambda cols: pl.BlockSpec((tm, cols), row)
    k_spec = pl.BlockSpec((None, KV_HEADS, tm, HEAD_DIM), lambda i: (i // seq_tiles, 0, i % seq_tiles, 0))
    vta_spec = pl.BlockSpec((None, KV_HEADS, None, V_ROWS, tm),
                            lambda i: (i // seq_tiles, 0, (i % seq_tiles) // sub, 0, (i % seq_tiles) % sub))
    vtb_spec = pl.BlockSpec((None, KV_HEADS, tm // WINDOW, V_ROWS, WINDOW),
                            lambda i: (i // seq_tiles, 0, i % seq_tiles, 0, 0))
    k_shape = bf(batch, KV_HEADS, seq, HEAD_DIM)
    return pl.pallas_call(
        functools.partial(_proj_kernel, d_model=d_model),
        grid=(tokens // tm,),
        in_specs=[
            pl.BlockSpec((tm, d_model), row),
            pl.BlockSpec((1, d_model), fixed),
            pl.BlockSpec((d_model, in_cols), fixed),
            pl.BlockSpec((tm, LANES), pos), pl.BlockSpec((tm, LANES), pos),
            pl.BlockSpec((tm, LANES), pos), pl.BlockSpec((tm, LANES), pos),
            pl.BlockSpec((1, LANES), fixed), pl.BlockSpec((1, LANES), fixed),
            pl.BlockSpec((LANES, LANES), fixed),
        ],
        out_specs=[tok_spec(Q_COLS), k_spec, vta_spec, tok_spec(Q_COLS), k_spec, vtb_spec,
                   tok_spec(d_model), tok_spec(d_model)],
        out_shape=[bf(tokens, Q_COLS), k_shape, bf(batch, KV_HEADS, seq // tk, V_ROWS, tk),
                   bf(tokens, Q_COLS), k_shape, bf(batch, KV_HEADS, seq // WINDOW, V_ROWS, WINDOW),
                   bf(tokens, d_model), bf(tokens, d_model)],
        compiler_params=_cparams(("arbitrary",)),
        name="proj",
    )(x2d, g_mix.astype(F32)[None, :], w_in.astype(BF16), cosa, sina, cos1, sin1, qg, kg, ones_blockdiag)


def _stack_group_queries(q):
    return jnp.concatenate([q[:, g * HEAD_DIM:(g + 1) * HEAD_DIM] for g in range(GROUPS)], axis=0)


def _unstack_group_outputs(o_t, tq):
    pad = jnp.zeros((LANES - HEAD_DIM, o_t.shape[1]), o_t.dtype)
    o_pad = jnp.concatenate([o_t, pad], axis=0)
    heads = [o_pad[:, g * tq:(g + 1) * tq].T[:, :HEAD_DIM] for g in range(GROUPS)]
    return jnp.concatenate(heads, axis=1)


def _attn_global_kernel(q_ref, k_ref, vt_ref, o_ref, s0, s1, p0, p1, acc_ref, *, tq, tk, n_chunks):
    q_st = _stack_group_queries(q_ref[...])
    m_rows = q_st.shape[0]

    def scores(c, s_ref):
        k_c = k_ref[pl.ds(pl.multiple_of(c * tk, tk), tk), :]
        s_ref[...] = lax.dot_general(k_c, q_st, (((1,), (1,)), ((), ())), preferred_element_type=F32)

    def softmax(s_ref, p_ref, m_prev):
        m_new = jnp.maximum(m_prev, jnp.max(s_ref[...], axis=0, keepdims=True))
        p_ref[...] = jnp.exp2(s_ref[...] - m_new).astype(BF16)
        return m_new, jnp.exp2(m_prev - m_new)

    def values(c, p_ref, alpha):
        acc_ref[...] = acc_ref[...] * alpha + jnp.dot(vt_ref[c], p_ref[...], preferred_element_type=F32)

    scores(0, s0)
    p1[...] = jnp.zeros(p1.shape, BF16)
    acc_ref[...] = jnp.zeros(acc_ref.shape, F32)

    def pair(j, carry):
        m, alpha = carry
        c = 2 * j
        scores(c + 1, s1)
        m_a, alpha_a = softmax(s0, p0, m)
        values(jnp.maximum(c - 1, 0), p1, alpha)
        scores(jnp.minimum(c + 2, n_chunks - 1), s0)
        m_b, alpha_b = softmax(s1, p1, m_a)
        values(c, p0, alpha_a)
        return m_b, alpha_b

    init = (jnp.full((1, m_rows), NEG_INF, F32), jnp.ones((1, m_rows), F32))
    _, alpha = lax.fori_loop(0, n_chunks // 2, pair, init)
    values(n_chunks - 1, p1, alpha)
    acc = acc_ref[...]
    o_t = acc[:HEAD_DIM, :] / acc[HEAD_DIM:HEAD_DIM + 1, :]
    o_ref[...] = _unstack_group_outputs(o_t, tq).astype(o_ref.dtype)


def _attn_window_kernel(q_ref, k_ref, vt_ref, sink_ref, band_ref, o_ref, *, tq, n_blocks, blocks_per_step):
    span = 3 * tq
    sink = sink_ref[...]
    for i in range(blocks_per_step):
        n = pl.program_id(2) * blocks_per_step + i
        c0 = jnp.clip(n - 1, 0, n_blocks - 3)
        start = pl.multiple_of(c0 * tq, tq)
        q_st = _stack_group_queries(q_ref[i * tq:(i + 1) * tq, :])
        k_w = k_ref[pl.ds(start, span), :]
        s = lax.dot_general(k_w, q_st, (((1,), (1,)), ((), ())), preferred_element_type=F32)
        s = s + band_ref[n - c0]
        m = jnp.maximum(jnp.max(s, axis=0, keepdims=True), sink)
        p = jnp.exp2(s - m).astype(BF16)
        acc = jnp.dot(vt_ref[c0], p[:tq], preferred_element_type=F32)
        for j in range(1, 3):
            acc += jnp.dot(vt_ref[c0 + j], p[j * tq:(j + 1) * tq], preferred_element_type=F32)
        denom = acc[HEAD_DIM:HEAD_DIM + 1, :] + jnp.exp2(sink - m)
        o_ref[i * tq:(i + 1) * tq, :] = _unstack_group_outputs(acc[:HEAD_DIM, :] / denom, tq).astype(o_ref.dtype)


def _attention_global(q2d, k_hm, v_t, batch, seq, tq):
    tokens = q2d.shape[0]
    n_chunks, tk = v_t.shape[2], v_t.shape[4]
    assert n_chunks % 2 == 0, "the chunk pipeline advances two chunks per step"
    q_tiles = seq // tq
    m_rows = GROUPS * tq
    qmap = lambda b, h, i: (b * q_tiles + i, h)
    return pl.pallas_call(
        functools.partial(_attn_global_kernel, tq=tq, tk=tk, n_chunks=n_chunks),
        grid=(batch, KV_HEADS, q_tiles),
        in_specs=[
            pl.BlockSpec((tq, GROUP_COLS), qmap),
            pl.BlockSpec((None, None, seq, HEAD_DIM), lambda b, h, i: (b, h, 0, 0)),
            pl.BlockSpec((None, None, n_chunks, V_ROWS, tk), lambda b, h, i: (b, h, 0, 0, 0)),
        ],
        out_specs=pl.BlockSpec((tq, GROUP_COLS), qmap),
        out_shape=jax.ShapeDtypeStruct((tokens, Q_COLS), BF16),
        scratch_shapes=[pltpu.VMEM((tk, m_rows), F32), pltpu.VMEM((tk, m_rows), F32),
                        pltpu.VMEM((tk, m_rows), BF16), pltpu.VMEM((tk, m_rows), BF16),
                        pltpu.VMEM((V_ROWS, m_rows), F32)],
        compiler_params=_cparams(("arbitrary", "arbitrary", "arbitrary")),
        name="attn_global",
    )(q2d, k_hm, v_t)


def _attention_window(q2d, k_hm, v_t, sink, batch, seq):
    tokens = q2d.shape[0]
    tq = WINDOW
    n_blocks = seq // tq
    assert n_blocks >= 3
    m_rows = GROUPS * tq
    sink_rows = jnp.repeat(sink.astype(F32).reshape(KV_HEADS, GROUPS) * LOG2_E, tq, axis=1)[:, None, :]
    per_step = WINDOW_BLOCKS_PER_STEP if n_blocks % WINDOW_BLOCKS_PER_STEP == 0 else 1
    steps = n_blocks // per_step
    key_j = jnp.arange(3 * tq)[None, :, None]
    query = (jnp.arange(m_rows) % tq)[None, None, :] + jnp.arange(3)[:, None, None] * tq
    band = jnp.where(jnp.abs(key_j - query) <= WINDOW, 0.0, NEG_INF).astype(F32)
    qmap = lambda b, h, i: (b * steps + i, h)
    return pl.pallas_call(
        functools.partial(_attn_window_kernel, tq=tq, n_blocks=n_blocks, blocks_per_step=per_step),
        grid=(batch, KV_HEADS, steps),
        in_specs=[
            pl.BlockSpec((per_step * tq, GROUP_COLS), qmap),
            pl.BlockSpec((None, None, seq, HEAD_DIM), lambda b, h, i: (b, h, 0, 0)),
            pl.BlockSpec((None, None, n_blocks, V_ROWS, tq), lambda b, h, i: (b, h, 0, 0, 0)),
            pl.BlockSpec((None, 1, m_rows), lambda b, h, i: (h, 0, 0)),
            pl.BlockSpec((3, 3 * tq, m_rows), lambda b, h, i: (0, 0, 0)),
        ],
        out_specs=pl.BlockSpec((per_step * tq, GROUP_COLS), qmap),
        out_shape=jax.ShapeDtypeStruct((tokens, Q_COLS), BF16),
        compiler_params=_cparams(("arbitrary", "arbitrary", "arbitrary")),
        name="attn_window",
    )(q2d, k_hm, v_t, sink_rows, band)


def _store_token_tiles(ref, val):
    rows = val.shape[0]
    for j in range(val.shape[1] // LANES):
        ref[pl.ds(j, rows, stride=SUBLANES), :] = val[:, j * LANES:(j + 1) * LANES]


def _load_token_tiles(ref, rows):
    return jnp.concatenate([ref[pl.ds(j, rows, stride=SUBLANES), :] for j in range(SUBLANES)], axis=1)


ROUTE_ROWS = 2 * SUBLANES


def _top_k_route(logits_t):
    eid = lax.broadcasted_iota(jnp.int32, logits_t.shape, 0)
    work = logits_t
    vals, idxs, sels = [], [], []
    for _ in range(TOP_K):
        mx = jnp.max(work, axis=0, keepdims=True)
        ix = jnp.min(jnp.where(work == mx, eid, N_EXPERTS), axis=0, keepdims=True)
        sel = eid == ix
        work = jnp.where(sel, -jnp.inf, work)
        vals.append(mx)
        idxs.append(ix)
        sels.append(sel)
    return vals, idxs, sels


def _mid_kernel(oa_ref, ob_ref, sga_ref, sgb_ref, x_ref, wa_ref, wb_ref, wo_ref, g_ref, wr_ref, br_ref, tri_ref,
                x1_ref, h2_ref, route_ref, gate_ref, cnt_ref, *, halves):
    @pl.when(pl.program_id(0) == 0)
    def _():
        cnt_ref[...] = jnp.zeros(cnt_ref.shape, F32)

    tm = x_ref.shape[0]
    hm = tm // halves
    routed = []
    for hf in range(halves):
        rs = slice(hf * hm, (hf + 1) * hm)
        a = jnp.dot(oa_ref[rs, :], wa_ref[...], preferred_element_type=F32)
        b = jnp.dot(ob_ref[rs, :], wb_ref[...], preferred_element_type=F32)
        merged = sga_ref[rs, :].astype(F32) * a + sgb_ref[rs, :].astype(F32) * b
        x1 = x_ref[rs, :] + jnp.dot(merged.astype(BF16), wo_ref[...], preferred_element_type=F32)
        x1_ref[rs, :] = x1
        ms = jnp.mean(x1 * x1, axis=-1, keepdims=True)
        h2 = x1 * lax.rsqrt(ms + EPS) * g_ref[...]
        _store_token_tiles(h2_ref.at[pl.ds(hf * hm * SUBLANES, hm * SUBLANES)], h2)
        hi = h2.astype(BF16)
        lo = (h2 - hi.astype(F32)).astype(BF16)
        nt = (((1,), (1,)), ((), ()))
        r2 = (lax.dot_general(wr_ref[...], hi, nt, preferred_element_type=F32)
              + lax.dot_general(wr_ref[...], lo, nt, preferred_element_type=F32))
        bias = jnp.concatenate([br_ref[...]] * (hm // LANES), axis=1)
        routed.append(_top_k_route(r2[:N_EXPERTS] + r2[N_EXPERTS:] + bias))

    onehots = []
    for vals, idxs, sels in routed:
        chosen = sels[0] | sels[1] | sels[2] | sels[3]
        onehots.append(jnp.where(chosen, 1.0, 0.0))
    onehot = jnp.concatenate(onehots, axis=1)
    counted = jnp.concatenate([cnt_ref[...]] * (tm // LANES), axis=1)
    before = jnp.dot(onehot.astype(BF16), tri_ref[...], preferred_element_type=F32) + counted
    zeros4 = jnp.zeros((SUBLANES - TOP_K, hm), jnp.int32)
    for hf, (vals, idxs, sels) in enumerate(routed):
        rs = slice(hf * hm, (hf + 1) * hm)
        exps = [jnp.exp(v - vals[0]) for v in vals]
        denom = exps[0] + exps[1] + exps[2] + exps[3]
        ranks = [jnp.sum(jnp.where(sels[k], before[:, rs], 0.0), axis=0, keepdims=True).astype(jnp.int32)
                 for k in range(TOP_K)]
        route_ref[:, rs] = jnp.concatenate(idxs + [zeros4] + ranks + [zeros4], axis=0)
        gates_t = jnp.concatenate([e / denom for e in exps] + [jnp.zeros((LANES - TOP_K, hm), F32)], axis=0)
        gate_ref[rs, :] = gates_t.T
    cnt_ref[...] = cnt_ref[...] + jnp.sum(onehot, axis=1, keepdims=True)


def _mid(oa, ob, sga, sgb, x2d, w_a, w_b, w_o, g_ffn, w_router, b_router, tm):
    tokens, d_model = x2d.shape
    wr_t = w_router.astype(F32).T
    wr_hi = wr_t.astype(BF16)
    wr_lo = (wr_t - wr_hi.astype(F32)).astype(BF16)
    wr_split = jnp.concatenate([wr_hi, wr_lo], axis=0)
    br = jnp.broadcast_to(b_router.astype(F32)[:, None], (N_EXPERTS, LANES))
    tri = (jnp.arange(tm)[:, None] < jnp.arange(tm)[None, :]).astype(BF16)
    halves = 2 if tm % (2 * LANES) == 0 else 1
    row = lambda i: (i, 0)
    fixed = lambda i: (0, 0)
    return pl.pallas_call(
        functools.partial(_mid_kernel, halves=halves),
        grid=(tokens // tm,),
        in_specs=[
            pl.BlockSpec((tm, Q_COLS), row), pl.BlockSpec((tm, Q_COLS), row),
            pl.BlockSpec((tm, d_model), row), pl.BlockSpec((tm, d_model), row),
            pl.BlockSpec((tm, d_model), row),
            pl.BlockSpec((Q_COLS, d_model), fixed), pl.BlockSpec((Q_COLS, d_model), fixed),
            pl.BlockSpec((d_model, d_model), fixed),
            pl.BlockSpec((1, d_model), fixed),
            pl.BlockSpec((2 * N_EXPERTS, d_model), fixed),
            pl.BlockSpec((N_EXPERTS, LANES), fixed),
            pl.BlockSpec((tm, tm), fixed),
        ],
        out_specs=[
            pl.BlockSpec((tm, d_model), row),
            pl.BlockSpec((tm * SUBLANES, LANES), row),
            pl.BlockSpec((ROUTE_ROWS, tm), lambda i: (0, i)),
            pl.BlockSpec((tm, LANES), row),
            pl.BlockSpec((N_EXPERTS, LANES), fixed),
        ],
        out_shape=[
            jax.ShapeDtypeStruct((tokens, d_model), F32),
            jax.ShapeDtypeStruct((tokens * SUBLANES, LANES), F32),
            jax.ShapeDtypeStruct((ROUTE_ROWS, tokens), jnp.int32),
            jax.ShapeDtypeStruct((tokens, LANES), F32),
            jax.ShapeDtypeStruct((N_EXPERTS, LANES), F32),
        ],
        compiler_params=_cparams(("arbitrary",)),
        name="mid",
    )(oa, ob, sga, sgb, x2d, w_a.astype(BF16), w_b.astype(BF16), w_o.astype(BF16),
      g_ffn.astype(F32)[None, :], wr_split, br, tri)


def _invert_kernel(gap_lo_ref, gap_hi_ref, dest_ref, out_ref, *, tokens):
    def clear(r, carry):
        out_ref[r] = -1
        return carry
    for g in range(gap_lo_ref.shape[0]):
        lax.fori_loop(gap_lo_ref[g], gap_hi_ref[g], clear, 0)
    for k in range(TOP_K):
        def put(t, carry):
            out_ref[dest_ref[k * tokens + t]] = t * TOP_K + k
            return carry
        lax.fori_loop(0, tokens, put, 0, unroll=16)


def _invert_routes(dest_flat, gap_lo, gap_hi, n_rows, tokens):
    smem = pl.BlockSpec(memory_space=pltpu.SMEM)
    return pl.pallas_call(
        functools.partial(_invert_kernel, tokens=tokens),
        in_specs=[smem, smem, smem],
        out_specs=smem,
        out_shape=jax.ShapeDtypeStruct((n_rows,), jnp.int32),
        name="invert_routes",
    )(gap_lo, gap_hi, dest_flat)


def _tile_copy(src_ref, src_row, dst_ref, dst_row, sem):
    return pltpu.make_async_copy(
        src_ref.at[pl.ds(pl.multiple_of(src_row, SUBLANES), SUBLANES)],
        dst_ref.at[pl.ds(pl.multiple_of(dst_row, SUBLANES), SUBLANES)], sem)


def _start_tile_gather(off_smem, base, n_rows, src_ref, dst_ref, sem):
    def start(r, carry):
        _tile_copy(src_ref, off_smem[base + r], dst_ref, r * SUBLANES, sem).start()
        return carry
    lax.fori_loop(0, n_rows, start, 0, unroll=8)


def _start_tile_scatter(off_smem, base, n_rows, src_ref, dst_ref, sem):
    def start(r, carry):
        _tile_copy(src_ref, r * SUBLANES, dst_ref, off_smem[base + r], sem).start()
        return carry
    lax.fori_loop(0, n_rows, start, 0, unroll=8)


def _wait_tiles(n_rows, src_ref, dst_ref, sem):
    window = pl.ds(0, n_rows * SUBLANES)
    pltpu.make_async_copy(src_ref.at[window], dst_ref.at[window], sem).wait()


SRC_WINDOWS = 3
DST_WINDOWS = 5
MOE_BUFFERS = 3
MOE_COL_TILES = 4
DMA_THREADS = 2


def _moe_kernel(bexp_ref, nused_ref, wslot_ref, wnext_ref, src_ref, dst_ref, h2_ref, wgu_ref, bgu_ref, wd_ref, bd_ref,
                y_ref, src_smem, dst_smem, xbuf, ybuf, wgu_f32, wd_f32, wgu_bf, wd_bf, isem, gsem, ssem, wsem,
                *, rows, n_assign):
    i = pl.program_id(0)
    n_used = nused_ref[0]
    cur = lax.rem(i, MOE_BUFFERS)
    far = lax.rem(i + 2, MOE_BUFFERS)
    src_base = lambda blk: lax.rem(blk, SRC_WINDOWS) * rows
    dst_base = lambda blk: lax.rem(blk, DST_WINDOWS) * rows

    def idx_copies(blk):
        sw = pl.ds(pl.multiple_of(src_base(blk), rows), rows)
        dw = pl.ds(pl.multiple_of(dst_base(blk), rows), rows)
        return (pltpu.make_async_copy(src_ref.at[blk], src_smem.at[sw], isem.at[0, lax.rem(blk, SRC_WINDOWS)]),
                pltpu.make_async_copy(dst_ref.at[blk], dst_smem.at[dw], isem.at[1, lax.rem(blk, DST_WINDOWS)]))

    def fetch_idx_now(blk):
        for c in idx_copies(blk):
            c.start()
        for c in idx_copies(blk):
            c.wait()

    def gather_copy(blk_base, r, s):
        return _tile_copy(h2_ref, src_smem[blk_base + r], xbuf.at[s], r * SUBLANES, gsem.at[s])

    def scatter_copy(blk_base, r, s):
        return _tile_copy(ybuf.at[s], r * SUBLANES, y_ref, dst_smem[blk_base + r], ssem.at[s])

    def wait_scatter(s):
        _wait_tiles(rows, ybuf.at[s], y_ref, ssem.at[s])

    def swiglu(g, u):
        gate = jnp.minimum(g, SWIGLU_LIMIT)
        up = jnp.clip(u, -SWIGLU_LIMIT, SWIGLU_LIMIT)
        return ((up + 1.0) * (gate * jax.nn.sigmoid(SWIGLU_ALPHA * gate))).astype(BF16)

    @pl.when(i == 0)
    def _():
        ybuf[...] = jnp.zeros(ybuf.shape, F32)
        for s in range(2):
            dump = pl.ds((n_assign + s * rows) * SUBLANES, rows * SUBLANES)
            zero_fill = pltpu.make_async_copy(ybuf.at[s], y_ref.at[dump], ssem.at[s])
            zero_fill.start()
            zero_fill.wait()
        fetch_idx_now(0)
        _start_tile_gather(src_smem, 0, rows, h2_ref, xbuf.at[0], gsem.at[0])

        @pl.when(1 < n_used)
        def _():
            fetch_idx_now(1)
            _start_tile_gather(src_smem, src_base(1), rows, h2_ref, xbuf.at[1], gsem.at[1])

        @pl.when(2 < n_used)
        def _():
            fetch_idx_now(2)

    @pl.when(i + 3 < n_used)
    def _():
        for c in idx_copies(i + 3):
            c.start()

    @pl.when(i < n_used)
    def _():
        _wait_tiles(rows, h2_ref, xbuf.at[cur], gsem.at[cur])

    @pl.when(jnp.logical_and(i >= 3, i <= n_used))
    def _():
        wait_scatter(cur)

    new_expert = jnp.logical_or(i == 0, bexp_ref[i] != bexp_ref[jnp.maximum(i - 1, 0)])

    def weight_copies(expert, s):
        return (pltpu.make_async_copy(wgu_ref.at[expert], wgu_f32.at[s], wsem.at[0, s]),
                pltpu.make_async_copy(wd_ref.at[expert], wd_f32.at[s], wsem.at[1, s]))

    @pl.when(jnp.logical_and(new_expert, i < n_used))
    def _():
        s = wslot_ref[i]

        @pl.when(i == 0)
        def _():
            for c in weight_copies(bexp_ref[0], 0):
                c.start()

        for c in weight_copies(bexp_ref[i], s):
            c.wait()
        wgu_bf[...] = wgu_f32[s].astype(BF16)
        wd_bf[...] = wd_f32[s].astype(BF16)

        @pl.when(wnext_ref[i] >= 0)
        def _():
            for c in weight_copies(wnext_ref[i], 1 - s):
                c.start()

    steady = jnp.logical_and(i >= 1, i + 2 < n_used)

    @pl.when(steady)
    def _():
        gbase, sbase = src_base(i + 2), dst_base(i - 1)
        burst = rows // MOE_COL_TILES
        width = D_FF // MOE_COL_TILES
        x = _load_token_tiles(xbuf.at[cur], rows).astype(BF16)
        acts = []
        for t in range(MOE_COL_TILES):
            cg = slice(t * width, (t + 1) * width)
            cu = slice(D_FF + t * width, D_FF + (t + 1) * width)
            g = jnp.dot(x, wgu_bf[:, cg], preferred_element_type=F32) + bgu_ref[:, cg]
            u = jnp.dot(x, wgu_bf[:, cu], preferred_element_type=F32) + bgu_ref[:, cu]
            for r in range(t * burst, (t + 1) * burst):
                gather_copy(gbase, r, far).start(priority=r % DMA_THREADS)
            acts.append(swiglu(g, u))
        act = jnp.concatenate(acts, axis=1)
        out_width = wd_bf.shape[1] // MOE_COL_TILES
        for t in range(MOE_COL_TILES):
            co = slice(t * out_width, (t + 1) * out_width)
            y_t = jnp.dot(act, wd_bf[:, co], preferred_element_type=F32) + bd_ref[:, co]
            for r in range(t * burst, (t + 1) * burst):
                scatter_copy(sbase, r, far).start(priority=r % DMA_THREADS)
            for j in range(out_width // LANES):
                ybuf[cur, pl.ds(t * (out_width // LANES) + j, rows, stride=SUBLANES), :] = (
                    y_t[:, j * LANES:(j + 1) * LANES])

    @pl.when(jnp.logical_and(i < n_used, jnp.logical_not(steady)))
    def _():
        @pl.when(i + 2 < n_used)
        def _():
            _start_tile_gather(src_smem, src_base(i + 2), rows, h2_ref, xbuf.at[far], gsem.at[far])

        x = _load_token_tiles(xbuf.at[cur], rows).astype(BF16)
        gu = jnp.dot(x, wgu_bf[...], preferred_element_type=F32) + bgu_ref[...]
        act = swiglu(gu[:, :D_FF], gu[:, D_FF:])
        y = jnp.dot(act, wd_bf[...], preferred_element_type=F32) + bd_ref[...]
        _store_token_tiles(ybuf.at[cur], y)

        @pl.when(i >= 1)
        def _():
            _start_tile_scatter(dst_smem, dst_base(i - 1), rows, ybuf.at[far], y_ref, ssem.at[far])

    @pl.when(i == n_used)
    def _():
        _start_tile_scatter(dst_smem, dst_base(i - 1), rows, ybuf.at[far], y_ref, ssem.at[far])

        @pl.when(i >= 2)
        def _():
            wait_scatter(lax.rem(i + 1, MOE_BUFFERS))

        wait_scatter(far)

    @pl.when(i + 3 < n_used)
    def _():
        for c in idx_copies(i + 3):
            c.wait()


def _moe(block_exp, n_used, weight_slot, next_exp, src_rows, dst_rows, h2_tiles, w_gate_up, b_gate_up, w_down,
         b_down, rows, n_assign):
    n_blocks = src_rows.shape[0]
    n_exp, d_model, ff2 = w_gate_up.shape
    by_expert = lambda i, be, nu, ws, ne: (be[i], 0, 0)
    hbm = pl.BlockSpec(memory_space=pl.ANY)
    grid_spec = pltpu.PrefetchScalarGridSpec(
        num_scalar_prefetch=4,
        grid=(n_blocks,),
        in_specs=[
            hbm, hbm, hbm,
            hbm,
            pl.BlockSpec((None, 1, ff2), by_expert),
            hbm,
            pl.BlockSpec((None, 1, d_model), by_expert),
        ],
        out_specs=hbm,
        scratch_shapes=[
            pltpu.SMEM((SRC_WINDOWS * rows,), jnp.int32),
            pltpu.SMEM((DST_WINDOWS * rows,), jnp.int32),
            pltpu.VMEM((MOE_BUFFERS, rows * SUBLANES, LANES), F32),
            pltpu.VMEM((MOE_BUFFERS, rows * SUBLANES, LANES), F32),
            pltpu.VMEM((2, d_model, ff2), F32),
            pltpu.VMEM((2, ff2 // 2, d_model), F32),
            pltpu.VMEM((d_model, ff2), BF16),
            pltpu.VMEM((ff2 // 2, d_model), BF16),
            pltpu.SemaphoreType.DMA((2, DST_WINDOWS)),
            pltpu.SemaphoreType.DMA((MOE_BUFFERS,)),
            pltpu.SemaphoreType.DMA((MOE_BUFFERS,)),
            pltpu.SemaphoreType.DMA((2, 2)),
        ],
    )
    return pl.pallas_call(
        functools.partial(_moe_kernel, rows=rows, n_assign=n_assign),
        grid_spec=grid_spec,
        out_shape=jax.ShapeDtypeStruct(((n_assign + 2 * rows) * SUBLANES, LANES), F32),
        compiler_params=_cparams(("arbitrary",)),
        name="moe",
    )(block_exp, n_used, weight_slot, next_exp, src_rows, dst_rows, h2_tiles, w_gate_up, b_gate_up[:, None, :],
      w_down, b_down[:, None, :])


def _combine_kernel(y_ref, gate_ref, x1_ref, g_ref, o_ref, *, tm):
    gate = gate_ref[...]
    out = x1_ref[...]
    stride = TOP_K * SUBLANES
    for k in range(TOP_K):
        y_k = jnp.concatenate(
            [y_ref[pl.ds(k * SUBLANES + j, tm, stride=stride), :] for j in range(SUBLANES)], axis=1)
        out = out + gate[:, k:k + 1] * y_k
    ms = jnp.mean(out * out, axis=-1, keepdims=True)
    o_ref[...] = out * lax.rsqrt(ms + EPS) * g_ref[...]


def _combine(y_tiles, gate, x1, g_final, tm):
    tokens, d_model = x1.shape
    row = lambda i: (i, 0)
    return pl.pallas_call(
        functools.partial(_combine_kernel, tm=tm),
        grid=(tokens // tm,),
        in_specs=[
            pl.BlockSpec((tm * TOP_K * SUBLANES, LANES), row),
            pl.BlockSpec((tm, LANES), row),
            pl.BlockSpec((tm, d_model), row),
            pl.BlockSpec((1, d_model), lambda i: (0, 0)),
        ],
        out_specs=pl.BlockSpec((tm, d_model), row),
        out_shape=jax.ShapeDtypeStruct((tokens, d_model), F32),
        compiler_params=_cparams(("arbitrary",)),
        name="combine",
    )(y_tiles, gate, x1, g_final.astype(F32)[None, :])


PROJ_ROWS = 512
MID_ROWS = 1024
ATTN_Q_ROWS = 1024
ATTN_K_ROWS = 512
WINDOW_BLOCKS_PER_STEP = 8
MOE_ROWS = 256
COMBINE_ROWS = 256


def _layer(x2d, batch, seq, g_mix, w_in, qa_norm, ka_norm, sink_b, w_branch_a, w_branch_b, w_out,
           g_ffn, w_router, b_router, w_gate_up, b_gate_up, w_down, b_down, g_final):
    tokens = x2d.shape[0]
    qa, ka, vta, qb, kb, vtb, sga, sgb = _project(x2d, g_mix, w_in, qa_norm, ka_norm, seq,
                                                  min(PROJ_ROWS, seq), min(ATTN_K_ROWS, seq // 2))
    oa = _attention_global(qa, ka, vta, batch, seq, ATTN_Q_ROWS)
    ob = _attention_window(qb, kb, vtb, sink_b, batch, seq)
    x1, h2_tiles, route, gate, cnt = _mid(
        oa, ob, sga, sgb, x2d, w_branch_a, w_branch_b, w_out, g_ffn, w_router, b_router, min(MID_ROWS, tokens))

    rows = MOE_ROWS
    n_assign = tokens * TOP_K
    n_blocks = n_assign // rows + N_EXPERTS
    counts = cnt[:, 0].astype(jnp.int32)
    padded = (counts + rows - 1) // rows * rows
    ends = jnp.cumsum(padded)
    starts = ends - padded
    eidx, rank = route[:TOP_K], route[SUBLANES:SUBLANES + TOP_K]
    seg_start = jnp.sum(jnp.where(eidx[None] == jnp.arange(N_EXPERTS)[:, None, None], starts[:, None, None], 0),
                        axis=0)
    dest = seg_start + rank
    gap_lo = jnp.concatenate([starts + counts, ends[-1:]]).astype(jnp.int32)
    gap_hi = jnp.concatenate([ends, jnp.full((1,), n_blocks * rows)]).astype(jnp.int32)
    row_assign = _invert_routes(dest.reshape(-1), gap_lo, gap_hi, n_blocks * rows, tokens).reshape(n_blocks, rows)
    real = row_assign >= 0
    dump_slot = (n_assign + (jnp.arange(n_blocks, dtype=jnp.int32) % 2)[:, None] * rows
                 + jnp.arange(rows, dtype=jnp.int32)[None, :])
    src_rows = jnp.where(real, row_assign // TOP_K, 0) * SUBLANES
    dst_rows = jnp.where(real, row_assign, dump_slot) * SUBLANES
    block_start = jnp.arange(n_blocks, dtype=jnp.int32) * rows
    block_exp = jnp.minimum(
        jnp.sum((ends[None, :] <= block_start[:, None]).astype(jnp.int32), axis=1), N_EXPERTS - 1)
    n_used = (ends[-1:] // rows).astype(jnp.int32)

    block_id = jnp.arange(n_blocks, dtype=jnp.int32)
    first_of_expert = jnp.logical_or(block_id == 0, block_exp != jnp.roll(block_exp, 1))
    weight_slot = ((jnp.cumsum(first_of_expert.astype(jnp.int32)) - 1) % 2).astype(jnp.int32)
    after_segment = ends[block_exp] // rows
    next_exp = jnp.where(after_segment < n_used[0],
                         block_exp[jnp.minimum(after_segment, n_blocks - 1)], -1).astype(jnp.int32)

    y_tiles = _moe(block_exp, n_used, weight_slot, next_exp, src_rows, dst_rows, h2_tiles,
                   w_gate_up, b_gate_up, w_down, b_down, rows, n_assign)
    return _combine(y_tiles, gate, x1, g_final, min(COMBINE_ROWS, tokens))


def kernel(x, g_mix, w_in, qa_norm, ka_norm, sink_b, w_branch_a, w_branch_b, w_out, g_ffn, w_router, b_router,
           w_gate_up, b_gate_up, w_down, b_down, g_final):
    batch, seq, d_model = x.shape
    assert g_mix.shape[0] == 1, "single-layer block"
    out = _layer(x.reshape(batch * seq, d_model), batch, seq, g_mix[0], w_in[0], qa_norm[0], ka_norm[0],
                 sink_b[0], w_branch_a[0], w_branch_b[0], w_out[0], g_ffn[0], w_router[0], b_router[0],
                 w_gate_up[0], b_gate_up[0], w_down[0], b_down[0], g_final)
    return out.reshape(batch, seq, d_model)
```

```python
import functools

import jax
import jax.numpy as jnp
from jax import lax
from jax.experimental import pallas as pl
from jax.experimental.pallas import tpu as pltpu

HEAD_DIM = 64
Q_HEADS = 8
KV_HEADS = 2
GROUPS = Q_HEADS // KV_HEADS
Q_COLS = Q_HEADS * HEAD_DIM
KV_COLS = KV_HEADS * HEAD_DIM
GROUP_COLS = GROUPS * HEAD_DIM
WINDOW = 128
GRID_W = 64
ROPE_THETA = 10000.0
N_EXPERTS = 32
TOP_K = 4
D_FF = 1024
SWIGLU_LIMIT = 7.0
SWIGLU_ALPHA = 1.702
EPS = 1e-6
NEG_INF = -1e30
LOG2_E = 1.4426950408889634

LANES = 128
SUBLANES = 8
V_ROWS = 80
VMEM_LIMIT = 56 * 1024 * 1024

F32 = jnp.float32
BF16 = jnp.bfloat16


def _cparams(sem, vmem=VMEM_LIMIT):
    return pltpu.CompilerParams(dimension_semantics=sem, vmem_limit_bytes=vmem)


def _rope(y, cos, sin_signed):
    lane = lax.broadcasted_iota(jnp.int32, y.shape, 1)
    first_half = (lane & 32) == 0
    rot = jnp.where(first_half, pltpu.roll(y, 96, 1), pltpu.roll(y, 32, 1))
    return y * cos + rot * sin_signed


def _head_rmsnorm(q, ones_blockdiag, gain):
    q2 = q * q
    hi = q2.astype(BF16)
    lo = (q2 - hi.astype(F32)).astype(BF16)
    ss = (jnp.dot(hi, ones_blockdiag, preferred_element_type=F32)
          + jnp.dot(lo, ones_blockdiag, preferred_element_type=F32))
    return q * lax.rsqrt(ss * (1.0 / HEAD_DIM) + EPS) * gain


def _proj_kernel(x_ref, g_ref, w_ref, cosa_ref, sina_ref, cos1_ref, sin1_ref, qg_ref, kg_ref, bd_ref,
                 qa_ref, ka_ref, vta_ref, qb_ref, kb_ref, vtb_ref, sga_ref, sgb_ref, *, d_model):
    x = x_ref[...]
    ms = jnp.mean(x * x, axis=-1, keepdims=True)
    h = (x * lax.rsqrt(ms + EPS) * g_ref[...]).astype(BF16)
    scale = HEAD_DIM ** -0.5 * LOG2_E

    def proj(c0, width):
        return jnp.dot(h, w_ref[:, c0:c0 + width], preferred_element_type=F32)

    cosa, sina = cosa_ref[...], sina_ref[...]
    cos1, sin1 = cos1_ref[...], sin1_ref[...]
    bd = bd_ref[...]
    slab = lambda v, s: v[:, s * LANES:(s + 1) * LANES]
    c = 0
    qa = proj(c, Q_COLS)
    for s in range(Q_COLS // LANES):
        q = _head_rmsnorm(slab(qa, s), bd, qg_ref[...])
        qa_ref[:, s * LANES:(s + 1) * LANES] = (_rope(q, cosa, sina) * scale).astype(BF16)
    c += Q_COLS
    kva = proj(c, 2 * KV_COLS)
    ka = _rope(_head_rmsnorm(slab(kva, 0), bd, kg_ref[...]), cosa, sina).astype(BF16)
    va_t = slab(kva, 1).T
    ones = jnp.ones((V_ROWS - HEAD_DIM, va_t.shape[1]), BF16)
    for hd in range(KV_HEADS):
        ka_ref[hd] = ka[:, hd * HEAD_DIM:(hd + 1) * HEAD_DIM]
        vta_ref[hd, :HEAD_DIM, :] = va_t[hd * HEAD_DIM:(hd + 1) * HEAD_DIM, :].astype(BF16)
        vta_ref[hd, HEAD_DIM:, :] = ones
    c += 2 * KV_COLS
    qb = proj(c, Q_COLS)
    for s in range(Q_COLS // LANES):
        qb_ref[:, s * LANES:(s + 1) * LANES] = (_rope(slab(qb, s), cos1, sin1) * scale).astype(BF16)
    c += Q_COLS
    kvb = proj(c, 2 * KV_COLS)
    kb = _rope(slab(kvb, 0), cos1, sin1).astype(BF16)
    vb_t = slab(kvb, 1).T
    ones = jnp.ones((V_ROWS - HEAD_DIM, WINDOW), BF16)
    for hd in range(KV_HEADS):
        kb_ref[hd] = kb[:, hd * HEAD_DIM:(hd + 1) * HEAD_DIM]
        for ch in range(vtb_ref.shape[1]):
            vtb_ref[hd, ch, :HEAD_DIM, :] = vb_t[hd * HEAD_DIM:(hd + 1) * HEAD_DIM,
                                                 ch * WINDOW:(ch + 1) * WINDOW].astype(BF16)
            vtb_ref[hd, ch, HEAD_DIM:, :] = ones
    c += 2 * KV_COLS
    sga_ref[...] = jax.nn.sigmoid(proj(c, d_model)).astype(BF16)
    c += d_model
    sgb_ref[...] = jax.nn.sigmoid(proj(c, d_model)).astype(BF16)


def _rope_tables(seq):
    def slab(ang):
        cos, sin = jnp.cos(ang), jnp.sin(ang)
        cos64 = jnp.concatenate([cos, cos], axis=-1)
        sin64 = jnp.concatenate([-sin, sin], axis=-1)
        return jnp.tile(cos64, (1, 2)), jnp.tile(sin64, (1, 2))

    pos = jnp.arange(seq, dtype=F32)
    inv1 = ROPE_THETA ** (-jnp.arange(0, HEAD_DIM, 2, dtype=F32) / HEAD_DIM)
    ang1 = pos[:, None] * inv1[None, :]
    rows = seq // GRID_W
    row = jnp.broadcast_to(jnp.arange(rows, dtype=F32)[:, None], (rows, GRID_W)).reshape(-1)
    col = jnp.broadcast_to(jnp.arange(GRID_W, dtype=F32)[None, :], (rows, GRID_W)).reshape(-1)
    axis_dim = HEAD_DIM // 2
    inva = ROPE_THETA ** (-jnp.arange(0, axis_dim, 2, dtype=F32) / axis_dim)
    anga = jnp.concatenate([row[:, None] * inva[None, :], col[:, None] * inva[None, :]], axis=-1)
    return slab(anga), slab(ang1)


def _project(x2d, g_mix, w_in, qa_norm, ka_norm, seq, tm, tk):
    tokens, d_model = x2d.shape
    in_cols = w_in.shape[1]
    (cosa, sina), (cos1, sin1) = _rope_tables(seq)
    head_id = jnp.arange(LANES) // HEAD_DIM
    ones_blockdiag = (head_id[:, None] == head_id[None, :]).astype(BF16)
    qg = jnp.tile(qa_norm.astype(F32), 2)[None, :]
    kg = jnp.tile(ka_norm.astype(F32), 2)[None, :]
    seq_tiles = seq // tm
    row = lambda i: (i, 0)
    pos = lambda i: (i % seq_tiles, 0)
    fixed = lambda i: (0, 0)
    batch = tokens // seq
    assert tk % tm == 0 and tm % WINDOW == 0
    sub = tk // tm
    bf = lambda *shape: jax.ShapeDtypeStruct(shape, BF16)
    tok_spec = lambda cols: pl.BlockSpec((tm, cols), row)
    k_spec = pl.BlockSpec((None, KV_HEADS, tm, HEAD_DIM), lambda i: (i // seq_tiles, 0, i % seq_tiles, 0))
    vta_spec = pl.BlockSpec((None, KV_HEADS, None, V_ROWS, tm),
                            lambda i: (i // seq_tiles, 0, (i % seq_tiles) // sub, 0, (i % seq_tiles) % sub))
    vtb_spec = pl.BlockSpec((None, KV_HEADS, tm // WINDOW, V_ROWS, WINDOW),
                            lambda i: (i // seq_tiles, 0, i % seq_tiles, 0, 0))
    k_shape = bf(batch, KV_HEADS, seq, HEAD_DIM)
    return pl.pallas_call(
        functools.partial(_proj_kernel, d_model=d_model),
        grid=(tokens // tm,),
        in_specs=[
            pl.BlockSpec((tm, d_model), row),
            pl.BlockSpec((1, d_model), fixed),
            pl.BlockSpec((d_model, in_cols), fixed),
            pl.BlockSpec((tm, LANES), pos), pl.BlockSpec((tm, LANES), pos),
            pl.BlockSpec((tm, LANES), pos), pl.BlockSpec((tm, LANES), pos),
            pl.BlockSpec((1, LANES), fixed), pl.BlockSpec((1, LANES), fixed),
            pl.BlockSpec((LANES, LANES), fixed),
        ],
        out_specs=[tok_spec(Q_COLS), k_spec, vta_spec, tok_spec(Q_COLS), k_spec, vtb_spec,
                   tok_spec(d_model), tok_spec(d_model)],
        out_shape=[bf(tokens, Q_COLS), k_shape, bf(batch, KV_HEADS, seq // tk, V_ROWS, tk),
                   bf(tokens, Q_COLS), k_shape, bf(batch, KV_HEADS, seq // WINDOW, V_ROWS, WINDOW),
                   bf(tokens, d_model), bf(tokens, d_model)],
        compiler_params=_cparams(("arbitrary",)),
        name="proj",
    )(x2d, g_mix.astype(F32)[None, :], w_in.astype(BF16), cosa, sina, cos1, sin1, qg, kg, ones_blockdiag)


def _stack_group_queries(q):
    return jnp.concatenate([q[:, g * HEAD_DIM:(g + 1) * HEAD_DIM] for g in range(GROUPS)], axis=0)


def _unstack_group_outputs(o_t, tq):
    pad = jnp.zeros((LANES - HEAD_DIM, o_t.shape[1]), o_t.dtype)
    o_pad = jnp.concatenate([o_t, pad], axis=0)
    heads = [o_pad[:, g * tq:(g + 1) * tq].T[:, :HEAD_DIM] for g in range(GROUPS)]
    return jnp.concatenate(heads, axis=1)


def _attn_global_kernel(q_ref, k_ref, vt_ref, o_ref, s0, s1, p0, p1, acc_ref, *, tq, tk, n_chunks):
    q_st = _stack_group_queries(q_ref[...])
    m_rows = q_st.shape[0]

    def scores(c, s_ref):
        k_c = k_ref[pl.ds(pl.multiple_of(c * tk, tk), tk), :]
        s_ref[...] = lax.dot_general(k_c, q_st, (((1,), (1,)), ((), ())), preferred_element_type=F32)

    def softmax(s_ref, p_ref, m_prev):
        m_new = jnp.maximum(m_prev, jnp.max(s_ref[...], axis=0, keepdims=True))
        p_ref[...] = jnp.exp2(s_ref[...] - m_new).astype(BF16)
        return m_new, jnp.exp2(m_prev - m_new)

    def values(c, p_ref, alpha):
        acc_ref[...] = acc_ref[...] * alpha + jnp.dot(vt_ref[c], p_ref[...], preferred_element_type=F32)

    scores(0, s0)
    p1[...] = jnp.zeros(p1.shape, BF16)
    acc_ref[...] = jnp.zeros(acc_ref.shape, F32)

    def pair(j, carry, more_chunks=True):
        m, alpha = carry
        c = 2 * j
        scores(c + 1, s1)
        m_a, alpha_a = softmax(s0, p0, m)
        values(jnp.maximum(c - 1, 0), p1, alpha)
        if more_chunks:
            scores(c + 2, s0)
        m_b, alpha_b = softmax(s1, p1, m_a)
        values(c, p0, alpha_a)
        return m_b, alpha_b

    init = (jnp.full((1, m_rows), NEG_INF, F32), jnp.ones((1, m_rows), F32))
    carry = lax.fori_loop(0, n_chunks // 2 - 1, pair, init)
    _, alpha = pair(n_chunks // 2 - 1, carry, more_chunks=False)
    values(n_chunks - 1, p1, alpha)
    acc = acc_ref[...]
    o_t = acc[:HEAD_DIM, :] / acc[HEAD_DIM:HEAD_DIM + 1, :]
    o_ref[...] = _unstack_group_outputs(o_t, tq).astype(o_ref.dtype)


def _attn_window_kernel(q_ref, k_ref, vt_ref, sink_ref, band_ref, o_ref, *, tq, n_blocks, blocks_per_step):
    span = 3 * tq
    sink = sink_ref[...]
    for i in range(blocks_per_step):
        n = pl.program_id(2) * blocks_per_step + i
        c0 = jnp.clip(n - 1, 0, n_blocks - 3)
        start = pl.multiple_of(c0 * tq, tq)
        q_st = _stack_group_queries(q_ref[i * tq:(i + 1) * tq, :])
        k_w = k_ref[pl.ds(start, span), :]
        s = lax.dot_general(k_w, q_st, (((1,), (1,)), ((), ())), preferred_element_type=F32)
        s = s + band_ref[n - c0]
        m = jnp.maximum(jnp.max(s, axis=0, keepdims=True), sink)
        p = jnp.exp2(s - m).astype(BF16)
        acc = jnp.dot(vt_ref[c0], p[:tq], preferred_element_type=F32)
        for j in range(1, 3):
            acc += jnp.dot(vt_ref[c0 + j], p[j * tq:(j + 1) * tq], preferred_element_type=F32)
        denom = acc[HEAD_DIM:HEAD_DIM + 1, :] + jnp.exp2(sink - m)
        o_ref[i * tq:(i + 1) * tq, :] = _unstack_group_outputs(acc[:HEAD_DIM, :] / denom, tq).astype(o_ref.dtype)


def _attention_global(q2d, k_hm, v_t, batch, seq, tq):
    tokens = q2d.shape[0]
    n_chunks, tk = v_t.shape[2], v_t.shape[4]
    assert n_chunks % 2 == 0, "the chunk pipeline advances two chunks per step"
    q_tiles = seq // tq
    m_rows = GROUPS * tq
    qmap = lambda b, h, i: (b * q_tiles + i, h)
    return pl.pallas_call(
        functools.partial(_attn_global_kernel, tq=tq, tk=tk, n_chunks=n_chunks),
        grid=(batch, KV_HEADS, q_tiles),
        in_specs=[
            pl.BlockSpec((tq, GROUP_COLS), qmap),
            pl.BlockSpec((None, None, seq, HEAD_DIM), lambda b, h, i: (b, h, 0, 0)),
            pl.BlockSpec((None, None, n_chunks, V_ROWS, tk), lambda b, h, i: (b, h, 0, 0, 0)),
        ],
        out_specs=pl.BlockSpec((tq, GROUP_COLS), qmap),
        out_shape=jax.ShapeDtypeStruct((tokens, Q_COLS), BF16),
        scratch_shapes=[pltpu.VMEM((tk, m_rows), F32), pltpu.VMEM((tk, m_rows), F32),
                        pltpu.VMEM((tk, m_rows), BF16), pltpu.VMEM((tk, m_rows), BF16),
                        pltpu.VMEM((V_ROWS, m_rows), F32)],
        compiler_params=_cparams(("arbitrary", "arbitrary", "arbitrary")),
        name="attn_global",
    )(q2d, k_hm, v_t)


def _attention_window(q2d, k_hm, v_t, sink, batch, seq):
    tokens = q2d.shape[0]
    tq = WINDOW
    n_blocks = seq // tq
    assert n_blocks >= 3
    m_rows = GROUPS * tq
    sink_rows = jnp.repeat(sink.astype(F32).reshape(KV_HEADS, GROUPS) * LOG2_E, tq, axis=1)[:, None, :]
    per_step = WINDOW_BLOCKS_PER_STEP if n_blocks % WINDOW_BLOCKS_PER_STEP == 0 else 1
    steps = n_blocks // per_step
    key_j = jnp.arange(3 * tq)[None, :, None]
    query = (jnp.arange(m_rows) % tq)[None, None, :] + jnp.arange(3)[:, None, None] * tq
    band = jnp.where(jnp.abs(key_j - query) <= WINDOW, 0.0, NEG_INF).astype(F32)
    qmap = lambda b, h, i: (b * steps + i, h)
    return pl.pallas_call(
        functools.partial(_attn_window_kernel, tq=tq, n_blocks=n_blocks, blocks_per_step=per_step),
        grid=(batch, KV_HEADS, steps),
        in_specs=[
            pl.BlockSpec((per_step * tq, GROUP_COLS), qmap),
            pl.BlockSpec((None, None, seq, HEAD_DIM), lambda b, h, i: (b, h, 0, 0)),
            pl.BlockSpec((None, None, n_blocks, V_ROWS, tq), lambda b, h, i: (b, h, 0, 0, 0)),
            pl.BlockSpec((None, 1, m_rows), lambda b, h, i: (h, 0, 0)),
            pl.BlockSpec((3, 3 * tq, m_rows), lambda b, h, i: (0, 0, 0)),
        ],
        out_specs=pl.BlockSpec((per_step * tq, GROUP_COLS), qmap),
        out_shape=jax.ShapeDtypeStruct((tokens, Q_COLS), BF16),
        compiler_params=_cparams(("arbitrary", "arbitrary", "arbitrary")),
        name="attn_window",
    )(q2d, k_hm, v_t, sink_rows, band)


def _store_token_tiles(ref, val):
    rows = val.shape[0]
    for j in range(val.shape[1] // LANES):
        ref[pl.ds(j, rows, stride=SUBLANES), :] = val[:, j * LANES:(j + 1) * LANES]


def _load_token_tiles(ref, rows):
    return jnp.concatenate([ref[pl.ds(j, rows, stride=SUBLANES), :] for j in range(SUBLANES)], axis=1)


ROUTE_ROWS = 2 * SUBLANES


def _top_k_route(logits_t):
    eid = lax.broadcasted_iota(jnp.int32, logits_t.shape, 0)
    work = logits_t
    vals, idxs, sels = [], [], []
    for _ in range(TOP_K):
        mx = jnp.max(work, axis=0, keepdims=True)
        ix = jnp.min(jnp.where(work == mx, eid, N_EXPERTS), axis=0, keepdims=True)
        sel = eid == ix
        work = jnp.where(sel, -jnp.inf, work)
        vals.append(mx)
        idxs.append(ix)
        sels.append(sel)
    return vals, idxs, sels


def _mid_kernel(oa_ref, ob_ref, sga_ref, sgb_ref, x_ref, wa_ref, wb_ref, wo_ref, g_ref, wr_ref, br_ref, tri_ref,
                x1_ref, h2_ref, route_ref, gate_ref, cnt_ref, *, halves):
    @pl.when(pl.program_id(0) == 0)
    def _():
        cnt_ref[...] = jnp.zeros(cnt_ref.shape, F32)

    tm = x_ref.shape[0]
    hm = tm // halves
    routed = []
    for hf in range(halves):
        rs = slice(hf * hm, (hf + 1) * hm)
        a = jnp.dot(oa_ref[rs, :], wa_ref[...], preferred_element_type=F32)
        b = jnp.dot(ob_ref[rs, :], wb_ref[...], preferred_element_type=F32)
        merged = sga_ref[rs, :].astype(F32) * a + sgb_ref[rs, :].astype(F32) * b
        x1 = x_ref[rs, :] + jnp.dot(merged.astype(BF16), wo_ref[...], preferred_element_type=F32)
        x1_ref[rs, :] = x1
        ms = jnp.mean(x1 * x1, axis=-1, keepdims=True)
        h2 = x1 * lax.rsqrt(ms + EPS) * g_ref[...]
        _store_token_tiles(h2_ref.at[pl.ds(hf * hm * SUBLANES, hm * SUBLANES)], h2)
        hi = h2.astype(BF16)
        lo = (h2 - hi.astype(F32)).astype(BF16)
        nt = (((1,), (1,)), ((), ()))
        r2 = (lax.dot_general(wr_ref[...], hi, nt, preferred_element_type=F32)
              + lax.dot_general(wr_ref[...], lo, nt, preferred_element_type=F32))
        bias = jnp.concatenate([br_ref[...]] * (hm // LANES), axis=1)
        routed.append(_top_k_route(r2[:N_EXPERTS] + r2[N_EXPERTS:] + bias))

    onehots = []
    for vals, idxs, sels in routed:
        chosen = sels[0] | sels[1] | sels[2] | sels[3]
        onehots.append(jnp.where(chosen, 1.0, 0.0))
    onehot = jnp.concatenate(onehots, axis=1)
    counted = jnp.concatenate([cnt_ref[...]] * (tm // LANES), axis=1)
    before = jnp.dot(onehot.astype(BF16), tri_ref[...], preferred_element_type=F32) + counted
    zeros4 = jnp.zeros((SUBLANES - TOP_K, hm), jnp.int32)
    for hf, (vals, idxs, sels) in enumerate(routed):
        rs = slice(hf * hm, (hf + 1) * hm)
        exps = [jnp.exp(v - vals[0]) for v in vals]
        denom = exps[0] + exps[1] + exps[2] + exps[3]
        ranks = [jnp.sum(jnp.where(sels[k], before[:, rs], 0.0), axis=0, keepdims=True).astype(jnp.int32)
                 for k in range(TOP_K)]
        route_ref[:, rs] = jnp.concatenate(idxs + [zeros4] + ranks + [zeros4], axis=0)
        gates_t = jnp.concatenate([e / denom for e in exps] + [jnp.zeros((LANES - TOP_K, hm), F32)], axis=0)
        gate_ref[rs, :] = gates_t.T
    cnt_ref[...] = cnt_ref[...] + jnp.sum(onehot, axis=1, keepdims=True)


def _mid(oa, ob, sga, sgb, x2d, w_a, w_b, w_o, g_ffn, w_router, b_router, tm):
    tokens, d_model = x2d.shape
    wr_t = w_router.astype(F32).T
    wr_hi = wr_t.astype(BF16)
    wr_lo = (wr_t - wr_hi.astype(F32)).astype(BF16)
    wr_split = jnp.concatenate([wr_hi, wr_lo], axis=0)
    br = jnp.broadcast_to(b_router.astype(F32)[:, None], (N_EXPERTS, LANES))
    tri = (jnp.arange(tm)[:, None] < jnp.arange(tm)[None, :]).astype(BF16)
    halves = 2 if tm % (2 * LANES) == 0 else 1
    row = lambda i: (i, 0)
    fixed = lambda i: (0, 0)
    return pl.pallas_call(
        functools.partial(_mid_kernel, halves=halves),
        grid=(tokens // tm,),
        in_specs=[
            pl.BlockSpec((tm, Q_COLS), row), pl.BlockSpec((tm, Q_COLS), row),
            pl.BlockSpec((tm, d_model), row), pl.BlockSpec((tm, d_model), row),
            pl.BlockSpec((tm, d_model), row),
            pl.BlockSpec((Q_COLS, d_model), fixed), pl.BlockSpec((Q_COLS, d_model), fixed),
            pl.BlockSpec((d_model, d_model), fixed),
            pl.BlockSpec((1, d_model), fixed),
            pl.BlockSpec((2 * N_EXPERTS, d_model), fixed),
            pl.BlockSpec((N_EXPERTS, LANES), fixed),
            pl.BlockSpec((tm, tm), fixed),
        ],
        out_specs=[
            pl.BlockSpec((tm, d_model), row),
            pl.BlockSpec((tm * SUBLANES, LANES), row),
            pl.BlockSpec((ROUTE_ROWS, tm), lambda i: (0, i)),
            pl.BlockSpec((tm, LANES), row),
            pl.BlockSpec((N_EXPERTS, LANES), fixed),
        ],
        out_shape=[
            jax.ShapeDtypeStruct((tokens, d_model), F32),
            jax.ShapeDtypeStruct((tokens * SUBLANES, LANES), F32),
            jax.ShapeDtypeStruct((ROUTE_ROWS, tokens), jnp.int32),
            jax.ShapeDtypeStruct((tokens, LANES), F32),
            jax.ShapeDtypeStruct((N_EXPERTS, LANES), F32),
        ],
        compiler_params=_cparams(("arbitrary",)),
        name="mid",
    )(oa, ob, sga, sgb, x2d, w_a.astype(BF16), w_b.astype(BF16), w_o.astype(BF16),
      g_ffn.astype(F32)[None, :], wr_split, br, tri)


def _invert_kernel(gap_lo_ref, gap_hi_ref, dest_ref, out_ref, *, tokens):
    def clear(r, carry):
        out_ref[r] = -1
        return carry
    for g in range(gap_lo_ref.shape[0]):
        lax.fori_loop(gap_lo_ref[g], gap_hi_ref[g], clear, 0)
    for k in range(TOP_K):
        def put(t, carry):
            out_ref[dest_ref[k * tokens + t]] = t * TOP_K + k
            return carry
        lax.fori_loop(0, tokens, put, 0, unroll=16)


def _invert_routes(dest_flat, gap_lo, gap_hi, n_rows, tokens):
    smem = pl.BlockSpec(memory_space=pltpu.SMEM)
    return pl.pallas_call(
        functools.partial(_invert_kernel, tokens=tokens),
        in_specs=[smem, smem, smem],
        out_specs=smem,
        out_shape=jax.ShapeDtypeStruct((n_rows,), jnp.int32),
        name="invert_routes",
    )(gap_lo, gap_hi, dest_flat)


def _tile_copy(src_ref, src_row, dst_ref, dst_row, sem):
    return pltpu.make_async_copy(
        src_ref.at[pl.ds(pl.multiple_of(src_row, SUBLANES), SUBLANES)],
        dst_ref.at[pl.ds(pl.multiple_of(dst_row, SUBLANES), SUBLANES)], sem)


def _start_tile_gather(off_smem, base, n_rows, src_ref, dst_ref, sem):
    def start(r, carry):
        _tile_copy(src_ref, off_smem[base + r], dst_ref, r * SUBLANES, sem).start()
        return carry
    lax.fori_loop(0, n_rows, start, 0, unroll=8)


def _start_tile_scatter(off_smem, base, n_rows, src_ref, dst_ref, sem):
    def start(r, carry):
        _tile_copy(src_ref, r * SUBLANES, dst_ref, off_smem[base + r], sem).start()
        return carry
    lax.fori_loop(0, n_rows, start, 0, unroll=8)


def _wait_tiles(n_rows, src_ref, dst_ref, sem):
    window = pl.ds(0, n_rows * SUBLANES)
    pltpu.make_async_copy(src_ref.at[window], dst_ref.at[window], sem).wait()


SRC_WINDOWS = 3
DST_WINDOWS = 5
MOE_BUFFERS = 3
MOE_COL_TILES = 4
DMA_THREADS = 2


def _moe_kernel(bexp_ref, nused_ref, wslot_ref, wnext_ref, src_ref, dst_ref, h2_ref, wgu_ref, bgu_ref, wd_ref, bd_ref,
                y_ref, src_smem, dst_smem, xbuf, ybuf, wgu_f32, wd_f32, wgu_bf, wd_bf, isem, gsem, ssem, wsem,
                *, rows, n_assign):
    i = pl.program_id(0)
    n_used = nused_ref[0]
    cur = lax.rem(i, MOE_BUFFERS)
    far = lax.rem(i + 2, MOE_BUFFERS)
    src_base = lambda blk: lax.rem(blk, SRC_WINDOWS) * rows
    dst_base = lambda blk: lax.rem(blk, DST_WINDOWS) * rows

    def idx_copies(blk):
        sw = pl.ds(pl.multiple_of(src_base(blk), rows), rows)
        dw = pl.ds(pl.multiple_of(dst_base(blk), rows), rows)
        return (pltpu.make_async_copy(src_ref.at[blk], src_smem.at[sw], isem.at[0, lax.rem(blk, SRC_WINDOWS)]),
                pltpu.make_async_copy(dst_ref.at[blk], dst_smem.at[dw], isem.at[1, lax.rem(blk, DST_WINDOWS)]))

    def fetch_idx_now(blk):
        for c in idx_copies(blk):
            c.start()
        for c in idx_copies(blk):
            c.wait()

    def gather_copy(blk_base, r, s):
        return _tile_copy(h2_ref, src_smem[blk_base + r], xbuf.at[s], r * SUBLANES, gsem.at[s])

    def scatter_copy(blk_base, r, s):
        return _tile_copy(ybuf.at[s], r * SUBLANES, y_ref, dst_smem[blk_base + r], ssem.at[s])

    def wait_scatter(s):
        _wait_tiles(rows, ybuf.at[s], y_ref, ssem.at[s])

    def swiglu(g, u):
        gate = jnp.minimum(g, SWIGLU_LIMIT)
        up = jnp.clip(u, -SWIGLU_LIMIT, SWIGLU_LIMIT)
        return ((up + 1.0) * (gate * jax.nn.sigmoid(SWIGLU_ALPHA * gate))).astype(BF16)

    @pl.when(i == 0)
    def _():
        ybuf[...] = jnp.zeros(ybuf.shape, F32)
        for s in range(2):
            dump = pl.ds((n_assign + s * rows) * SUBLANES, rows * SUBLANES)
            zero_fill = pltpu.make_async_copy(ybuf.at[s], y_ref.at[dump], ssem.at[s])
            zero_fill.start()
            zero_fill.wait()
        fetch_idx_now(0)
        _start_tile_gather(src_smem, 0, rows, h2_ref, xbuf.at[0], gsem.at[0])

        @pl.when(1 < n_used)
        def _():
            fetch_idx_now(1)
            _start_tile_gather(src_smem, src_base(1), rows, h2_ref, xbuf.at[1], gsem.at[1])

        @pl.when(2 < n_used)
        def _():
            fetch_idx_now(2)

    @pl.when(i + 3 < n_used)
    def _():
        for c in idx_copies(i + 3):
            c.start()

    @pl.when(i < n_used)
    def _():
        _wait_tiles(rows, h2_ref, xbuf.at[cur], gsem.at[cur])

    @pl.when(jnp.logical_and(i >= 3, i <= n_used))
    def _():
        wait_scatter(cur)

    new_expert = jnp.logical_or(i == 0, bexp_ref[i] != bexp_ref[jnp.maximum(i - 1, 0)])

    def weight_copies(expert, s):
        return (pltpu.make_async_copy(wgu_ref.at[expert], wgu_f32.at[s], wsem.at[0, s]),
                pltpu.make_async_copy(wd_ref.at[expert], wd_f32.at[s], wsem.at[1, s]))

    @pl.when(jnp.logical_and(new_expert, i < n_used))
    def _():
        s = wslot_ref[i]

        @pl.when(i == 0)
        def _():
            for c in weight_copies(bexp_ref[0], 0):
                c.start()

        for c in weight_copies(bexp_ref[i], s):
            c.wait()
        wgu_bf[...] = wgu_f32[s].astype(BF16)
        wd_bf[...] = wd_f32[s].astype(BF16)

        @pl.when(wnext_ref[i] >= 0)
        def _():
            for c in weight_copies(wnext_ref[i], 1 - s):
                c.start(priority=1)

    steady = jnp.logical_and(i >= 1, i + 2 < n_used)

    @pl.when(steady)
    def _():
        gbase, sbase = src_base(i + 2), dst_base(i - 1)
        burst = rows // MOE_COL_TILES
        width = D_FF // MOE_COL_TILES
        x = _load_token_tiles(xbuf.at[cur], rows).astype(BF16)
        acts = []
        for t in range(MOE_COL_TILES):
            cg = slice(t * width, (t + 1) * width)
            cu = slice(D_FF + t * width, D_FF + (t + 1) * width)
            g = jnp.dot(x, wgu_bf[:, cg], preferred_element_type=F32) + bgu_ref[:, cg]
            u = jnp.dot(x, wgu_bf[:, cu], preferred_element_type=F32) + bgu_ref[:, cu]
            for r in range(t * burst, (t + 1) * burst):
                gather_copy(gbase, r, far).start(priority=0)
            acts.append(swiglu(g, u))
        act = jnp.concatenate(acts, axis=1)
        out_width = wd_bf.shape[1] // MOE_COL_TILES
        for t in range(MOE_COL_TILES):
            co = slice(t * out_width, (t + 1) * out_width)
            y_t = jnp.dot(act, wd_bf[:, co], preferred_element_type=F32) + bd_ref[:, co]
            for r in range(t * burst, (t + 1) * burst):
                scatter_copy(sbase, r, far).start(priority=r % DMA_THREADS)
            for j in range(out_width // LANES):
                ybuf[cur, pl.ds(t * (out_width // LANES) + j, rows, stride=SUBLANES), :] = (
                    y_t[:, j * LANES:(j + 1) * LANES])

    @pl.when(jnp.logical_and(i < n_used, jnp.logical_not(steady)))
    def _():
        @pl.when(i + 2 < n_used)
        def _():
            _start_tile_gather(src_smem, src_base(i + 2), rows, h2_ref, xbuf.at[far], gsem.at[far])

        x = _load_token_tiles(xbuf.at[cur], rows).astype(BF16)
        gu = jnp.dot(x, wgu_bf[...], preferred_element_type=F32) + bgu_ref[...]
        act = swiglu(gu[:, :D_FF], gu[:, D_FF:])
        y = jnp.dot(act, wd_bf[...], preferred_element_type=F32) + bd_ref[...]
        _store_token_tiles(ybuf.at[cur], y)

        @pl.when(i >= 1)
        def _():
            _start_tile_scatter(dst_smem, dst_base(i - 1), rows, ybuf.at[far], y_ref, ssem.at[far])

    @pl.when(i == n_used)
    def _():
        _start_tile_scatter(dst_smem, dst_base(i - 1), rows, ybuf.at[far], y_ref, ssem.at[far])

        @pl.when(i >= 2)
        def _():
            wait_scatter(lax.rem(i + 1, MOE_BUFFERS))

        wait_scatter(far)

    @pl.when(i + 3 < n_used)
    def _():
        for c in idx_copies(i + 3):
            c.wait()


def _moe(block_exp, n_used, weight_slot, next_exp, src_rows, dst_rows, h2_tiles, w_gate_up, b_gate_up, w_down,
         b_down, rows, n_assign):
    n_blocks = src_rows.shape[0]
    n_exp, d_model, ff2 = w_gate_up.shape
    by_expert = lambda i, be, nu, ws, ne: (be[i], 0, 0)
    hbm = pl.BlockSpec(memory_space=pl.ANY)
    grid_spec = pltpu.PrefetchScalarGridSpec(
        num_scalar_prefetch=4,
        grid=(n_blocks,),
        in_specs=[
            hbm, hbm, hbm,
            hbm,
            pl.BlockSpec((None, 1, ff2), by_expert),
            hbm,
            pl.BlockSpec((None, 1, d_model), by_expert),
        ],
        out_specs=hbm,
        scratch_shapes=[
            pltpu.SMEM((SRC_WINDOWS * rows,), jnp.int32),
            pltpu.SMEM((DST_WINDOWS * rows,), jnp.int32),
            pltpu.VMEM((MOE_BUFFERS, rows * SUBLANES, LANES), F32),
            pltpu.VMEM((MOE_BUFFERS, rows * SUBLANES, LANES), F32),
            pltpu.VMEM((2, d_model, ff2), F32),
            pltpu.VMEM((2, ff2 // 2, d_model), F32),
            pltpu.VMEM((d_model, ff2), BF16),
            pltpu.VMEM((ff2 // 2, d_model), BF16),
            pltpu.SemaphoreType.DMA((2, DST_WINDOWS)),
            pltpu.SemaphoreType.DMA((MOE_BUFFERS,)),
            pltpu.SemaphoreType.DMA((MOE_BUFFERS,)),
            pltpu.SemaphoreType.DMA((2, 2)),
        ],
    )
    return pl.pallas_call(
        functools.partial(_moe_kernel, rows=rows, n_assign=n_assign),
        grid_spec=grid_spec,
        out_shape=jax.ShapeDtypeStruct(((n_assign + 2 * rows) * SUBLANES, LANES), F32),
        compiler_params=_cparams(("arbitrary",)),
        name="moe",
    )(block_exp, n_used, weight_slot, next_exp, src_rows, dst_rows, h2_tiles, w_gate_up, b_gate_up[:, None, :],
      w_down, b_down[:, None, :])


def _combine_kernel(y_ref, gate_ref, x1_ref, g_ref, o_ref, *, tm):
    gate = gate_ref[...]
    out = x1_ref[...]
    stride = TOP_K * SUBLANES
    for k in range(TOP_K):
        y_k = jnp.concatenate(
            [y_ref[pl.ds(k * SUBLANES + j, tm, stride=stride), :] for j in range(SUBLANES)], axis=1)
        out = out + gate[:, k:k + 1] * y_k
    ms = jnp.mean(out * out, axis=-1, keepdims=True)
    o_ref[...] = out * lax.rsqrt(ms + EPS) * g_ref[...]


def _combine(y_tiles, gate, x1, g_final, tm):
    tokens, d_model = x1.shape
    row = lambda i: (i, 0)
    return pl.pallas_call(
        functools.partial(_combine_kernel, tm=tm),
        grid=(tokens // tm,),
        in_specs=[
            pl.BlockSpec((tm * TOP_K * SUBLANES, LANES), row),
            pl.BlockSpec((tm, LANES), row),
            pl.BlockSpec((tm, d_model), row),
            pl.BlockSpec((1, d_model), lambda i: (0, 0)),
        ],
        out_specs=pl.BlockSpec((tm, d_model), row),
        out_shape=jax.ShapeDtypeStruct((tokens, d_model), F32),
        compiler_params=_cparams(("arbitrary",)),
        name="combine",
    )(y_tiles, gate, x1, g_final.astype(F32)[None, :])


PROJ_ROWS = 512
MID_ROWS = 1024
ATTN_Q_ROWS = 1024
ATTN_K_ROWS = 512
WINDOW_BLOCKS_PER_STEP = 8
MOE_ROWS = 256
COMBINE_ROWS = 256


def _layer(x2d, batch, seq, g_mix, w_in, qa_norm, ka_norm, sink_b, w_branch_a, w_branch_b, w_out,
           g_ffn, w_router, b_router, w_gate_up, b_gate_up, w_down, b_down, g_final):
    tokens = x2d.shape[0]
    qa, ka, vta, qb, kb, vtb, sga, sgb = _project(x2d, g_mix, w_in, qa_norm, ka_norm, seq,
                                                  min(PROJ_ROWS, seq), min(ATTN_K_ROWS, seq // 2))
    oa = _attention_global(qa, ka, vta, batch, seq, ATTN_Q_ROWS)
    ob = _attention_window(qb, kb, vtb, sink_b, batch, seq)
    x1, h2_tiles, route, gate, cnt = _mid(
        oa, ob, sga, sgb, x2d, w_branch_a, w_branch_b, w_out, g_ffn, w_router, b_router, min(MID_ROWS, tokens))

    rows = MOE_ROWS
    n_assign = tokens * TOP_K
    n_blocks = n_assign // rows + N_EXPERTS
    counts = cnt[:, 0].astype(jnp.int32)
    padded = (counts + rows - 1) // rows * rows
    ends = jnp.cumsum(padded)
    starts = ends - padded
    eidx, rank = route[:TOP_K], route[SUBLANES:SUBLANES + TOP_K]
    seg_start = jnp.sum(jnp.where(eidx[None] == jnp.arange(N_EXPERTS)[:, None, None], starts[:, None, None], 0),
                        axis=0)
    dest = seg_start + rank
    gap_lo = jnp.concatenate([starts + counts, ends[-1:]]).astype(jnp.int32)
    gap_hi = jnp.concatenate([ends, jnp.full((1,), n_blocks * rows)]).astype(jnp.int32)
    row_assign = _invert_routes(dest.reshape(-1), gap_lo, gap_hi, n_blocks * rows, tokens).reshape(n_blocks, rows)
    real = row_assign >= 0
    dump_slot = (n_assign + (jnp.arange(n_blocks, dtype=jnp.int32) % 2)[:, None] * rows
                 + jnp.arange(rows, dtype=jnp.int32)[None, :])
    src_rows = jnp.where(real, row_assign // TOP_K, 0) * SUBLANES
    dst_rows = jnp.where(real, row_assign, dump_slot) * SUBLANES
    block_start = jnp.arange(n_blocks, dtype=jnp.int32) * rows
    block_exp = jnp.minimum(
        jnp.sum((ends[None, :] <= block_start[:, None]).astype(jnp.int32), axis=1), N_EXPERTS - 1)
    n_used = (ends[-1:] // rows).astype(jnp.int32)

    block_id = jnp.arange(n_blocks, dtype=jnp.int32)
    first_of_expert = jnp.logical_or(block_id == 0, block_exp != jnp.roll(block_exp, 1))
    weight_slot = ((jnp.cumsum(first_of_expert.astype(jnp.int32)) - 1) % 2).astype(jnp.int32)
    after_segment = ends[block_exp] // rows
    next_exp = jnp.where(after_segment < n_used[0],
                         block_exp[jnp.minimum(after_segment, n_blocks - 1)], -1).astype(jnp.int32)

    y_tiles = _moe(block_exp, n_used, weight_slot, next_exp, src_rows, dst_rows, h2_tiles,
                   w_gate_up, b_gate_up, w_down, b_down, rows, n_assign)
    return _combine(y_tiles, gate, x1, g_final, min(COMBINE_ROWS, tokens))


def kernel(x, g_mix, w_in, qa_norm, ka_norm, sink_b, w_branch_a, w_branch_b, w_out, g_ffn, w_router, b_router,
           w_gate_up, b_gate_up, w_down, b_down, g_final):
    batch, seq, d_model = x.shape
    assert g_mix.shape[0] == 1, "single-layer block"
    out = _layer(x.reshape(batch * seq, d_model), batch, seq, g_mix[0], w_in[0], qa_norm[0], ka_norm[0],
                 sink_b[0], w_branch_a[0], w_branch_b[0], w_out[0], g_ffn[0], w_router[0], b_router[0],
                 w_gate_up[0], b_gate_up[0], w_down[0], b_down[0], g_final)
    return out.reshape(batch, seq, d_model)
```

```python
import functools

import jax
import jax.numpy as jnp
from jax import lax
from jax.experimental import pallas as pl
from jax.experimental.pallas import tpu as pltpu

HEAD_DIM = 64
Q_HEADS = 8
KV_HEADS = 2
GROUPS = Q_HEADS // KV_HEADS
Q_COLS = Q_HEADS * HEAD_DIM
KV_COLS = KV_HEADS * HEAD_DIM
GROUP_COLS = GROUPS * HEAD_DIM
WINDOW = 128
GRID_W = 64
ROPE_THETA = 10000.0
N_EXPERTS = 32
TOP_K = 4
D_FF = 1024
SWIGLU_LIMIT = 7.0
SWIGLU_ALPHA = 1.702
EPS = 1e-6
NEG_INF = -1e30
LOG2_E = 1.4426950408889634

LANES = 128
SUBLANES = 8
V_ROWS = 80
VMEM_LIMIT = 56 * 1024 * 1024

F32 = jnp.float32
BF16 = jnp.bfloat16


def _cparams(sem, vmem=VMEM_LIMIT):
    return pltpu.CompilerParams(dimension_semantics=sem, vmem_limit_bytes=vmem)


def _rope(y, cos, sin_signed):
    lane = lax.broadcasted_iota(jnp.int32, y.shape, 1)
    first_half = (lane & 32) == 0
    rot = jnp.where(first_half, pltpu.roll(y, 96, 1), pltpu.roll(y, 32, 1))
    return y * cos + rot * sin_signed


def _head_rmsnorm(q, ones_blockdiag, gain):
    q2 = q * q
    hi = q2.astype(BF16)
    lo = (q2 - hi.astype(F32)).astype(BF16)
    ss = (jnp.dot(hi, ones_blockdiag, preferred_element_type=F32)
          + jnp.dot(lo, ones_blockdiag, preferred_element_type=F32))
    return q * lax.rsqrt(ss * (1.0 / HEAD_DIM) + EPS) * gain


def _proj_kernel(x_ref, g_ref, w_ref, cosa_ref, sina_ref, cos1_ref, sin1_ref, qg_ref, kg_ref, bd_ref,
                 qa_ref, ka_ref, vta_ref, qb_ref, kb_ref, vtb_ref, sga_ref, sgb_ref, *, d_model):
    x = x_ref[...]
    ms = jnp.mean(x * x, axis=-1, keepdims=True)
    h = (x * lax.rsqrt(ms + EPS) * g_ref[...]).astype(BF16)
    scale = HEAD_DIM ** -0.5 * LOG2_E

    def proj(c0, width):
        return jnp.dot(h, w_ref[:, c0:c0 + width], preferred_element_type=F32)

    cosa, sina = cosa_ref[...], sina_ref[...]
    cos1, sin1 = cos1_ref[...], sin1_ref[...]
    bd = bd_ref[...]
    slab = lambda v, s: v[:, s * LANES:(s + 1) * LANES]
    c = 0
    qa = proj(c, Q_COLS)
    for s in range(Q_COLS // LANES):
        q = _head_rmsnorm(slab(qa, s), bd, qg_ref[...])
        qa_ref[:, s * LANES:(s + 1) * LANES] = (_rope(q, cosa, sina) * scale).astype(BF16)
    c += Q_COLS
    kva = proj(c, 2 * KV_COLS)
    ka = _rope(_head_rmsnorm(slab(kva, 0), bd, kg_ref[...]), cosa, sina).astype(BF16)
    va_t = slab(kva, 1).T
    ones = jnp.ones((V_ROWS - HEAD_DIM, va_t.shape[1]), BF16)
    for hd in range(KV_HEADS):
        ka_ref[hd] = ka[:, hd * HEAD_DIM:(hd + 1) * HEAD_DIM]
        vta_ref[hd, :HEAD_DIM, :] = va_t[hd * HEAD_DIM:(hd + 1) * HEAD_DIM, :].astype(BF16)
        vta_ref[hd, HEAD_DIM:, :] = ones
    c += 2 * KV_COLS
    qb = proj(c, Q_COLS)
    for s in range(Q_COLS // LANES):
        qb_ref[:, s * LANES:(s + 1) * LANES] = (_rope(slab(qb, s), cos1, sin1) * scale).astype(BF16)
    c += Q_COLS
    kvb = proj(c, 2 * KV_COLS)
    kb = _rope(slab(kvb, 0), cos1, sin1).astype(BF16)
    vb_t = slab(kvb, 1).T
    ones = jnp.ones((V_ROWS - HEAD_DIM, WINDOW), BF16)
    for hd in range(KV_HEADS):
        kb_ref[hd] = kb[:, hd * HEAD_DIM:(hd + 1) * HEAD_DIM]
        for ch in range(vtb_ref.shape[1]):
            vtb_ref[hd, ch, :HEAD_DIM, :] = vb_t[hd * HEAD_DIM:(hd + 1) * HEAD_DIM,
                                                 ch * WINDOW:(ch + 1) * WINDOW].astype(BF16)
            vtb_ref[hd, ch, HEAD_DIM:, :] = ones
    c += 2 * KV_COLS
    sga_ref[...] = jax.nn.sigmoid(proj(c, d_model)).astype(BF16)
    c += d_model
    sgb_ref[...] = jax.nn.sigmoid(proj(c, d_model)).astype(BF16)


def _rope_tables(seq):
    def slab(ang):
        cos, sin = jnp.cos(ang), jnp.sin(ang)
        cos64 = jnp.concatenate([cos, cos], axis=-1)
        sin64 = jnp.concatenate([-sin, sin], axis=-1)
        return jnp.tile(cos64, (1, 2)), jnp.tile(sin64, (1, 2))

    pos = jnp.arange(seq, dtype=F32)
    inv1 = ROPE_THETA ** (-jnp.arange(0, HEAD_DIM, 2, dtype=F32) / HEAD_DIM)
    ang1 = pos[:, None] * inv1[None, :]
    rows = seq // GRID_W
    row = jnp.broadcast_to(jnp.arange(rows, dtype=F32)[:, None], (rows, GRID_W)).reshape(-1)
    col = jnp.broadcast_to(jnp.arange(GRID_W, dtype=F32)[None, :], (rows, GRID_W)).reshape(-1)
    axis_dim = HEAD_DIM // 2
    inva = ROPE_THETA ** (-jnp.arange(0, axis_dim, 2, dtype=F32) / axis_dim)
    anga = jnp.concatenate([row[:, None] * inva[None, :], col[:, None] * inva[None, :]], axis=-1)
    return slab(anga), slab(ang1)


def _project(x2d, g_mix, w_in, qa_norm, ka_norm, seq, tm, tk):
    tokens, d_model = x2d.shape
    in_cols = w_in.shape[1]
    (cosa, sina), (cos1, sin1) = _rope_tables(seq)
    head_id = jnp.arange(LANES) // HEAD_DIM
    ones_blockdiag = (head_id[:, None] == head_id[None, :]).astype(BF16)
    qg = jnp.tile(qa_norm.astype(F32), 2)[None, :]
    kg = jnp.tile(ka_norm.astype(F32), 2)[None, :]
    seq_tiles = seq // tm
    row = lambda i: (i, 0)
    pos = lambda i: (i % seq_tiles, 0)
    fixed = lambda i: (0, 0)
    batch = tokens // seq
    assert tk % tm == 0 and tm % WINDOW == 0
    sub = tk // tm
    bf = lambda *shape: jax.ShapeDtypeStruct(shape, BF16)
    tok_spec = lambda cols: pl.BlockSpec((tm, cols), row)
    k_spec = pl.BlockSpec((None, KV_HEADS, tm, HEAD_DIM), lambda i: (i // seq_tiles, 0, i % seq_tiles, 0))
    vta_spec = pl.BlockSpec((None, KV_HEADS, None, V_ROWS, tm),
                            lambda i: (i // seq_tiles, 0, (i % seq_tiles) // sub, 0, (i % seq_tiles) % sub))
    vtb_spec = pl.BlockSpec((None, KV_HEADS, tm // WINDOW, V_ROWS, WINDOW),
                            lambda i: (i // seq_tiles, 0, i % seq_tiles, 0, 0))
    k_shape = bf(batch, KV_HEADS, seq, HEAD_DIM)
    return pl.pallas_call(
        functools.partial(_proj_kernel, d_model=d_model),
        grid=(tokens // tm,),
        in_specs=[
            pl.BlockSpec((tm, d_model), row),
            pl.BlockSpec((1, d_model), fixed),
            pl.BlockSpec((d_model, in_cols), fixed),
            pl.BlockSpec((tm, LANES), pos), pl.BlockSpec((tm, LANES), pos),
            pl.BlockSpec((tm, LANES), pos), pl.BlockSpec((tm, LANES), pos),
            pl.BlockSpec((1, LANES), fixed), pl.BlockSpec((1, LANES), fixed),
            pl.BlockSpec((LANES, LANES), fixed),
        ],
        out_specs=[tok_spec(Q_COLS), k_spec, vta_spec, tok_spec(Q_COLS), k_spec, vtb_spec,
                   tok_spec(d_model), tok_spec(d_model)],
        out_shape=[bf(tokens, Q_COLS), k_shape, bf(batch, KV_HEADS, seq // tk, V_ROWS, tk),
                   bf(tokens, Q_COLS), k_shape, bf(batch, KV_HEADS, seq // WINDOW, V_ROWS, WINDOW),
                   bf(tokens, d_model), bf(tokens, d_model)],
        compiler_params=_cparams(("arbitrary",)),
        name="proj",
    )(x2d, g_mix.astype(F32)[None, :], w_in.astype(BF16), cosa, sina, cos1, sin1, qg, kg, ones_blockdiag)


def _stack_group_queries(q):
    return jnp.concatenate([q[:, g * HEAD_DIM:(g + 1) * HEAD_DIM] for g in range(GROUPS)], axis=0)


def _unstack_group_outputs(o_t, tq):
    pad = jnp.zeros((LANES - HEAD_DIM, o_t.shape[1]), o_t.dtype)
    o_pad = jnp.concatenate([o_t, pad], axis=0)
    heads = [o_pad[:, g * tq:(g + 1) * tq].T[:, :HEAD_DIM] for g in range(GROUPS)]
    return jnp.concatenate(heads, axis=1)


def _attn_global_kernel(q_ref, k_ref, vt_ref, o_ref, s0, s1, p0, p1, acc_ref, *, tq, tk, n_chunks):
    q_st = _stack_group_queries(q_ref[...])
    m_rows = q_st.shape[0]

    def scores(c, s_ref):
        k_c = k_ref[pl.ds(pl.multiple_of(c * tk, tk), tk), :]
        s_ref[...] = lax.dot_general(k_c, q_st, (((1,), (1,)), ((), ())), preferred_element_type=F32)

    def softmax(s_ref, p_ref, m_prev):
        m_new = jnp.maximum(m_prev, jnp.max(s_ref[...], axis=0, keepdims=True))
        p_ref[...] = jnp.exp2(s_ref[...] - m_new).astype(BF16)
        return m_new, jnp.exp2(m_prev - m_new)

    def values(c, p_ref, alpha):
        acc_ref[...] = acc_ref[...] * alpha + jnp.dot(vt_ref[c], p_ref[...], preferred_element_type=F32)

    scores(0, s0)
    p1[...] = jnp.zeros(p1.shape, BF16)
    acc_ref[...] = jnp.zeros(acc_ref.shape, F32)

    def pair(j, carry, more_chunks=True):
        m, alpha = carry
        c = 2 * j
        scores(c + 1, s1)
        m_a, alpha_a = softmax(s0, p0, m)
        values(jnp.maximum(c - 1, 0), p1, alpha)
        if more_chunks:
            scores(c + 2, s0)
        m_b, alpha_b = softmax(s1, p1, m_a)
        values(c, p0, alpha_a)
        return m_b, alpha_b

    init = (jnp.full((1, m_rows), NEG_INF, F32), jnp.ones((1, m_rows), F32))
    carry = lax.fori_loop(0, n_chunks // 2 - 1, pair, init)
    _, alpha = pair(n_chunks // 2 - 1, carry, more_chunks=False)
    values(n_chunks - 1, p1, alpha)
    acc = acc_ref[...]
    o_t = acc[:HEAD_DIM, :] / acc[HEAD_DIM:HEAD_DIM + 1, :]
    o_ref[...] = _unstack_group_outputs(o_t, tq).astype(o_ref.dtype)


def _attn_window_kernel(q_ref, k_ref, vt_ref, sink_ref, band_ref, o_ref, *, tq, n_blocks, blocks_per_step):
    span = 3 * tq
    sink = sink_ref[...]
    for i in range(blocks_per_step):
        n = pl.program_id(2) * blocks_per_step + i
        c0 = jnp.clip(n - 1, 0, n_blocks - 3)
        start = pl.multiple_of(c0 * tq, tq)
        q_st = _stack_group_queries(q_ref[i * tq:(i + 1) * tq, :])
        k_w = k_ref[pl.ds(start, span), :]
        s = lax.dot_general(k_w, q_st, (((1,), (1,)), ((), ())), preferred_element_type=F32)
        s = s + band_ref[n - c0]
        m = jnp.maximum(jnp.max(s, axis=0, keepdims=True), sink)
        p = jnp.exp2(s - m).astype(BF16)
        acc = jnp.dot(vt_ref[c0], p[:tq], preferred_element_type=F32)
        for j in range(1, 3):
            acc += jnp.dot(vt_ref[c0 + j], p[j * tq:(j + 1) * tq], preferred_element_type=F32)
        denom = acc[HEAD_DIM:HEAD_DIM + 1, :] + jnp.exp2(sink - m)
        o_ref[i * tq:(i + 1) * tq, :] = _unstack_group_outputs(acc[:HEAD_DIM, :] / denom, tq).astype(o_ref.dtype)


def _attention_global(q2d, k_hm, v_t, batch, seq, tq):
    tokens = q2d.shape[0]
    n_chunks, tk = v_t.shape[2], v_t.shape[4]
    assert n_chunks % 2 == 0, "the chunk pipeline advances two chunks per step"
    q_tiles = seq // tq
    m_rows = GROUPS * tq
    qmap = lambda b, h, i: (b * q_tiles + i, h)
    return pl.pallas_call(
        functools.partial(_attn_global_kernel, tq=tq, tk=tk, n_chunks=n_chunks),
        grid=(batch, KV_HEADS, q_tiles),
        in_specs=[
            pl.BlockSpec((tq, GROUP_COLS), qmap),
            pl.BlockSpec((None, None, seq, HEAD_DIM), lambda b, h, i: (b, h, 0, 0)),
            pl.BlockSpec((None, None, n_chunks, V_ROWS, tk), lambda b, h, i: (b, h, 0, 0, 0)),
        ],
        out_specs=pl.BlockSpec((tq, GROUP_COLS), qmap),
        out_shape=jax.ShapeDtypeStruct((tokens, Q_COLS), BF16),
        scratch_shapes=[pltpu.VMEM((tk, m_rows), F32), pltpu.VMEM((tk, m_rows), F32),
                        pltpu.VMEM((tk, m_rows), BF16), pltpu.VMEM((tk, m_rows), BF16),
                        pltpu.VMEM((V_ROWS, m_rows), F32)],
        compiler_params=_cparams(("arbitrary", "arbitrary", "arbitrary")),
        name="attn_global",
    )(q2d, k_hm, v_t)


def _attention_window(q2d, k_hm, v_t, sink, batch, seq):
    tokens = q2d.shape[0]
    tq = WINDOW
    n_blocks = seq // tq
    assert n_blocks >= 3
    m_rows = GROUPS * tq
    sink_rows = jnp.repeat(sink.astype(F32).reshape(KV_HEADS, GROUPS) * LOG2_E, tq, axis=1)[:, None, :]
    per_step = WINDOW_BLOCKS_PER_STEP if n_blocks % WINDOW_BLOCKS_PER_STEP == 0 else 1
    steps = n_blocks // per_step
    key_j = jnp.arange(3 * tq)[None, :, None]
    query = (jnp.arange(m_rows) % tq)[None, None, :] + jnp.arange(3)[:, None, None] * tq
    band = jnp.where(jnp.abs(key_j - query) <= WINDOW, 0.0, NEG_INF).astype(F32)
    qmap = lambda b, h, i: (b * steps + i, h)
    return pl.pallas_call(
        functools.partial(_attn_window_kernel, tq=tq, n_blocks=n_blocks, blocks_per_step=per_step),
        grid=(batch, KV_HEADS, steps),
        in_specs=[
            pl.BlockSpec((per_step * tq, GROUP_COLS), qmap),
            pl.BlockSpec((None, None, seq, HEAD_DIM), lambda b, h, i: (b, h, 0, 0)),
            pl.BlockSpec((None, None, n_blocks, V_ROWS, tq), lambda b, h, i: (b, h, 0, 0, 0)),
            pl.BlockSpec((None, 1, m_rows), lambda b, h, i: (h, 0, 0)),
            pl.BlockSpec((3, 3 * tq, m_rows), lambda b, h, i: (0, 0, 0)),
        ],
        out_specs=pl.BlockSpec((per_step * tq, GROUP_COLS), qmap),
        out_shape=jax.ShapeDtypeStruct((tokens, Q_COLS), BF16),
        compiler_params=_cparams(("arbitrary", "arbitrary", "arbitrary")),
        name="attn_window",
    )(q2d, k_hm, v_t, sink_rows, band)


def _store_token_tiles(ref, val):
    rows = val.shape[0]
    for j in range(val.shape[1] // LANES):
        ref[pl.ds(j, rows, stride=SUBLANES), :] = val[:, j * LANES:(j + 1) * LANES]


def _load_token_tiles(ref, rows):
    return jnp.concatenate([ref[pl.ds(j, rows, stride=SUBLANES), :] for j in range(SUBLANES)], axis=1)


ROUTE_ROWS = 2 * SUBLANES


def _top_k_route(logits_t):
    eid = lax.broadcasted_iota(jnp.int32, logits_t.shape, 0)
    work = logits_t
    vals, idxs, sels = [], [], []
    for _ in range(TOP_K):
        mx = jnp.max(work, axis=0, keepdims=True)
        ix = jnp.min(jnp.where(work == mx, eid, N_EXPERTS), axis=0, keepdims=True)
        sel = eid == ix
        work = jnp.where(sel, -jnp.inf, work)
        vals.append(mx)
        idxs.append(ix)
        sels.append(sel)
    return vals, idxs, sels


def _mid_kernel(oa_ref, ob_ref, sga_ref, sgb_ref, x_ref, wa_ref, wb_ref, wo_ref, g_ref, wr_ref, br_ref, tri_ref,
                x1_ref, h2_ref, route_ref, gate_ref, cnt_ref, *, halves):
    @pl.when(pl.program_id(0) == 0)
    def _():
        cnt_ref[...] = jnp.zeros(cnt_ref.shape, F32)

    tm = x_ref.shape[0]
    hm = tm // halves
    routed = []
    for hf in range(halves):
        rs = slice(hf * hm, (hf + 1) * hm)
        a = jnp.dot(oa_ref[rs, :], wa_ref[...], preferred_element_type=F32)
        b = jnp.dot(ob_ref[rs, :], wb_ref[...], preferred_element_type=F32)
        merged = sga_ref[rs, :].astype(F32) * a + sgb_ref[rs, :].astype(F32) * b
        x1 = x_ref[rs, :] + jnp.dot(merged.astype(BF16), wo_ref[...], preferred_element_type=F32)
        x1_ref[rs, :] = x1
        ms = jnp.mean(x1 * x1, axis=-1, keepdims=True)
        h2 = x1 * lax.rsqrt(ms + EPS) * g_ref[...]
        _store_token_tiles(h2_ref.at[pl.ds(hf * hm * SUBLANES, hm * SUBLANES)], h2)
        hi = h2.astype(BF16)
        lo = (h2 - hi.astype(F32)).astype(BF16)
        nt = (((1,), (1,)), ((), ()))
        r2 = (lax.dot_general(wr_ref[...], hi, nt, preferred_element_type=F32)
              + lax.dot_general(wr_ref[...], lo, nt, preferred_element_type=F32))
        bias = jnp.concatenate([br_ref[...]] * (hm // LANES), axis=1)
        routed.append(_top_k_route(r2[:N_EXPERTS] + r2[N_EXPERTS:] + bias))

    onehots = []
    for vals, idxs, sels in routed:
        chosen = sels[0] | sels[1] | sels[2] | sels[3]
        onehots.append(jnp.where(chosen, 1.0, 0.0))
    onehot = jnp.concatenate(onehots, axis=1)
    counted = jnp.concatenate([cnt_ref[...]] * (tm // LANES), axis=1)
    before = jnp.dot(onehot.astype(BF16), tri_ref[...], preferred_element_type=F32) + counted
    zeros4 = jnp.zeros((SUBLANES - TOP_K, hm), jnp.int32)
    for hf, (vals, idxs, sels) in enumerate(routed):
        rs = slice(hf * hm, (hf + 1) * hm)
        exps = [jnp.exp(v - vals[0]) for v in vals]
        denom = exps[0] + exps[1] + exps[2] + exps[3]
        ranks = [jnp.sum(jnp.where(sels[k], before[:, rs], 0.0), axis=0, keepdims=True).astype(jnp.int32)
                 for k in range(TOP_K)]
        route_ref[:, rs] = jnp.concatenate(idxs + [zeros4] + ranks + [zeros4], axis=0)
        gates_t = jnp.concatenate([e / denom for e in exps] + [jnp.zeros((LANES - TOP_K, hm), F32)], axis=0)
        gate_ref[rs, :] = gates_t.T
    cnt_ref[...] = cnt_ref[...] + jnp.sum(onehot, axis=1, keepdims=True)


def _mid(oa, ob, sga, sgb, x2d, w_a, w_b, w_o, g_ffn, w_router, b_router, tm):
    tokens, d_model = x2d.shape
    wr_t = w_router.astype(F32).T
    wr_hi = wr_t.astype(BF16)
    wr_lo = (wr_t - wr_hi.astype(F32)).astype(BF16)
    wr_split = jnp.concatenate([wr_hi, wr_lo], axis=0)
    br = jnp.broadcast_to(b_router.astype(F32)[:, None], (N_EXPERTS, LANES))
    tri = (jnp.arange(tm)[:, None] < jnp.arange(tm)[None, :]).astype(BF16)
    halves = 2 if tm % (2 * LANES) == 0 else 1
    row = lambda i: (i, 0)
    fixed = lambda i: (0, 0)
    return pl.pallas_call(
        functools.partial(_mid_kernel, halves=halves),
        grid=(tokens // tm,),
        in_specs=[
            pl.BlockSpec((tm, Q_COLS), row), pl.BlockSpec((tm, Q_COLS), row),
            pl.BlockSpec((tm, d_model), row), pl.BlockSpec((tm, d_model), row),
            pl.BlockSpec((tm, d_model), row),
            pl.BlockSpec((Q_COLS, d_model), fixed), pl.BlockSpec((Q_COLS, d_model), fixed),
            pl.BlockSpec((d_model, d_model), fixed),
            pl.BlockSpec((1, d_model), fixed),
            pl.BlockSpec((2 * N_EXPERTS, d_model), fixed),
            pl.BlockSpec((N_EXPERTS, LANES), fixed),
            pl.BlockSpec((tm, tm), fixed),
        ],
        out_specs=[
            pl.BlockSpec((tm, d_model), row),
            pl.BlockSpec((tm * SUBLANES, LANES), row),
            pl.BlockSpec((ROUTE_ROWS, tm), lambda i: (0, i)),
            pl.BlockSpec((tm, LANES), row),
            pl.BlockSpec((N_EXPERTS, LANES), fixed),
        ],
        out_shape=[
            jax.ShapeDtypeStruct((tokens, d_model), F32),
            jax.ShapeDtypeStruct((tokens * SUBLANES, LANES), F32),
            jax.ShapeDtypeStruct((ROUTE_ROWS, tokens), jnp.int32),
            jax.ShapeDtypeStruct((tokens, LANES), F32),
            jax.ShapeDtypeStruct((N_EXPERTS, LANES), F32),
        ],
        compiler_params=_cparams(("arbitrary",)),
        name="mid",
    )(oa, ob, sga, sgb, x2d, w_a.astype(BF16), w_b.astype(BF16), w_o.astype(BF16),
      g_ffn.astype(F32)[None, :], wr_split, br, tri)


def _invert_kernel(gap_lo_ref, gap_hi_ref, dest_ref, out_ref, *, tokens):
    def clear(r, carry):
        out_ref[r] = -1
        return carry
    for g in range(gap_lo_ref.shape[0]):
        lax.fori_loop(gap_lo_ref[g], gap_hi_ref[g], clear, 0)
    for k in range(TOP_K):
        def put(t, carry):
            out_ref[dest_ref[k * tokens + t]] = t * TOP_K + k
            return carry
        lax.fori_loop(0, tokens, put, 0, unroll=16)


def _invert_routes(dest_flat, gap_lo, gap_hi, n_rows, tokens):
    smem = pl.BlockSpec(memory_space=pltpu.SMEM)
    return pl.pallas_call(
        functools.partial(_invert_kernel, tokens=tokens),
        in_specs=[smem, smem, smem],
        out_specs=smem,
        out_shape=jax.ShapeDtypeStruct((n_rows,), jnp.int32),
        name="invert_routes",
    )(gap_lo, gap_hi, dest_flat)


def _tile_copy(src_ref, src_row, dst_ref, dst_row, sem):
    return pltpu.make_async_copy(
        src_ref.at[pl.ds(pl.multiple_of(src_row, SUBLANES), SUBLANES)],
        dst_ref.at[pl.ds(pl.multiple_of(dst_row, SUBLANES), SUBLANES)], sem)


def _start_tile_gather(off_smem, base, n_rows, src_ref, dst_ref, sem):
    def start(r, carry):
        _tile_copy(src_ref, off_smem[base + r], dst_ref, r * SUBLANES, sem).start()
        return carry
    lax.fori_loop(0, n_rows, start, 0, unroll=8)


def _start_tile_scatter(off_smem, base, n_rows, src_ref, dst_ref, sem):
    def start(r, carry):
        _tile_copy(src_ref, r * SUBLANES, dst_ref, off_smem[base + r], sem).start()
        return carry
    lax.fori_loop(0, n_rows, start, 0, unroll=8)


def _wait_tiles(n_rows, src_ref, dst_ref, sem):
    window = pl.ds(0, n_rows * SUBLANES)
    pltpu.make_async_copy(src_ref.at[window], dst_ref.at[window], sem).wait()


SRC_WINDOWS = 3
DST_WINDOWS = 5
MOE_BUFFERS = 3
MOE_COL_TILES = 4
DMA_THREADS = 2


def _moe_kernel(bexp_ref, nused_ref, wslot_ref, wnext_ref, src_ref, dst_ref, h2_ref, wgu_ref, bgu_ref, wd_ref, bd_ref,
                y_ref, src_smem, dst_smem, xbuf, ybuf, wgu_f32, wd_f32, wgu_bf, wd_bf, isem, gsem, ssem, wsem,
                *, rows, n_assign):
    i = pl.program_id(0)
    n_used = nused_ref[0]
    cur = lax.rem(i, MOE_BUFFERS)
    far = lax.rem(i + 2, MOE_BUFFERS)
    src_base = lambda blk: lax.rem(blk, SRC_WINDOWS) * rows
    dst_base = lambda blk: lax.rem(blk, DST_WINDOWS) * rows

    def idx_copies(blk):
        sw = pl.ds(pl.multiple_of(src_base(blk), rows), rows)
        dw = pl.ds(pl.multiple_of(dst_base(blk), rows), rows)
        return (pltpu.make_async_copy(src_ref.at[blk], src_smem.at[sw], isem.at[0, lax.rem(blk, SRC_WINDOWS)]),
                pltpu.make_async_copy(dst_ref.at[blk], dst_smem.at[dw], isem.at[1, lax.rem(blk, DST_WINDOWS)]))

    def fetch_idx_now(blk):
        for c in idx_copies(blk):
            c.start()
        for c in idx_copies(blk):
            c.wait()

    def gather_copy(blk_base, r, s):
        return _tile_copy(h2_ref, src_smem[blk_base + r], xbuf.at[s], r * SUBLANES, gsem.at[s])

    def scatter_copy(blk_base, r, s):
        return _tile_copy(ybuf.at[s], r * SUBLANES, y_ref, dst_smem[blk_base + r], ssem.at[s])

    def wait_scatter(s):
        _wait_tiles(rows, ybuf.at[s], y_ref, ssem.at[s])

    def swiglu(g, u):
        gate = jnp.minimum(g, SWIGLU_LIMIT)
        up = jnp.clip(u, -SWIGLU_LIMIT, SWIGLU_LIMIT)
        return ((up + 1.0) * (gate * jax.nn.sigmoid(SWIGLU_ALPHA * gate))).astype(BF16)

    @pl.when(i == 0)
    def _():
        ybuf[...] = jnp.zeros(ybuf.shape, F32)
        for s in range(2):
            dump = pl.ds((n_assign + s * rows) * SUBLANES, rows * SUBLANES)
            zero_fill = pltpu.make_async_copy(ybuf.at[s], y_ref.at[dump], ssem.at[s])
            zero_fill.start()
            zero_fill.wait()
        fetch_idx_now(0)
        _start_tile_gather(src_smem, 0, rows, h2_ref, xbuf.at[0], gsem.at[0])

        @pl.when(1 < n_used)
        def _():
            fetch_idx_now(1)
            _start_tile_gather(src_smem, src_base(1), rows, h2_ref, xbuf.at[1], gsem.at[1])

        @pl.when(2 < n_used)
        def _():
            fetch_idx_now(2)

    @pl.when(i + 3 < n_used)
    def _():
        for c in idx_copies(i + 3):
            c.start()

    @pl.when(i < n_used)
    def _():
        _wait_tiles(rows, h2_ref, xbuf.at[cur], gsem.at[cur])

    @pl.when(jnp.logical_and(i >= 3, i <= n_used))
    def _():
        wait_scatter(cur)

    new_expert = jnp.logical_or(i == 0, bexp_ref[i] != bexp_ref[jnp.maximum(i - 1, 0)])

    def weight_copies(expert, s):
        return (pltpu.make_async_copy(wgu_ref.at[expert], wgu_f32.at[s], wsem.at[0, s]),
                pltpu.make_async_copy(wd_ref.at[expert], wd_f32.at[s], wsem.at[1, s]))

    @pl.when(jnp.logical_and(new_expert, i < n_used))
    def _():
        s = wslot_ref[i]

        @pl.when(i == 0)
        def _():
            for c in weight_copies(bexp_ref[0], 0):
                c.start()

        for c in weight_copies(bexp_ref[i], s):
            c.wait()
        wgu_bf[...] = wgu_f32[s].astype(BF16)
        wd_bf[...] = wd_f32[s].astype(BF16)

        @pl.when(wnext_ref[i] >= 0)
        def _():
            for c in weight_copies(wnext_ref[i], 1 - s):
                c.start(priority=1)

    steady = jnp.logical_and(i >= 1, i + 2 < n_used)

    @pl.when(steady)
    def _():
        gbase, sbase = src_base(i + 2), dst_base(i - 1)
        burst = rows // MOE_COL_TILES
        width = D_FF // MOE_COL_TILES
        x = _load_token_tiles(xbuf.at[cur], rows).astype(BF16)
        acts = []
        for t in range(MOE_COL_TILES):
            cg = slice(t * width, (t + 1) * width)
            cu = slice(D_FF + t * width, D_FF + (t + 1) * width)
            g = jnp.dot(x, wgu_bf[:, cg], preferred_element_type=F32) + bgu_ref[:, cg]
            u = jnp.dot(x, wgu_bf[:, cu], preferred_element_type=F32) + bgu_ref[:, cu]
            for r in range(t * burst, (t + 1) * burst):
                gather_copy(gbase, r, far).start(priority=0)
            acts.append(swiglu(g, u))
        act = jnp.concatenate(acts, axis=1)
        out_width = wd_bf.shape[1] // MOE_COL_TILES
        for t in range(MOE_COL_TILES):
            co = slice(t * out_width, (t + 1) * out_width)
            y_t = jnp.dot(act, wd_bf[:, co], preferred_element_type=F32) + bd_ref[:, co]
            for r in range(t * burst, (t + 1) * burst):
                scatter_copy(sbase, r, far).start(priority=r % DMA_THREADS)
            for j in range(out_width // LANES):
                ybuf[cur, pl.ds(t * (out_width // LANES) + j, rows, stride=SUBLANES), :] = (
                    y_t[:, j * LANES:(j + 1) * LANES])

    @pl.when(jnp.logical_and(i < n_used, jnp.logical_not(steady)))
    def _():
        @pl.when(i + 2 < n_used)
        def _():
            _start_tile_gather(src_smem, src_base(i + 2), rows, h2_ref, xbuf.at[far], gsem.at[far])

        x = _load_token_tiles(xbuf.at[cur], rows).astype(BF16)
        gu = jnp.dot(x, wgu_bf[...], preferred_element_type=F32) + bgu_ref[...]
        act = swiglu(gu[:, :D_FF], gu[:, D_FF:])
        y = jnp.dot(act, wd_bf[...], preferred_element_type=F32) + bd_ref[...]
        _store_token_tiles(ybuf.at[cur], y)

        @pl.when(i >= 1)
        def _():
            _start_tile_scatter(dst_smem, dst_base(i - 1), rows, ybuf.at[far], y_ref, ssem.at[far])

    @pl.when(i == n_used)
    def _():
        _start_tile_scatter(dst_smem, dst_base(i - 1), rows, ybuf.at[far], y_ref, ssem.at[far])

        @pl.when(i >= 2)
        def _():
            wait_scatter(lax.rem(i + 1, MOE_BUFFERS))

        wait_scatter(far)

    @pl.when(i + 3 < n_used)
    def _():
        for c in idx_copies(i + 3):
            c.wait()


def _moe(block_exp, n_used, weight_slot, next_exp, src_rows, dst_rows, h2_tiles, w_gate_up, b_gate_up, w_down,
         b_down, rows, n_assign):
    n_blocks = src_rows.shape[0]
    n_exp, d_model, ff2 = w_gate_up.shape
    by_expert = lambda i, be, nu, ws, ne: (be[i], 0, 0)
    hbm = pl.BlockSpec(memory_space=pl.ANY)
    grid_spec = pltpu.PrefetchScalarGridSpec(
        num_scalar_prefetch=4,
        grid=(n_blocks,),
        in_specs=[
            hbm, hbm, hbm,
            hbm,
            pl.BlockSpec((None, 1, ff2), by_expert),
            hbm,
            pl.BlockSpec((None, 1, d_model), by_expert),
        ],
        out_specs=hbm,
        scratch_shapes=[
            pltpu.SMEM((SRC_WINDOWS * rows,), jnp.int32),
            pltpu.SMEM((DST_WINDOWS * rows,), jnp.int32),
            pltpu.VMEM((MOE_BUFFERS, rows * SUBLANES, LANES), F32),
            pltpu.VMEM((MOE_BUFFERS, rows * SUBLANES, LANES), F32),
            pltpu.VMEM((2, d_model, ff2), F32),
            pltpu.VMEM((2, ff2 // 2, d_model), F32),
            pltpu.VMEM((d_model, ff2), BF16),
            pltpu.VMEM((ff2 // 2, d_model), BF16),
            pltpu.SemaphoreType.DMA((2, DST_WINDOWS)),
            pltpu.SemaphoreType.DMA((MOE_BUFFERS,)),
            pltpu.SemaphoreType.DMA((MOE_BUFFERS,)),
            pltpu.SemaphoreType.DMA((2, 2)),
        ],
    )
    return pl.pallas_call(
        functools.partial(_moe_kernel, rows=rows, n_assign=n_assign),
        grid_spec=grid_spec,
        out_shape=jax.ShapeDtypeStruct(((n_assign + 2 * rows) * SUBLANES, LANES), F32),
        compiler_params=_cparams(("arbitrary",)),
        name="moe",
    )(block_exp, n_used, weight_slot, next_exp, src_rows, dst_rows, h2_tiles, w_gate_up, b_gate_up[:, None, :],
      w_down, b_down[:, None, :])


def _combine_kernel(y_ref, gate_ref, x1_ref, g_ref, o_ref, *, tm):
    gate = gate_ref[...]
    out = x1_ref[...]
    stride = TOP_K * SUBLANES
    for k in range(TOP_K):
        y_k = jnp.concatenate(
            [y_ref[pl.ds(k * SUBLANES + j, tm, stride=stride), :] for j in range(SUBLANES)], axis=1)
        out = out + gate[:, k:k + 1] * y_k
    ms = jnp.mean(out * out, axis=-1, keepdims=True)
    o_ref[...] = out * lax.rsqrt(ms + EPS) * g_ref[...]


def _combine(y_tiles, gate, x1, g_final, tm):
    tokens, d_model = x1.shape
    row = lambda i: (i, 0)
    return pl.pallas_call(
        functools.partial(_combine_kernel, tm=tm),
        grid=(tokens // tm,),
        in_specs=[
            pl.BlockSpec((tm * TOP_K * SUBLANES, LANES), row),
            pl.BlockSpec((tm, LANES), row),
            pl.BlockSpec((tm, d_model), row),
            pl.BlockSpec((1, d_model), lambda i: (0, 0)),
        ],
        out_specs=pl.BlockSpec((tm, d_model), row),
        out_shape=jax.ShapeDtypeStruct((tokens, d_model), F32),
        compiler_params=_cparams(("arbitrary",)),
        name="combine",
    )(y_tiles, gate, x1, g_final.astype(F32)[None, :])


PROJ_ROWS = 512
MID_ROWS = 1024
ATTN_Q_ROWS = 1024
ATTN_K_ROWS = 512
WINDOW_BLOCKS_PER_STEP = 16
MOE_ROWS = 256
COMBINE_ROWS = 256


def _layer(x2d, batch, seq, g_mix, w_in, qa_norm, ka_norm, sink_b, w_branch_a, w_branch_b, w_out,
           g_ffn, w_router, b_router, w_gate_up, b_gate_up, w_down, b_down, g_final):
    tokens = x2d.shape[0]
    qa, ka, vta, qb, kb, vtb, sga, sgb = _project(x2d, g_mix, w_in, qa_norm, ka_norm, seq,
                                                  min(PROJ_ROWS, seq), min(ATTN_K_ROWS, seq // 2))
    oa = _attention_global(qa, ka, vta, batch, seq, ATTN_Q_ROWS)
    ob = _attention_window(qb, kb, vtb, sink_b, batch, seq)
    x1, h2_tiles, route, gate, cnt = _mid(
        oa, ob, sga, sgb, x2d, w_branch_a, w_branch_b, w_out, g_ffn, w_router, b_router, min(MID_ROWS, tokens))

    rows = MOE_ROWS
    n_assign = tokens * TOP_K
    n_blocks = n_assign // rows + N_EXPERTS
    counts = cnt[:, 0].astype(jnp.int32)
    padded = (counts + rows - 1) // rows * rows
    ends = jnp.cumsum(padded)
    starts = ends - padded
    eidx, rank = route[:TOP_K], route[SUBLANES:SUBLANES + TOP_K]
    seg_start = jnp.sum(jnp.where(eidx[None] == jnp.arange(N_EXPERTS)[:, None, None], starts[:, None, None], 0),
                        axis=0)
    dest = seg_start + rank
    gap_lo = jnp.concatenate([starts + counts, ends[-1:]]).astype(jnp.int32)
    gap_hi = jnp.concatenate([ends, jnp.full((1,), n_blocks * rows)]).astype(jnp.int32)
    row_assign = _invert_routes(dest.reshape(-1), gap_lo, gap_hi, n_blocks * rows, tokens).reshape(n_blocks, rows)
    real = row_assign >= 0
    dump_slot = (n_assign + (jnp.arange(n_blocks, dtype=jnp.int32) % 2)[:, None] * rows
                 + jnp.arange(rows, dtype=jnp.int32)[None, :])
    assert TOP_K & (TOP_K - 1) == 0
    src_rows = jnp.where(real, lax.shift_right_logical(row_assign, TOP_K.bit_length() - 1), 0) * SUBLANES
    dst_rows = jnp.where(real, row_assign, dump_slot) * SUBLANES
    block_start = jnp.arange(n_blocks, dtype=jnp.int32) * rows
    block_exp = jnp.minimum(
        jnp.sum((ends[None, :] <= block_start[:, None]).astype(jnp.int32), axis=1), N_EXPERTS - 1)
    n_used = (ends[-1:] // rows).astype(jnp.int32)

    block_id = jnp.arange(n_blocks, dtype=jnp.int32)
    first_of_expert = jnp.logical_or(block_id == 0, block_exp != jnp.roll(block_exp, 1))
    weight_slot = ((jnp.cumsum(first_of_expert.astype(jnp.int32)) - 1) % 2).astype(jnp.int32)
    after_segment = ends[block_exp] // rows
    next_exp = jnp.where(after_segment < n_used[0],
                         block_exp[jnp.minimum(after_segment, n_blocks - 1)], -1).astype(jnp.int32)

    y_tiles = _moe(block_exp, n_used, weight_slot, next_exp, src_rows, dst_rows, h2_tiles,
                   w_gate_up, b_gate_up, w_down, b_down, rows, n_assign)
    return _combine(y_tiles, gate, x1, g_final, min(COMBINE_ROWS, tokens))


def kernel(x, g_mix, w_in, qa_norm, ka_norm, sink_b, w_branch_a, w_branch_b, w_out, g_ffn, w_router, b_router,
           w_gate_up, b_gate_up, w_down, b_down, g_final):
    batch, seq, d_model = x.shape
    assert g_mix.shape[0] == 1, "single-layer block"
    out = _layer(x.reshape(batch * seq, d_model), batch, seq, g_mix[0], w_in[0], qa_norm[0], ka_norm[0],
                 sink_b[0], w_branch_a[0], w_branch_b[0], w_out[0], g_ffn[0], w_router[0], b_router[0],
                 w_gate_up[0], b_gate_up[0], w_down[0], b_down[0], g_final)
    return out.reshape(batch, seq, d_model)
```

```python
import functools

import jax
import jax.numpy as jnp
from jax import lax
from jax.experimental import pallas as pl
from jax.experimental.pallas import tpu as pltpu

HEAD_DIM = 64
Q_HEADS = 8
KV_HEADS = 2
GROUPS = Q_HEADS // KV_HEADS
Q_COLS = Q_HEADS * HEAD_DIM
KV_COLS = KV_HEADS * HEAD_DIM
GROUP_COLS = GROUPS * HEAD_DIM
WINDOW = 128
GRID_W = 64
ROPE_THETA = 10000.0
N_EXPERTS = 32
TOP_K = 4
D_FF = 1024
SWIGLU_LIMIT = 7.0
SWIGLU_ALPHA = 1.702
EPS = 1e-6
NEG_INF = -1e30
LOG2_E = 1.4426950408889634

LANES = 128
SUBLANES = 8
V_ROWS = 80
VMEM_LIMIT = 56 * 1024 * 1024

F32 = jnp.float32
BF16 = jnp.bfloat16


def _cparams(sem, vmem=VMEM_LIMIT):
    return pltpu.CompilerParams(dimension_semantics=sem, vmem_limit_bytes=vmem)


def _rope(y, cos, sin_signed):
    lane = lax.broadcasted_iota(jnp.int32, y.shape, 1)
    first_half = (lane & 32) == 0
    rot = jnp.where(first_half, pltpu.roll(y, 96, 1), pltpu.roll(y, 32, 1))
    return y * cos + rot * sin_signed


def _head_rmsnorm(q, ones_blockdiag, gain):
    q2 = q * q
    hi = q2.astype(BF16)
    lo = (q2 - hi.astype(F32)).astype(BF16)
    ss = (jnp.dot(hi, ones_blockdiag, preferred_element_type=F32)
          + jnp.dot(lo, ones_blockdiag, preferred_element_type=F32))
    return q * lax.rsqrt(ss * (1.0 / HEAD_DIM) + EPS) * gain


def _proj_kernel(x_ref, g_ref, w_ref, cosa_ref, sina_ref, cos1_ref, sin1_ref, qg_ref, kg_ref, bd_ref,
                 qa_ref, ka_ref, vta_ref, qb_ref, kb_ref, vtb_ref, sga_ref, sgb_ref, *, d_model):
    x = x_ref[...]
    ms = jnp.mean(x * x, axis=-1, keepdims=True)
    h = (x * lax.rsqrt(ms + EPS) * g_ref[...]).astype(BF16)
    scale = HEAD_DIM ** -0.5 * LOG2_E

    def proj(c0, width):
        return jnp.dot(h, w_ref[:, c0:c0 + width], preferred_element_type=F32)

    cosa, sina = cosa_ref[...], sina_ref[...]
    cos1, sin1 = cos1_ref[...], sin1_ref[...]
    bd = bd_ref[...]
    slab = lambda v, s: v[:, s * LANES:(s + 1) * LANES]
    c = 0
    qa = proj(c, Q_COLS)
    for s in range(Q_COLS // LANES):
        q = _head_rmsnorm(slab(qa, s), bd, qg_ref[...])
        qa_ref[:, s * LANES:(s + 1) * LANES] = (_rope(q, cosa, sina) * scale).astype(BF16)
    c += Q_COLS
    kva = proj(c, 2 * KV_COLS)
    ka = _rope(_head_rmsnorm(slab(kva, 0), bd, kg_ref[...]), cosa, sina).astype(BF16)
    va_t = slab(kva, 1).T
    ones = jnp.ones((V_ROWS - HEAD_DIM, va_t.shape[1]), BF16)
    for hd in range(KV_HEADS):
        ka_ref[hd] = ka[:, hd * HEAD_DIM:(hd + 1) * HEAD_DIM]
        vta_ref[hd, :HEAD_DIM, :] = va_t[hd * HEAD_DIM:(hd + 1) * HEAD_DIM, :].astype(BF16)
        vta_ref[hd, HEAD_DIM:, :] = ones
    c += 2 * KV_COLS
    qb = proj(c, Q_COLS)
    for s in range(Q_COLS // LANES):
        qb_ref[:, s * LANES:(s + 1) * LANES] = (_rope(slab(qb, s), cos1, sin1) * scale).astype(BF16)
    c += Q_COLS
    kvb = proj(c, 2 * KV_COLS)
    kb = _rope(slab(kvb, 0), cos1, sin1).astype(BF16)
    vb_t = slab(kvb, 1).T
    ones = jnp.ones((V_ROWS - HEAD_DIM, WINDOW), BF16)
    for hd in range(KV_HEADS):
        kb_ref[hd] = kb[:, hd * HEAD_DIM:(hd + 1) * HEAD_DIM]
        for ch in range(vtb_ref.shape[1]):
            vtb_ref[hd, ch, :HEAD_DIM, :] = vb_t[hd * HEAD_DIM:(hd + 1) * HEAD_DIM,
                                                 ch * WINDOW:(ch + 1) * WINDOW].astype(BF16)
            vtb_ref[hd, ch, HEAD_DIM:, :] = ones
    c += 2 * KV_COLS
    sga_ref[...] = jax.nn.sigmoid(proj(c, d_model)).astype(BF16)
    c += d_model
    sgb_ref[...] = jax.nn.sigmoid(proj(c, d_model)).astype(BF16)


def _rope_tables(seq):
    def slab(ang):
        cos, sin = jnp.cos(ang), jnp.sin(ang)
        cos64 = jnp.concatenate([cos, cos], axis=-1)
        sin64 = jnp.concatenate([-sin, sin], axis=-1)
        return jnp.tile(cos64, (1, 2)), jnp.tile(sin64, (1, 2))

    pos = jnp.arange(seq, dtype=F32)
    inv1 = ROPE_THETA ** (-jnp.arange(0, HEAD_DIM, 2, dtype=F32) / HEAD_DIM)
    ang1 = pos[:, None] * inv1[None, :]
    rows = seq // GRID_W
    row = jnp.broadcast_to(jnp.arange(rows, dtype=F32)[:, None], (rows, GRID_W)).reshape(-1)
    col = jnp.broadcast_to(jnp.arange(GRID_W, dtype=F32)[None, :], (rows, GRID_W)).reshape(-1)
    axis_dim = HEAD_DIM // 2
    inva = ROPE_THETA ** (-jnp.arange(0, axis_dim, 2, dtype=F32) / axis_dim)
    anga = jnp.concatenate([row[:, None] * inva[None, :], col[:, None] * inva[None, :]], axis=-1)
    return slab(anga), slab(ang1)


def _project(x2d, g_mix, w_in, qa_norm, ka_norm, seq, tm, tk):
    tokens, d_model = x2d.shape
    in_cols = w_in.shape[1]
    (cosa, sina), (cos1, sin1) = _rope_tables(seq)
    head_id = jnp.arange(LANES) // HEAD_DIM
    ones_blockdiag = (head_id[:, None] == head_id[None, :]).astype(BF16)
    qg = jnp.tile(qa_norm.astype(F32), 2)[None, :]
    kg = jnp.tile(ka_norm.astype(F32), 2)[None, :]
    seq_tiles = seq // tm
    row = lambda i: (i, 0)
    pos = lambda i: (i % seq_tiles, 0)
    fixed = lambda i: (0, 0)
    batch = tokens // seq
    assert tk % tm == 0 and tm % WINDOW == 0
    sub = tk // tm
    bf = lambda *shape: jax.ShapeDtypeStruct(shape, BF16)
    tok_spec = lambda cols: pl.BlockSpec((tm, cols), row)
    k_spec = pl.BlockSpec((None, KV_HEADS, tm, HEAD_DIM), lambda i: (i // seq_tiles, 0, i % seq_tiles, 0))
    vta_spec = pl.BlockSpec((None, KV_HEADS, None, V_ROWS, tm),
                            lambda i: (i // seq_tiles, 0, (i % seq_tiles) // sub, 0, (i % seq_tiles) % sub))
    vtb_spec = pl.BlockSpec((None, KV_HEADS, tm // WINDOW, V_ROWS, WINDOW),
                            lambda i: (i // seq_tiles, 0, i % seq_tiles, 0, 0))
    k_shape = bf(batch, KV_HEADS, seq, HEAD_DIM)
    return pl.pallas_call(
        functools.partial(_proj_kernel, d_model=d_model),
        grid=(tokens // tm,),
        in_specs=[
            pl.BlockSpec((tm, d_model), row),
            pl.BlockSpec((1, d_model), fixed),
            pl.BlockSpec((d_model, in_cols), fixed),
            pl.BlockSpec((tm, LANES), pos), pl.BlockSpec((tm, LANES), pos),
            pl.BlockSpec((tm, LANES), pos), pl.BlockSpec((tm, LANES), pos),
            pl.BlockSpec((1, LANES), fixed), pl.BlockSpec((1, LANES), fixed),
            pl.BlockSpec((LANES, LANES), fixed),
        ],
        out_specs=[tok_spec(Q_COLS), k_spec, vta_spec, tok_spec(Q_COLS), k_spec, vtb_spec,
                   tok_spec(d_model), tok_spec(d_model)],
        out_shape=[bf(tokens, Q_COLS), k_shape, bf(batch, KV_HEADS, seq // tk, V_ROWS, tk),
                   bf(tokens, Q_COLS), k_shape, bf(batch, KV_HEADS, seq // WINDOW, V_ROWS, WINDOW),
                   bf(tokens, d_model), bf(tokens, d_model)],
        compiler_params=_cparams(("arbitrary",)),
        name="proj",
    )(x2d, g_mix.astype(F32)[None, :], w_in.astype(BF16), cosa, sina, cos1, sin1, qg, kg, ones_blockdiag)


def _stack_group_queries(q):
    return jnp.concatenate([q[:, g * HEAD_DIM:(g + 1) * HEAD_DIM] for g in range(GROUPS)], axis=0)


def _unstack_group_outputs(o_t, tq):
    pad = jnp.zeros((LANES - HEAD_DIM, o_t.shape[1]), o_t.dtype)
    o_pad = jnp.concatenate([o_t, pad], axis=0)
    heads = [o_pad[:, g * tq:(g + 1) * tq].T[:, :HEAD_DIM] for g in range(GROUPS)]
    return jnp.concatenate(heads, axis=1)


def _attn_global_kernel(q_ref, k_ref, vt_ref, o_ref, s0, s1, p0, p1, acc_ref, *, tq, tk, n_chunks):
    q_st = _stack_group_queries(q_ref[...])
    m_rows = q_st.shape[0]

    def scores(c, s_ref):
        k_c = k_ref[pl.ds(pl.multiple_of(c * tk, tk), tk), :]
        s_ref[...] = lax.dot_general(k_c, q_st, (((1,), (1,)), ((), ())), preferred_element_type=F32)

    def softmax(s_ref, p_ref, m_prev):
        m_new = jnp.maximum(m_prev, jnp.max(s_ref[...], axis=0, keepdims=True))
        p_ref[...] = jnp.exp2(s_ref[...] - m_new).astype(BF16)
        return m_new, jnp.exp2(m_prev - m_new)

    def values(c, p_ref, alpha):
        acc_ref[...] = acc_ref[...] * alpha + jnp.dot(vt_ref[c], p_ref[...], preferred_element_type=F32)

    scores(0, s0)
    p1[...] = jnp.zeros(p1.shape, BF16)
    acc_ref[...] = jnp.zeros(acc_ref.shape, F32)

    def pair(j, carry, more_chunks=True):
        m, alpha = carry
        c = 2 * j
        scores(c + 1, s1)
        m_a, alpha_a = softmax(s0, p0, m)
        values(jnp.maximum(c - 1, 0), p1, alpha)
        if more_chunks:
            scores(c + 2, s0)
        m_b, alpha_b = softmax(s1, p1, m_a)
        values(c, p0, alpha_a)
        return m_b, alpha_b

    init = (jnp.full((1, m_rows), NEG_INF, F32), jnp.ones((1, m_rows), F32))
    carry = lax.fori_loop(0, n_chunks // 2 - 1, pair, init)
    _, alpha = pair(n_chunks // 2 - 1, carry, more_chunks=False)
    values(n_chunks - 1, p1, alpha)
    acc = acc_ref[...]
    o_t = acc[:HEAD_DIM, :] / acc[HEAD_DIM:HEAD_DIM + 1, :]
    o_ref[...] = _unstack_group_outputs(o_t, tq).astype(o_ref.dtype)


def _attn_window_kernel(q_ref, k_ref, vt_ref, sink_ref, band_ref, o_ref, *, tq, n_blocks, blocks_per_step):
    span = 3 * tq
    sink = sink_ref[...]
    for i in range(blocks_per_step):
        n = pl.program_id(2) * blocks_per_step + i
        c0 = jnp.clip(n - 1, 0, n_blocks - 3)
        start = pl.multiple_of(c0 * tq, tq)
        q_st = _stack_group_queries(q_ref[i * tq:(i + 1) * tq, :])
        k_w = k_ref[pl.ds(start, span), :]
        s = lax.dot_general(k_w, q_st, (((1,), (1,)), ((), ())), preferred_element_type=F32)
        s = s + band_ref[n - c0]
        m = jnp.maximum(jnp.max(s, axis=0, keepdims=True), sink)
        p = jnp.exp2(s - m).astype(BF16)
        acc = jnp.dot(vt_ref[c0], p[:tq], preferred_element_type=F32)
        for j in range(1, 3):
            acc += jnp.dot(vt_ref[c0 + j], p[j * tq:(j + 1) * tq], preferred_element_type=F32)
        denom = acc[HEAD_DIM:HEAD_DIM + 1, :] + jnp.exp2(sink - m)
        o_ref[i * tq:(i + 1) * tq, :] = _unstack_group_outputs(acc[:HEAD_DIM, :] / denom, tq).astype(o_ref.dtype)


def _attention_global(q2d, k_hm, v_t, batch, seq, tq):
    tokens = q2d.shape[0]
    n_chunks, tk = v_t.shape[2], v_t.shape[4]
    assert n_chunks % 2 == 0, "the chunk pipeline advances two chunks per step"
    q_tiles = seq // tq
    m_rows = GROUPS * tq
    qmap = lambda b, h, i: (b * q_tiles + i, h)
    return pl.pallas_call(
        functools.partial(_attn_global_kernel, tq=tq, tk=tk, n_chunks=n_chunks),
        grid=(batch, KV_HEADS, q_tiles),
        in_specs=[
            pl.BlockSpec((tq, GROUP_COLS), qmap),
            pl.BlockSpec((None, None, seq, HEAD_DIM), lambda b, h, i: (b, h, 0, 0)),
            pl.BlockSpec((None, None, n_chunks, V_ROWS, tk), lambda b, h, i: (b, h, 0, 0, 0)),
        ],
        out_specs=pl.BlockSpec((tq, GROUP_COLS), qmap),
        out_shape=jax.ShapeDtypeStruct((tokens, Q_COLS), BF16),
        scratch_shapes=[pltpu.VMEM((tk, m_rows), F32), pltpu.VMEM((tk, m_rows), F32),
                        pltpu.VMEM((tk, m_rows), BF16), pltpu.VMEM((tk, m_rows), BF16),
                        pltpu.VMEM((V_ROWS, m_rows), F32)],
        compiler_params=_cparams(("arbitrary", "arbitrary", "arbitrary")),
        name="attn_global",
    )(q2d, k_hm, v_t)


def _attention_window(q2d, k_hm, v_t, sink, batch, seq):
    tokens = q2d.shape[0]
    tq = WINDOW
    n_blocks = seq // tq
    assert n_blocks >= 3
    m_rows = GROUPS * tq
    sink_rows = jnp.repeat(sink.astype(F32).reshape(KV_HEADS, GROUPS) * LOG2_E, tq, axis=1)[:, None, :]
    per_step = WINDOW_BLOCKS_PER_STEP if n_blocks % WINDOW_BLOCKS_PER_STEP == 0 else 1
    steps = n_blocks // per_step
    key_j = jnp.arange(3 * tq)[None, :, None]
    query = (jnp.arange(m_rows) % tq)[None, None, :] + jnp.arange(3)[:, None, None] * tq
    band = jnp.where(jnp.abs(key_j - query) <= WINDOW, 0.0, NEG_INF).astype(F32)
    qmap = lambda b, h, i: (b * steps + i, h)
    return pl.pallas_call(
        functools.partial(_attn_window_kernel, tq=tq, n_blocks=n_blocks, blocks_per_step=per_step),
        grid=(batch, KV_HEADS, steps),
        in_specs=[
            pl.BlockSpec((per_step * tq, GROUP_COLS), qmap),
            pl.BlockSpec((None, None, seq, HEAD_DIM), lambda b, h, i: (b, h, 0, 0)),
            pl.BlockSpec((None, None, n_blocks, V_ROWS, tq), lambda b, h, i: (b, h, 0, 0, 0)),
            pl.BlockSpec((None, 1, m_rows), lambda b, h, i: (h, 0, 0)),
            pl.BlockSpec((3, 3 * tq, m_rows), lambda b, h, i: (0, 0, 0)),
        ],
        out_specs=pl.BlockSpec((per_step * tq, GROUP_COLS), qmap),
        out_shape=jax.ShapeDtypeStruct((tokens, Q_COLS), BF16),
        compiler_params=_cparams(("arbitrary", "arbitrary", "arbitrary")),
        name="attn_window",
    )(q2d, k_hm, v_t, sink_rows, band)


def _store_token_tiles(ref, val):
    rows = val.shape[0]
    for j in range(val.shape[1] // LANES):
        ref[pl.ds(j, rows, stride=SUBLANES), :] = val[:, j * LANES:(j + 1) * LANES]


def _load_token_tiles(ref, rows):
    return jnp.concatenate([ref[pl.ds(j, rows, stride=SUBLANES), :] for j in range(SUBLANES)], axis=1)


ROUTE_ROWS = 2 * SUBLANES


def _top_k_route(logits_t):
    eid = lax.broadcasted_iota(jnp.int32, logits_t.shape, 0)
    work = logits_t
    vals, idxs, sels = [], [], []
    for _ in range(TOP_K):
        mx = jnp.max(work, axis=0, keepdims=True)
        ix = jnp.min(jnp.where(work == mx, eid, N_EXPERTS), axis=0, keepdims=True)
        sel = eid == ix
        work = jnp.where(sel, -jnp.inf, work)
        vals.append(mx)
        idxs.append(ix)
        sels.append(sel)
    return vals, idxs, sels


def _mid_kernel(oa_ref, ob_ref, sga_ref, sgb_ref, x_ref, wa_ref, wb_ref, wo_ref, g_ref, wr_ref, br_ref, tri_ref,
                x1_ref, h2_ref, route_ref, gate_ref, cnt_ref, *, halves):
    @pl.when(pl.program_id(0) == 0)
    def _():
        cnt_ref[...] = jnp.zeros(cnt_ref.shape, F32)

    tm = x_ref.shape[0]
    hm = tm // halves
    routed = []
    for hf in range(halves):
        rs = slice(hf * hm, (hf + 1) * hm)
        a = jnp.dot(oa_ref[rs, :], wa_ref[...], preferred_element_type=F32)
        b = jnp.dot(ob_ref[rs, :], wb_ref[...], preferred_element_type=F32)
        merged = sga_ref[rs, :].astype(F32) * a + sgb_ref[rs, :].astype(F32) * b
        x1 = x_ref[rs, :] + jnp.dot(merged.astype(BF16), wo_ref[...], preferred_element_type=F32)
        x1_ref[rs, :] = x1
        ms = jnp.mean(x1 * x1, axis=-1, keepdims=True)
        h2 = x1 * lax.rsqrt(ms + EPS) * g_ref[...]
        _store_token_tiles(h2_ref.at[pl.ds(hf * hm * SUBLANES, hm * SUBLANES)], h2)
        hi = h2.astype(BF16)
        lo = (h2 - hi.astype(F32)).astype(BF16)
        nt = (((1,), (1,)), ((), ()))
        r2 = (lax.dot_general(wr_ref[...], hi, nt, preferred_element_type=F32)
              + lax.dot_general(wr_ref[...], lo, nt, preferred_element_type=F32))
        bias = jnp.concatenate([br_ref[...]] * (hm // LANES), axis=1)
        routed.append(_top_k_route(r2[:N_EXPERTS] + r2[N_EXPERTS:] + bias))

    onehots = []
    for vals, idxs, sels in routed:
        chosen = sels[0] | sels[1] | sels[2] | sels[3]
        onehots.append(jnp.where(chosen, 1.0, 0.0))
    onehot = jnp.concatenate(onehots, axis=1)
    counted = jnp.concatenate([cnt_ref[...]] * (tm // LANES), axis=1)
    before = jnp.dot(onehot.astype(BF16), tri_ref[...], preferred_element_type=F32) + counted
    zeros4 = jnp.zeros((SUBLANES - TOP_K, hm), jnp.int32)
    for hf, (vals, idxs, sels) in enumerate(routed):
        rs = slice(hf * hm, (hf + 1) * hm)
        exps = [jnp.exp(v - vals[0]) for v in vals]
        denom = exps[0] + exps[1] + exps[2] + exps[3]
        ranks = [jnp.sum(jnp.where(sels[k], before[:, rs], 0.0), axis=0, keepdims=True).astype(jnp.int32)
                 for k in range(TOP_K)]
        route_ref[:, rs] = jnp.concatenate(idxs + [zeros4] + ranks + [zeros4], axis=0)
        gates_t = jnp.concatenate([e / denom for e in exps] + [jnp.zeros((LANES - TOP_K, hm), F32)], axis=0)
        gate_ref[rs, :] = gates_t.T
    cnt_ref[...] = cnt_ref[...] + jnp.sum(onehot, axis=1, keepdims=True)


def _mid(oa, ob, sga, sgb, x2d, w_a, w_b, w_o, g_ffn, w_router, b_router, tm):
    tokens, d_model = x2d.shape
    wr_t = w_router.astype(F32).T
    wr_hi = wr_t.astype(BF16)
    wr_lo = (wr_t - wr_hi.astype(F32)).astype(BF16)
    wr_split = jnp.concatenate([wr_hi, wr_lo], axis=0)
    br = jnp.broadcast_to(b_router.astype(F32)[:, None], (N_EXPERTS, LANES))
    tri = (jnp.arange(tm)[:, None] < jnp.arange(tm)[None, :]).astype(BF16)
    halves = 2 if tm % (2 * LANES) == 0 else 1
    row = lambda i: (i, 0)
    fixed = lambda i: (0, 0)
    return pl.pallas_call(
        functools.partial(_mid_kernel, halves=halves),
        grid=(tokens // tm,),
        in_specs=[
            pl.BlockSpec((tm, Q_COLS), row), pl.BlockSpec((tm, Q_COLS), row),
            pl.BlockSpec((tm, d_model), row), pl.BlockSpec((tm, d_model), row),
            pl.BlockSpec((tm, d_model), row),
            pl.BlockSpec((Q_COLS, d_model), fixed), pl.BlockSpec((Q_COLS, d_model), fixed),
            pl.BlockSpec((d_model, d_model), fixed),
            pl.BlockSpec((1, d_model), fixed),
            pl.BlockSpec((2 * N_EXPERTS, d_model), fixed),
            pl.BlockSpec((N_EXPERTS, LANES), fixed),
            pl.BlockSpec((tm, tm), fixed),
        ],
        out_specs=[
            pl.BlockSpec((tm, d_model), row),
            pl.BlockSpec((tm * SUBLANES, LANES), row),
            pl.BlockSpec((ROUTE_ROWS, tm), lambda i: (0, i)),
            pl.BlockSpec((tm, LANES), row),
            pl.BlockSpec((N_EXPERTS, LANES), fixed),
        ],
        out_shape=[
            jax.ShapeDtypeStruct((tokens, d_model), F32),
            jax.ShapeDtypeStruct((tokens * SUBLANES, LANES), F32),
            jax.ShapeDtypeStruct((ROUTE_ROWS, tokens), jnp.int32),
            jax.ShapeDtypeStruct((tokens, LANES), F32),
            jax.ShapeDtypeStruct((N_EXPERTS, LANES), F32),
        ],
        compiler_params=_cparams(("arbitrary",)),
        name="mid",
    )(oa, ob, sga, sgb, x2d, w_a.astype(BF16), w_b.astype(BF16), w_o.astype(BF16),
      g_ffn.astype(F32)[None, :], wr_split, br, tri)


def _invert_kernel(gap_lo_ref, gap_hi_ref, dest_ref, out_ref, *, tokens):
    def clear(r, carry):
        out_ref[r] = -1
        return carry
    for g in range(gap_lo_ref.shape[0]):
        lax.fori_loop(gap_lo_ref[g], gap_hi_ref[g], clear, 0)
    def put(t, carry):
        for k in range(TOP_K):
            out_ref[dest_ref[k * tokens + t]] = t * TOP_K + k
        return carry
    lax.fori_loop(0, tokens, put, 0, unroll=8)


def _invert_routes(dest_flat, gap_lo, gap_hi, n_rows, tokens):
    smem = pl.BlockSpec(memory_space=pltpu.SMEM)
    return pl.pallas_call(
        functools.partial(_invert_kernel, tokens=tokens),
        in_specs=[smem, smem, smem],
        out_specs=smem,
        out_shape=jax.ShapeDtypeStruct((n_rows,), jnp.int32),
        name="invert_routes",
    )(gap_lo, gap_hi, dest_flat)


def _tile_copy(src_ref, src_row, dst_ref, dst_row, sem):
    return pltpu.make_async_copy(
        src_ref.at[pl.ds(pl.multiple_of(src_row, SUBLANES), SUBLANES)],
        dst_ref.at[pl.ds(pl.multiple_of(dst_row, SUBLANES), SUBLANES)], sem)


def _start_tile_gather(off_smem, base, n_rows, src_ref, dst_ref, sem):
    def start(r, carry):
        _tile_copy(src_ref, off_smem[base + r], dst_ref, r * SUBLANES, sem).start()
        return carry
    lax.fori_loop(0, n_rows, start, 0, unroll=8)


def _start_tile_scatter(off_smem, base, n_rows, src_ref, dst_ref, sem):
    def start(r, carry):
        _tile_copy(src_ref, r * SUBLANES, dst_ref, off_smem[base + r], sem).start()
        return carry
    lax.fori_loop(0, n_rows, start, 0, unroll=8)


def _wait_tiles(n_rows, src_ref, dst_ref, sem):
    window = pl.ds(0, n_rows * SUBLANES)
    pltpu.make_async_copy(src_ref.at[window], dst_ref.at[window], sem).wait()


SRC_WINDOWS = 3
DST_WINDOWS = 5
MOE_BUFFERS = 3
MOE_COL_TILES = 4
DMA_THREADS = 2


def _moe_kernel(bexp_ref, nused_ref, wslot_ref, wnext_ref, src_ref, dst_ref, h2_ref, wgu_ref, bgu_ref, wd_ref, bd_ref,
                y_ref, src_smem, dst_smem, xbuf, ybuf, wgu_f32, wd_f32, wgu_bf, wd_bf, isem, gsem, ssem, wsem,
                *, rows, n_assign):
    i = pl.program_id(0)
    n_used = nused_ref[0]
    cur = lax.rem(i, MOE_BUFFERS)
    far = lax.rem(i + 2, MOE_BUFFERS)
    src_base = lambda blk: lax.rem(blk, SRC_WINDOWS) * rows
    dst_base = lambda blk: lax.rem(blk, DST_WINDOWS) * rows

    def idx_copies(blk):
        sw = pl.ds(pl.multiple_of(src_base(blk), rows), rows)
        dw = pl.ds(pl.multiple_of(dst_base(blk), rows), rows)
        return (pltpu.make_async_copy(src_ref.at[blk], src_smem.at[sw], isem.at[0, lax.rem(blk, SRC_WINDOWS)]),
                pltpu.make_async_copy(dst_ref.at[blk], dst_smem.at[dw], isem.at[1, lax.rem(blk, DST_WINDOWS)]))

    def fetch_idx_now(blk):
        for c in idx_copies(blk):
            c.start()
        for c in idx_copies(blk):
            c.wait()

    def gather_copy(blk_base, r, s):
        return _tile_copy(h2_ref, src_smem[blk_base + r], xbuf.at[s], r * SUBLANES, gsem.at[s])

    def scatter_copy(blk_base, r, s):
        return _tile_copy(ybuf.at[s], r * SUBLANES, y_ref, dst_smem[blk_base + r], ssem.at[s])

    def wait_scatter(s):
        _wait_tiles(rows, ybuf.at[s], y_ref, ssem.at[s])

    def swiglu(g, u):
        gate = jnp.minimum(g, SWIGLU_LIMIT)
        up = jnp.clip(u, -SWIGLU_LIMIT, SWIGLU_LIMIT)
        return ((up + 1.0) * (gate * jax.nn.sigmoid(SWIGLU_ALPHA * gate))).astype(BF16)

    @pl.when(i == 0)
    def _():
        ybuf[...] = jnp.zeros(ybuf.shape, F32)
        for s in range(2):
            dump = pl.ds((n_assign + s * rows) * SUBLANES, rows * SUBLANES)
            zero_fill = pltpu.make_async_copy(ybuf.at[s], y_ref.at[dump], ssem.at[s])
            zero_fill.start()
            zero_fill.wait()
        fetch_idx_now(0)
        _start_tile_gather(src_smem, 0, rows, h2_ref, xbuf.at[0], gsem.at[0])

        @pl.when(1 < n_used)
        def _():
            fetch_idx_now(1)
            _start_tile_gather(src_smem, src_base(1), rows, h2_ref, xbuf.at[1], gsem.at[1])

        @pl.when(2 < n_used)
        def _():
            fetch_idx_now(2)

    @pl.when(i + 3 < n_used)
    def _():
        for c in idx_copies(i + 3):
            c.start()

    @pl.when(i < n_used)
    def _():
        _wait_tiles(rows, h2_ref, xbuf.at[cur], gsem.at[cur])

    @pl.when(jnp.logical_and(i >= 3, i <= n_used))
    def _():
        wait_scatter(cur)

    new_expert = jnp.logical_or(i == 0, bexp_ref[i] != bexp_ref[jnp.maximum(i - 1, 0)])

    def weight_copies(expert, s):
        return (pltpu.make_async_copy(wgu_ref.at[expert], wgu_f32.at[s], wsem.at[0, s]),
                pltpu.make_async_copy(wd_ref.at[expert], wd_f32.at[s], wsem.at[1, s]))

    @pl.when(jnp.logical_and(new_expert, i < n_used))
    def _():
        s = wslot_ref[i]

        @pl.when(i == 0)
        def _():
            for c in weight_copies(bexp_ref[0], 0):
                c.start()

        for c in weight_copies(bexp_ref[i], s):
            c.wait()
        wgu_bf[...] = wgu_f32[s].astype(BF16)
        wd_bf[...] = wd_f32[s].astype(BF16)

        @pl.when(wnext_ref[i] >= 0)
        def _():
            for c in weight_copies(wnext_ref[i], 1 - s):
                c.start(priority=1)

    steady = jnp.logical_and(i >= 1, i + 2 < n_used)

    @pl.when(steady)
    def _():
        gbase, sbase = src_base(i + 2), dst_base(i - 1)
        burst = rows // MOE_COL_TILES
        width = D_FF // MOE_COL_TILES
        x = _load_token_tiles(xbuf.at[cur], rows).astype(BF16)
        acts = []
        for t in range(MOE_COL_TILES):
            cg = slice(t * width, (t + 1) * width)
            cu = slice(D_FF + t * width, D_FF + (t + 1) * width)
            g = jnp.dot(x, wgu_bf[:, cg], preferred_element_type=F32) + bgu_ref[:, cg]
            u = jnp.dot(x, wgu_bf[:, cu], preferred_element_type=F32) + bgu_ref[:, cu]
            for r in range(t * burst, (t + 1) * burst):
                gather_copy(gbase, r, far).start(priority=0)
            acts.append(swiglu(g, u))
        act = jnp.concatenate(acts, axis=1)
        out_width = wd_bf.shape[1] // MOE_COL_TILES
        for t in range(MOE_COL_TILES):
            co = slice(t * out_width, (t + 1) * out_width)
            y_t = jnp.dot(act, wd_bf[:, co], preferred_element_type=F32) + bd_ref[:, co]
            for r in range(t * burst, (t + 1) * burst):
                scatter_copy(sbase, r, far).start(priority=r % DMA_THREADS)
            for j in range(out_width // LANES):
                ybuf[cur, pl.ds(t * (out_width // LANES) + j, rows, stride=SUBLANES), :] = (
                    y_t[:, j * LANES:(j + 1) * LANES])

    @pl.when(jnp.logical_and(i < n_used, jnp.logical_not(steady)))
    def _():
        @pl.when(i + 2 < n_used)
        def _():
            _start_tile_gather(src_smem, src_base(i + 2), rows, h2_ref, xbuf.at[far], gsem.at[far])

        x = _load_token_tiles(xbuf.at[cur], rows).astype(BF16)
        gu = jnp.dot(x, wgu_bf[...], preferred_element_type=F32) + bgu_ref[...]
        act = swiglu(gu[:, :D_FF], gu[:, D_FF:])
        y = jnp.dot(act, wd_bf[...], preferred_element_type=F32) + bd_ref[...]
        _store_token_tiles(ybuf.at[cur], y)

        @pl.when(i >= 1)
        def _():
            _start_tile_scatter(dst_smem, dst_base(i - 1), rows, ybuf.at[far], y_ref, ssem.at[far])

    @pl.when(i == n_used)
    def _():
        _start_tile_scatter(dst_smem, dst_base(i - 1), rows, ybuf.at[far], y_ref, ssem.at[far])

        @pl.when(i >= 2)
        def _():
            wait_scatter(lax.rem(i + 1, MOE_BUFFERS))

        wait_scatter(far)

    @pl.when(i + 3 < n_used)
    def _():
        for c in idx_copies(i + 3):
            c.wait()


def _moe(block_exp, n_used, weight_slot, next_exp, src_rows, dst_rows, h2_tiles, w_gate_up, b_gate_up, w_down,
         b_down, rows, n_assign):
    n_blocks = src_rows.shape[0]
    n_exp, d_model, ff2 = w_gate_up.shape
    by_expert = lambda i, be, nu, ws, ne: (be[i], 0, 0)
    hbm = pl.BlockSpec(memory_space=pl.ANY)
    grid_spec = pltpu.PrefetchScalarGridSpec(
        num_scalar_prefetch=4,
        grid=(n_blocks,),
        in_specs=[
            hbm, hbm, hbm,
            hbm,
            pl.BlockSpec((None, 1, ff2), by_expert),
            hbm,
            pl.BlockSpec((None, 1, d_model), by_expert),
        ],
        out_specs=hbm,
        scratch_shapes=[
            pltpu.SMEM((SRC_WINDOWS * rows,), jnp.int32),
            pltpu.SMEM((DST_WINDOWS * rows,), jnp.int32),
            pltpu.VMEM((MOE_BUFFERS, rows * SUBLANES, LANES), F32),
            pltpu.VMEM((MOE_BUFFERS, rows * SUBLANES, LANES), F32),
            pltpu.VMEM((2, d_model, ff2), F32),
            pltpu.VMEM((2, ff2 // 2, d_model), F32),
            pltpu.VMEM((d_model, ff2), BF16),
            pltpu.VMEM((ff2 // 2, d_model), BF16),
            pltpu.SemaphoreType.DMA((2, DST_WINDOWS)),
            pltpu.SemaphoreType.DMA((MOE_BUFFERS,)),
            pltpu.SemaphoreType.DMA((MOE_BUFFERS,)),
            pltpu.SemaphoreType.DMA((2, 2)),
        ],
    )
    return pl.pallas_call(
        functools.partial(_moe_kernel, rows=rows, n_assign=n_assign),
        grid_spec=grid_spec,
        out_shape=jax.ShapeDtypeStruct(((n_assign + 2 * rows) * SUBLANES, LANES), F32),
        compiler_params=_cparams(("arbitrary",)),
        name="moe",
    )(block_exp, n_used, weight_slot, next_exp, src_rows, dst_rows, h2_tiles, w_gate_up, b_gate_up[:, None, :],
      w_down, b_down[:, None, :])


def _combine_kernel(y_ref, gate_ref, x1_ref, g_ref, o_ref, *, tm):
    gate = gate_ref[...]
    out = x1_ref[...]
    stride = TOP_K * SUBLANES
    for k in range(TOP_K):
        y_k = jnp.concatenate(
            [y_ref[pl.ds(k * SUBLANES + j, tm, stride=stride), :] for j in range(SUBLANES)], axis=1)
        out = out + gate[:, k:k + 1] * y_k
    ms = jnp.mean(out * out, axis=-1, keepdims=True)
    o_ref[...] = out * lax.rsqrt(ms + EPS) * g_ref[...]


def _combine(y_tiles, gate, x1, g_final, tm):
    tokens, d_model = x1.shape
    row = lambda i: (i, 0)
    return pl.pallas_call(
        functools.partial(_combine_kernel, tm=tm),
        grid=(tokens // tm,),
        in_specs=[
            pl.BlockSpec((tm * TOP_K * SUBLANES, LANES), row),
            pl.BlockSpec((tm, LANES), row),
            pl.BlockSpec((tm, d_model), row),
            pl.BlockSpec((1, d_model), lambda i: (0, 0)),
        ],
        out_specs=pl.BlockSpec((tm, d_model), row),
        out_shape=jax.ShapeDtypeStruct((tokens, d_model), F32),
        compiler_params=_cparams(("arbitrary",)),
        name="combine",
    )(y_tiles, gate, x1, g_final.astype(F32)[None, :])


PROJ_ROWS = 512
MID_ROWS = 1024
ATTN_Q_ROWS = 1024
ATTN_K_ROWS = 512
WINDOW_BLOCKS_PER_STEP = 16
MOE_ROWS = 256
COMBINE_ROWS = 256


def _layer(x2d, batch, seq, g_mix, w_in, qa_norm, ka_norm, sink_b, w_branch_a, w_branch_b, w_out,
           g_ffn, w_router, b_router, w_gate_up, b_gate_up, w_down, b_down, g_final):
    tokens = x2d.shape[0]
    qa, ka, vta, qb, kb, vtb, sga, sgb = _project(x2d, g_mix, w_in, qa_norm, ka_norm, seq,
                                                  min(PROJ_ROWS, seq), min(ATTN_K_ROWS, seq // 2))
    oa = _attention_global(qa, ka, vta, batch, seq, ATTN_Q_ROWS)
    ob = _attention_window(qb, kb, vtb, sink_b, batch, seq)
    x1, h2_tiles, route, gate, cnt = _mid(
        oa, ob, sga, sgb, x2d, w_branch_a, w_branch_b, w_out, g_ffn, w_router, b_router, min(MID_ROWS, tokens))

    rows = MOE_ROWS
    n_assign = tokens * TOP_K
    n_blocks = n_assign // rows + N_EXPERTS
    counts = cnt[:, 0].astype(jnp.int32)
    padded = (counts + rows - 1) // rows * rows
    ends = jnp.cumsum(padded)
    starts = ends - padded
    eidx, rank = route[:TOP_K], route[SUBLANES:SUBLANES + TOP_K]
    seg_start = jnp.sum(jnp.where(eidx[None] == jnp.arange(N_EXPERTS)[:, None, None], starts[:, None, None], 0),
                        axis=0)
    dest = seg_start + rank
    gap_lo = jnp.concatenate([starts + counts, ends[-1:]]).astype(jnp.int32)
    gap_hi = jnp.concatenate([ends, jnp.full((1,), n_blocks * rows)]).astype(jnp.int32)
    row_assign = _invert_routes(dest.reshape(-1), gap_lo, gap_hi, n_blocks * rows, tokens).reshape(n_blocks, rows)
    real = row_assign >= 0
    dump_slot = (n_assign + (jnp.arange(n_blocks, dtype=jnp.int32) % 2)[:, None] * rows
                 + jnp.arange(rows, dtype=jnp.int32)[None, :])
    src_rows = jnp.where(real, row_assign // TOP_K, 0) * SUBLANES
    dst_rows = jnp.where(real, row_assign, dump_slot) * SUBLANES
    block_start = jnp.arange(n_blocks, dtype=jnp.int32) * rows
    block_exp = jnp.minimum(
        jnp.sum((ends[None, :] <= block_start[:, None]).astype(jnp.int32), axis=1), N_EXPERTS - 1)
    n_used = (ends[-1:] // rows).astype(jnp.int32)

    block_id = jnp.arange(n_blocks, dtype=jnp.int32)
    first_of_expert = jnp.logical_or(block_id == 0, block_exp != jnp.roll(block_exp, 1))
    weight_slot = ((jnp.cumsum(first_of_expert.astype(jnp.int32)) - 1) % 2).astype(jnp.int32)
    after_segment = ends[block_exp] // rows
    next_exp = jnp.where(after_segment < n_used[0],
                         block_exp[jnp.minimum(after_segment, n_blocks - 1)], -1).astype(jnp.int32)

    y_tiles = _moe(block_exp, n_used, weight_slot, next_exp, src_rows, dst_rows, h2_tiles,
                   w_gate_up, b_gate_up, w_down, b_down, rows, n_assign)
    return _combine(y_tiles, gate, x1, g_final, min(COMBINE_ROWS, tokens))


def kernel(x, g_mix, w_in, qa_norm, ka_norm, sink_b, w_branch_a, w_branch_b, w_out, g_ffn, w_router, b_router,
           w_gate_up, b_gate_up, w_down, b_down, g_final):
    batch, seq, d_model = x.shape
    assert g_mix.shape[0] == 1, "single-layer block"
    out = _layer(x.reshape(batch * seq, d_model), batch, seq, g_mix[0], w_in[0], qa_norm[0], ka_norm[0],
                 sink_b[0], w_branch_a[0], w_branch_b[0], w_out[0], g_ffn[0], w_router[0], b_router[0],
                 w_gate_up[0], b_gate_up[0], w_down[0], b_down[0], g_final)
    return out.reshape(batch, seq, d_model)
```
